```python
import jax, jax.numpy as jnp
from jax import lax
import numpy as np

D_MODEL = 4096
BATCH = 1
SEQ = 8192
DEPTH = 1

CTX_LEN = 256
GRID_W = 64
D_RWKV = D_MODEL // 2
D_CONV = D_MODEL - D_RWKV
HEAD_SIZE = 64
N_HEADS_RWKV = D_RWKV // HEAD_SIZE
DECAY_LORA = 96
ICLR_LORA = 96
GATE_LORA = 256
D_FF = -(-8 * D_MODEL // (3 * 256)) * 256

OFF_K = 0
OFF_V = OFF_K + D_RWKV
OFF_WL = OFF_V + D_RWKV
OFF_AL = OFF_WL + DECAY_LORA
STATE_COLS = OFF_AL + ICLR_LORA
OFF_R = STATE_COLS
OFF_GL = OFF_R + D_RWKV
RWKV_COLS = OFF_GL + GATE_LORA
IN_COLS = RWKV_COLS + 3 * D_CONV

NORM_EPS = 1e-6
LNX_EPS = 64e-5

kernel_name = "hybrid_rwkv7_shortconv_dit_block"


def _rmsnorm(x, gain):
    xf = x.astype(jnp.float32)
    y = xf * lax.rsqrt(jnp.mean(xf * xf, axis=-1, keepdims=True) + NORM_EPS)
    return (y * gain.astype(jnp.float32)).astype(x.dtype)


def _modulate(h, shift, scale):
    return h * (1 + scale) + shift


def _neighbours(z, n_rows, row_len):
    b, t, ch = z.shape
    zr = z.reshape(b, n_rows, row_len, ch)
    prev = jnp.pad(zr[:, :, :-1], ((0, 0), (0, 0), (1, 0), (0, 0))).reshape(b, t, ch)
    nxt = jnp.pad(zr[:, :, 1:], ((0, 0), (0, 0), (0, 1), (0, 0))).reshape(b, t, ch)
    return prev, nxt


def _token_shift(z, mu, n_rows, row_len):
    prev, nxt = _neighbours(z, n_rows, row_len)
    return z + mu * (0.5 * (prev + nxt) - z)


def _short_conv(u, w, n_rows, row_len):
    prev, nxt = _neighbours(u, n_rows, row_len)
    return w[0] * prev + w[1] * u + w[2] * nxt


def _split_heads(t):
    b, s, _ = t.shape
    return t.reshape(b, s, N_HEADS_RWKV, HEAD_SIZE)


def _state_inputs(za, k_k, k_a, w0, w_decay_up, a0, w_iclr_up):
    f32 = jnp.float32
    k = za[..., OFF_K:OFF_V].astype(f32)
    v = _split_heads(za[..., OFF_V:OFF_WL].astype(f32))
    wl = jnp.tanh(za[..., OFF_WL:OFF_AL].astype(f32))
    al = za[..., OFF_AL:STATE_COLS].astype(f32)
    kk = _split_heads(k * k_k.astype(f32))
    kk = kk / jnp.maximum(jnp.sqrt(jnp.sum(kk * kk, axis=-1, keepdims=True)), 1e-12)
    k_a = k_a.astype(f32)
    dirs = []
    for d in range(2):
        w_log = -jax.nn.softplus(-(w0[d].astype(f32) + wl @ w_decay_up[d].astype(f32))) - 0.5
        decay = jnp.exp(-jnp.exp(w_log))
        a = jax.nn.sigmoid(a0[d].astype(f32) + al @ w_iclr_up[d].astype(f32))
        k_d = k * (1 + (a - 1) * k_a)
        dirs.append((_split_heads(decay), _split_heads(k_d), _split_heads(a)))
    return v, kk, dirs


def _wkv_scan(state0, decay, k, v, kk, a, r, reverse):
    tm = lambda t: jnp.moveaxis(t, 1, 0)
    xs = (tm(decay), tm(k), tm(v), tm(-kk), tm(kk * a))
    if r is not None:
        xs = xs + (tm(r),)

    def step(s, inp):
        w_t, k_t, v_t, a_t, b_t = inp[:5]
        sa = jnp.einsum("bhvk,bhk->bhv", s, a_t)
        s = s * w_t[:, :, None, :] + sa[..., None] * b_t[:, :, None, :] + v_t[..., None] * k_t[:, :, None, :]
        y = jnp.einsum("bhvk,bhk->bhv", s, inp[5]) if len(inp) == 6 else None
        return s, y

    s_final, ys = lax.scan(step, state0, xs, reverse=reverse)
    return s_final, (None if r is None else jnp.moveaxis(ys, 0, 1))


def _rwkv_readout(za, r, ys, ks, v, r_k, lnx_w, lnx_b, w_gate_up):
    f32 = jnp.float32
    b, t = za.shape[:2]
    y = ys[0] + ys[1]
    mu = jnp.mean(y, axis=-1, keepdims=True)
    var = jnp.mean(jnp.square(y - mu), axis=-1, keepdims=True)
    y = ((y - mu) * lax.rsqrt(var + LNX_EPS)).reshape(b, t, D_RWKV) * lnx_w.astype(f32) + lnx_b.astype(f32)
    rk = r_k.astype(f32)
    bonus = jnp.sum(r * ks[0] * rk, axis=-1, keepdims=True) * v + jnp.sum(r * ks[1] * rk, axis=-1, keepdims=True) * v
    y = y + bonus.reshape(b, t, D_RWKV)
    g = jax.nn.sigmoid(za[..., OFF_GL:RWKV_COLS].astype(f32)) @ w_gate_up.astype(f32)
    return y * g


def _swiglu(h, w_gate, w_up, w_down):
    return (jax.nn.silu(h @ w_gate) * (h @ w_up)) @ w_down


def setup_inputs(seed: int = 0) -> dict:
    key = jax.random.key(seed)
    ks = jax.random.split(key, 32)
    f = jnp.float32
    L = DEPTH
    nrm = lambda k, shape, s: jax.random.normal(k, shape, f) * s
    return {
        "x": nrm(ks[0], (BATCH, SEQ, D_MODEL), 1.0),
        "c": nrm(ks[1], (BATCH, D_MODEL), 1.0),
        "ctx": nrm(ks[2], (BATCH, CTX_LEN, D_MODEL), 1.0),
        "c_ctx": nrm(ks[3], (D_MODEL,), 1.0),
        "w_ada": nrm(ks[4], (L, D_MODEL, 6 * D_MODEL), 0.5 * D_MODEL ** -0.5),
        "b_ada": nrm(ks[5], (L, 6 * D_MODEL), 0.02),
        "norm1": 1.0 + nrm(ks[6], (L, D_MODEL), 0.02),
        "w_in": nrm(ks[7], (L, D_MODEL, IN_COLS), D_MODEL ** -0.5),
        "mu_shift": jax.random.uniform(ks[8], (L, RWKV_COLS), f),
        "k_k": 0.85 + nrm(ks[9], (L, D_RWKV), 0.02),
        "k_a": 1.0 + nrm(ks[10], (L, D_RWKV), 0.02),
        "r_k": nrm(ks[11], (L, N_HEADS_RWKV, HEAD_SIZE), 0.1),
        "w0": nrm(ks[12], (L, 2, D_RWKV), 0.5),
        "w_decay_up": nrm(ks[13], (L, 2, DECAY_LORA, D_RWKV), 0.5 * DECAY_LORA ** -0.5),
        "a0": nrm(ks[14], (L, 2, D_RWKV), 0.1),
        "w_iclr_up": nrm(ks[15], (L, 2, ICLR_LORA, D_RWKV), 0.5 * ICLR_LORA ** -0.5),
        "w_gate_up": nrm(ks[16], (L, GATE_LORA, D_RWKV), GATE_LORA ** -0.5),
        "lnx_w": 1.0 + nrm(ks[17], (L, D_RWKV), 0.02),
        "lnx_b": nrm(ks[18], (L, D_RWKV), 0.02),
        "conv_w": nrm(ks[19], (L, 3, D_CONV), 3 ** -0.5),
        "w_out": nrm(ks[20], (L, D_MODEL, D_MODEL), D_MODEL ** -0.5),
        "norm2": 1.0 + nrm(ks[21], (L, D_MODEL), 0.02),
        "w_ffn_gate": nrm(ks[22], (L, D_MODEL, D_FF), D_MODEL ** -0.5),
        "w_ffn_up": nrm(ks[23], (L, D_MODEL, D_FF), D_MODEL ** -0.5),
        "w_ffn_down": nrm(ks[24], (L, D_FF, D_MODEL), D_FF ** -0.5),
        "norm_f": 1.0 + nrm(ks[25], (D_MODEL,), 0.02),
    }


def reference(x, c, ctx, c_ctx, w_ada, b_ada, norm1, w_in, mu_shift, k_k, k_a, r_k, w0, w_decay_up, a0,
              w_iclr_up, w_gate_up, lnx_w, lnx_b, conv_w, w_out, norm2, w_ffn_gate, w_ffn_up, w_ffn_down, norm_f):
    dt = x.dtype
    b = x.shape[0]
    rows = x.shape[1] // GRID_W
    ctx_len = ctx.shape[1]
    silu_c = jax.nn.silu(c)
    silu_cc = jax.nn.silu(c_ctx)[None, :]
    for i in range(DEPTH):
        last = i == DEPTH - 1
        mod = (silu_c @ w_ada[i] + b_ada[i])[:, None, :]
        mod_c = (silu_cc @ w_ada[i] + b_ada[i])[:, None, :]
        sh1, sc1, g1, sh2, sc2, g2 = jnp.split(mod, 6, axis=-1)
        csh1, csc1, cg1, csh2, csc2, cg2 = jnp.split(mod_c, 6, axis=-1)

        h = _modulate(_rmsnorm(x, norm1[i]), sh1, sc1)
        hc = _modulate(_rmsnorm(ctx, norm1[i]), csh1, csc1)
        z = h @ w_in[i]
        zc = hc @ (w_in[i][:, :STATE_COLS] if last else w_in[i])
        za = _token_shift(z[..., :RWKV_COLS], mu_shift[i], rows, GRID_W)
        n_ca = min(zc.shape[-1], RWKV_COLS)
        zca = _token_shift(zc[..., :n_ca], mu_shift[i][:n_ca], 1, ctx_len)

        v_l, kk_l, dirs_l = _state_inputs(za, k_k[i], k_a[i], w0[i], w_decay_up[i], a0[i], w_iclr_up[i])
        v_c, kk_c, dirs_c = _state_inputs(zca, k_k[i], k_k[i] * 0 + k_a[i], w0[i], w_decay_up[i], a0[i], w_iclr_up[i]) if False else _state_inputs(zca, k_k[i], k_a[i], w0[i], w_decay_up[i], a0[i], w_iclr_up[i])
        r_l = _split_heads(za[..., OFF_R:OFF_GL].astype(jnp.float32))
        r_c = None if last else _split_heads(zca[..., OFF_R:OFF_GL].astype(jnp.float32))
        state0 = jnp.zeros((b, N_HEADS_RWKV, HEAD_SIZE, HEAD_SIZE), jnp.float32)
        ys_l, ys_c = [], []
        for d in range(2):
            rev = d == 1
            dec_c, k_cd, a_cd = dirs_c[d]
            s_ctx, y_c = _wkv_scan(state0, dec_c, k_cd, v_c, kk_c, a_cd, r_c, rev)
            dec_l, k_ld, a_ld = dirs_l[d]
            _, y_l = _wkv_scan(s_ctx, dec_l, k_ld, v_l, kk_l, a_ld, r_l, rev)
            ys_l.append(y_l)
            ys_c.append(y_c)
        y_rwkv = _rwkv_readout(za, r_l, ys_l, [dd[1] for dd in dirs_l], v_l, r_k[i], lnx_w[i], lnx_b[i], w_gate_up[i])

        b_g, c_g, x_g = jnp.split(z[..., RWKV_COLS:], 3, axis=-1)
        y_conv = b_g * _short_conv(c_g * x_g, conv_w[i], rows, GRID_W)

        x = x + g1 * (jnp.concatenate([y_rwkv.astype(dt), y_conv], axis=-1) @ w_out[i])
        x = x + g2 * _swiglu(_modulate(_rmsnorm(x, norm2[i]), sh2, sc2), w_ffn_gate[i], w_ffn_up[i], w_ffn_down[i])

        if not last:
            yc_rwkv = _rwkv_readout(zca, r_c, ys_c, [dd[1] for dd in dirs_c], v_c, r_k[i], lnx_w[i], lnx_b[i], w_gate_up[i])
            bc_g, cc_g, xc_g = jnp.split(zc[..., RWKV_COLS:], 3, axis=-1)
            yc_conv = bc_g * _short_conv(cc_g * xc_g, conv_w[i], 1, ctx_len)
            ctx = ctx + cg1 * (jnp.concatenate([yc_rwkv.astype(dt), yc_conv], axis=-1) @ w_out[i])
            ctx = ctx + cg2 * _swiglu(_modulate(_rmsnorm(ctx, norm2[i]), csh2, csc2), w_ffn_gate[i], w_ffn_up[i], w_ffn_down[i])
    return _rmsnorm(x, norm_f)
```

```python
import functools
import math

import jax
import jax.numpy as jnp
from jax import lax
from jax.experimental import pallas as pl
from jax.experimental.pallas import tpu as pltpu

F32 = jnp.float32
BF16 = jnp.bfloat16

LANES = 128
VMEM_LIMIT_BYTES = 56 * 1024 * 1024

HEAD = 64
PAIR = 2 * HEAD
CHUNK = 64
GRID_W = 64
DECAY_LORA = 96
ICLR_LORA = 96
GATE_LORA = 256
LORA_PAD = 256
NORM_EPS = 1e-6
LNX_EPS = 64e-5
EXP_M05 = math.exp(-0.5)


def _cparams(*sem):
    return pltpu.CompilerParams(dimension_semantics=sem, vmem_limit_bytes=VMEM_LIMIT_BYTES)


def _dot(a, b):
    return jnp.dot(a, b, preferred_element_type=F32)


def _dot_nt(a, b):
    return lax.dot_general(a, b, (((1,), (1,)), ((), ())), preferred_element_type=F32)


def _dot_tn(a, b):
    return lax.dot_general(a, b, (((0,), (0,)), ((), ())), preferred_element_type=F32)


def _split2(x):
    hi = x.astype(BF16)
    lo = (x - hi.astype(F32)).astype(BF16)
    return hi, lo


def _sigmoid(x):
    return 1.0 / (1.0 + jnp.exp(-x))


def _scan_body(k_ref, v_ref, r_ref, la_ref, kk_ref, ka_ref, w0_ref, a0_ref, wd_hi_ref, wd_lo_ref,
               wa_hi_ref, wa_lo_ref, y_ref, h_ref, *, rev, n_pairs):
    c = pl.program_id(0)

    @pl.when(c == 0)
    def _():
        h_ref[...] = jnp.zeros_like(h_ref)

    k = k_ref[...]
    v = v_ref[...]
    r = r_ref[...]
    la = la_ref[...]
    la_t = jnp.tanh(la)
    lt_hi, lt_lo = _split2(la_t)
    la_hi, la_lo = _split2(la)
    wd_hi = wd_hi_ref[...]
    wa_hi = wa_hi_ref[...]
    dec_pre = w0_ref[...] + _dot(lt_hi, wd_hi) + _dot(lt_lo, wd_hi) + _dot(lt_hi, wd_lo_ref[...])
    icl_pre = a0_ref[...] + _dot(la_hi, wa_hi) + _dot(la_lo, wa_hi) + _dot(la_hi, wa_lo_ref[...])
    lw = -EXP_M05 * _sigmoid(dec_pre)
    a = _sigmoid(icl_pre)
    kd = k * (1.0 + (a - 1.0) * ka_ref[...])
    kkr = k * kk_ref[...]

    row = lax.broadcasted_iota(jnp.int32, (CHUNK, CHUNK), 0)
    col = lax.broadcasted_iota(jnp.int32, (CHUNK, CHUNK), 1)
    tri = ((col >= row) if rev else (col <= row)).astype(BF16)
    lw_hi, lw_lo = _split2(lw)
    L = _dot(tri, lw_hi) + _dot(tri, lw_lo)
    ltot = L[0:1, :] if rev else L[CHUNK - 1:CHUNK, :]
    e_l = jnp.exp(L)
    e_nl = jnp.exp(-L)
    e_lx = jnp.exp(L - lw)
    e_tl = jnp.exp(ltot - L)

    t_i = lax.broadcasted_iota(jnp.int32, (CHUNK, PAIR), 0)
    lane = lax.broadcasted_iota(jnp.int32, (CHUNK, PAIR), 1)
    s_i = lane & (HEAD - 1)
    head0 = lane < HEAD
    strict = (s_i > t_i) if rev else (s_i < t_i)
    incl = (s_i >= t_i) if rev else (s_i <= t_i)
    eye = (s_i == t_i).astype(F32)
    rr = lax.broadcasted_iota(jnp.int32, (PAIR, PAIR), 0)
    cc = lax.broadcasted_iota(jnp.int32, (PAIR, PAIR), 1)
    same_head = (rr < HEAD) == (cc < HEAD)
    ones_bd = same_head.astype(BF16)
    ones_cp = jnp.ones((CHUNK, PAIR), BF16)

    def bd(x):
        z = jnp.zeros_like(x)
        return jnp.concatenate([jnp.where(head0, x, z), jnp.where(head0, z, x)], axis=0)

    for p in range(n_pairs):
        sl = slice(p * PAIR, (p + 1) * PAIR)
        kkr_p = kkr[:, sl]
        sq_hi, sq_lo = _split2(kkr_p * kkr_p)
        ss = _dot(sq_hi, ones_bd) + _dot(sq_lo, ones_bd)
        kk_p = kkr_p / jnp.maximum(jnp.sqrt(ss), 1e-12)
        b_p = kk_p * a[:, sl]
        v_p = v[:, sl]
        ah = (-kk_p * e_lx[:, sl]).astype(BF16)
        rh = (r[:, sl] * e_l[:, sl]).astype(BF16)
        bc = (b_p * e_nl[:, sl]).astype(BF16)
        kc = (kd[:, sl] * e_nl[:, sl]).astype(BF16)
        kt = (kd[:, sl] * e_tl[:, sl]).astype(BF16)
        bt = (b_p * e_tl[:, sl]).astype(BF16)
        v_b = v_p.astype(BF16)

        sc = _dot_nt(jnp.concatenate([ah, rh], axis=0),
                     jnp.concatenate([bd(bc), bd(kc)], axis=0))
        zero = jnp.zeros((CHUNK, PAIR), F32)
        m_ab = jnp.where(strict, sc[:CHUNK, :PAIR], zero)
        m_ak = jnp.where(strict, sc[:CHUNK, PAIR:], zero).astype(BF16)
        n_rb = jnp.where(incl, sc[CHUNK:, :PAIR], zero).astype(BF16)
        n_rk = jnp.where(incl, sc[CHUNK:, PAIR:], zero).astype(BF16)

        s_acc = eye + m_ab
        m_pow = m_ab.astype(BF16)
        m_pow = _dot(m_pow, bd(m_pow))
        n_steps = int(math.log2(CHUNK)) - 1
        for j in range(n_steps):
            mb = m_pow.astype(BF16)
            if j < n_steps - 1:
                both = _dot(mb, jnp.concatenate([bd(mb), bd(s_acc.astype(BF16))], axis=1))
                m_pow = both[:, :PAIR]
                s_acc = s_acc + both[:, PAIR:]
            else:
                s_acc = s_acc + _dot(mb, bd(s_acc.astype(BF16)))
        t_inv = s_acc.astype(BF16)

        h0 = h_ref[p]
        h0_b = h0.astype(BF16)
        v_bd = bd(v_b)
        w = _dot(jnp.concatenate([ah, m_ak], axis=1), jnp.concatenate([h0_b, v_bd], axis=0))
        u = _dot(t_inv, bd(w.astype(BF16)))
        u_b = u.astype(BF16)
        y = _dot(jnp.concatenate([rh, n_rk, n_rb], axis=1),
                 jnp.concatenate([h0_b, v_bd, bd(u_b)], axis=0))
        y_ref[:, sl] = y

        lw_p_hi, lw_p_lo = lw_hi[:, sl], lw_lo[:, sl]
        ltot_col = _dot_tn(lw_p_hi, ones_cp) + _dot_tn(lw_p_lo, ones_cp)
        upd = _dot_tn(jnp.concatenate([kt, bt], axis=0), jnp.concatenate([v_b, u_b], axis=0))
        h_ref[p] = jnp.exp(ltot_col) * h0 + jnp.where(same_head, upd, jnp.zeros_like(upd))


def _scan_call(za, kk, ka, w0, a0, wd, wa, *, rev, n_lat_chunks, n_ctx_chunks, d_rwkv):
    n_chunks = n_lat_chunks + n_ctx_chunks
    n_pairs = d_rwkv // PAIR
    lora_blk = 3 * d_rwkv // LORA_PAD

    def chunk_of(c):
        if rev:
            return jnp.where(c < n_ctx_chunks, n_chunks - 1 - c, n_chunks - 1 - c)
        return jnp.where(c < n_ctx_chunks, n_lat_chunks + c, c - n_ctx_chunks)

    wd_hi, wd_lo = _split2(wd)
    wa_hi, wa_lo = _split2(wa)
    vec = pl.BlockSpec((1, d_rwkv), lambda c: (0, 0))
    mat = pl.BlockSpec((LORA_PAD, d_rwkv), lambda c: (0, 0))
    return pl.pallas_call(
        functools.partial(_scan_body, rev=rev, n_pairs=n_pairs),
        grid=(n_chunks,),
        in_specs=[
            pl.BlockSpec((CHUNK, d_rwkv), lambda c: (chunk_of(c), 0)),
            pl.BlockSpec((CHUNK, d_rwkv), lambda c: (chunk_of(c), 1)),
            pl.BlockSpec((CHUNK, d_rwkv), lambda c: (chunk_of(c), 2)),
            pl.BlockSpec((CHUNK, LORA_PAD), lambda c: (chunk_of(c), lora_blk)),
            vec, vec, vec, vec, mat, mat, mat, mat,
        ],
        out_specs=pl.BlockSpec((CHUNK, d_rwkv), lambda c: (chunk_of(c), 0)),
        out_shape=jax.ShapeDtypeStruct((n_chunks * CHUNK, d_rwkv), F32),
        scratch_shapes=[pltpu.VMEM((n_pairs, PAIR, PAIR), F32)],
        compiler_params=_cparams("arbitrary"),
        name="wkv_scan_rev" if rev else "wkv_scan_fwd",
    )(za, za, za, za, kk, ka, w0, a0, wd_hi, wd_lo, wa_hi, wa_lo)


MOD_ROWS = 8


def _mod_body(cc_ref, w_ref, b_ref, o_ref):
    cc = cc_ref[...]
    s = cc * _sigmoid(cc)
    s_hi, s_lo = _split2(s)
    w = w_ref[...].astype(BF16)
    o_ref[...] = _dot(s_hi, w) + _dot(s_lo, w) + b_ref[...]


def _mod_call(cc, w_ada, b_ada, *, tn):
    d, n = w_ada.shape
    return pl.pallas_call(
        _mod_body,
        grid=(n // tn,),
        in_specs=[pl.BlockSpec((MOD_ROWS, d), lambda j: (0, 0)),
                  pl.BlockSpec((d, tn), lambda j: (0, j)),
                  pl.BlockSpec((1, tn), lambda j: (0, j))],
        out_specs=pl.BlockSpec((MOD_ROWS, tn), lambda j: (0, j)),
        out_shape=jax.ShapeDtypeStruct((MOD_ROWS, n), F32),
        compiler_params=_cparams("arbitrary"),
        name="adaln_mod",
    )(cc, w_ada, b_ada)


def _norm_mod(xf, gain, shift, scale):
    y = xf * lax.rsqrt(jnp.mean(xf * xf, axis=-1, keepdims=True) + NORM_EPS) * gain
    return y * (1.0 + scale) + shift


def _norm_body(x_ref, g_ref, sh_ref, sc_ref, o_ref):
    o_ref[...] = _norm_mod(x_ref[...], g_ref[...], sh_ref[0:1, :], sc_ref[0:1, :]).astype(o_ref.dtype)


def _norm_ctx_body(x_ref, ctx_ref, g_ref, sh_ref, sc_ref, o_ref, *, n_lat_blocks):
    i = pl.program_id(0)

    @pl.when(i < n_lat_blocks)
    def _():
        o_ref[...] = _norm_mod(x_ref[...], g_ref[...], sh_ref[0:1, :], sc_ref[0:1, :]).astype(o_ref.dtype)

    @pl.when(i >= n_lat_blocks)
    def _():
        o_ref[...] = _norm_mod(ctx_ref[...], g_ref[...], sh_ref[1:2, :], sc_ref[1:2, :]).astype(o_ref.dtype)


def _norm_call(x, ctx, gain, mod, shift_blk, scale_blk, *, tm):
    t, d = x.shape
    n_lat = t // tm
    vec = pl.BlockSpec((1, d), lambda i: (0, 0))
    sh = pl.BlockSpec((MOD_ROWS, d), lambda i: (0, shift_blk))
    sc = pl.BlockSpec((MOD_ROWS, d), lambda i: (0, scale_blk))
    if ctx is None:
        return pl.pallas_call(
            _norm_body, grid=(n_lat,),
            in_specs=[pl.BlockSpec((tm, d), lambda i: (i, 0)), vec, sh, sc],
            out_specs=pl.BlockSpec((tm, d), lambda i: (i, 0)),
            out_shape=jax.ShapeDtypeStruct((t, d), BF16),
            compiler_params=_cparams("arbitrary"), name="norm_mod",
        )(x, gain, mod, mod)
    tc = ctx.shape[0]
    n_ctx = tc // tm
    return pl.pallas_call(
        functools.partial(_norm_ctx_body, n_lat_blocks=n_lat), grid=(n_lat + n_ctx,),
        in_specs=[pl.BlockSpec((tm, d), lambda i: (jnp.minimum(i, n_lat - 1), 0)),
                  pl.BlockSpec((tm, d), lambda i: (jnp.maximum(i - n_lat, 0), 0)), vec, sh, sc],
        out_specs=pl.BlockSpec((tm, d), lambda i: (i, 0)),
        out_shape=jax.ShapeDtypeStruct((t + tc, d), BF16),
        compiler_params=_cparams("arbitrary"), name="norm_mod_ctx",
    )(x, ctx, gain, mod, mod)


def _neighbours(z, row0, t_lat, ctx_len):
    tm = z.shape[0]
    g = row0 + lax.broadcasted_iota(jnp.int32, (tm, 1), 0)
    is_ctx = g >= t_lat
    pos = jnp.where(is_ctx, g - t_lat, g & (GRID_W - 1))
    last = jnp.where(is_ctx, ctx_len - 1, GRID_W - 1)
    prev = jnp.where(pos == 0, 0.0, pltpu.roll(z, 1, 0))
    nxt = jnp.where(pos == last, 0.0, pltpu.roll(z, tm - 1, 0))
    return prev, nxt


def _inproj_shift_body(h_ref, w_ref, mu_ref, o_ref, *, t_lat, ctx_len):
    z = _dot(h_ref[...], w_ref[...])
    prev, nxt = _neighbours(z, pl.program_id(0) * z.shape[0], t_lat, ctx_len)
    o_ref[...] = z + mu_ref[...] * (0.5 * (prev + nxt) - z)


def _inproj_shift_call(h_all, w, mu, *, t_lat, ctx_len, tm, tn):
    t, d = h_all.shape
    n = w.shape[1]
    return pl.pallas_call(
        functools.partial(_inproj_shift_body, t_lat=t_lat, ctx_len=ctx_len),
        grid=(t // tm, n // tn),
        in_specs=[pl.BlockSpec((tm, d), lambda i, j: (i, 0)),
                  pl.BlockSpec((d, tn), lambda i, j: (0, j)),
                  pl.BlockSpec((1, tn), lambda i, j: (0, j))],
        out_specs=pl.BlockSpec((tm, tn), lambda i, j: (i, j)),
        out_shape=jax.ShapeDtypeStruct((t, n), F32),
        compiler_params=_cparams("arbitrary", "arbitrary"), name="inproj_shift",
    )(h_all, w, mu)


def _inproj_conv_body(h_ref, w_ref, cw_ref, o_ref, *, t_lat):
    z = _dot(h_ref[...], w_ref[...])
    tc = o_ref.shape[1]
    u = z[:, tc:2 * tc] * z[:, 2 * tc:]
    prev, nxt = _neighbours(u, pl.program_id(0) * z.shape[0], t_lat, 1)
    cw = cw_ref[...]
    o_ref[...] = (z[:, :tc] * (cw[0:1, :] * prev + cw[1:2, :] * u + cw[2:3, :] * nxt)).astype(o_ref.dtype)


def _inproj_conv_call(h_all, w, conv_w, *, t_lat, tm, tc):
    d = h_all.shape[1]
    d_conv = conv_w.shape[1]
    cw = jnp.zeros((MOD_ROWS, d_conv), F32).at[:3].set(conv_w)
    return pl.pallas_call(
        functools.partial(_inproj_conv_body, t_lat=t_lat),
        grid=(t_lat // tm, d_conv // tc),
        in_specs=[pl.BlockSpec((tm, d), lambda i, j: (i, 0)),
                  pl.BlockSpec((d, 3 * tc), lambda i, j: (0, j)),
                  pl.BlockSpec((MOD_ROWS, tc), lambda i, j: (0, j))],
        out_specs=pl.BlockSpec((tm, tc), lambda i, j: (i, j)),
        out_shape=jax.ShapeDtypeStruct((t_lat, d_conv), BF16),
        compiler_params=_cparams("arbitrary", "arbitrary"), name="inproj_conv",
    )(h_all, w, cw)


SEG_W = 256


def _readout_body(y0_ref, y1_ref, k_ref, v_ref, r_ref, la_ref, gl_ref, ka_ref, rk_ref, a0_ref,
                  wa0_hi_ref, wa0_lo_ref, wa1_hi_ref, wa1_lo_ref, wg_ref, lw_ref, lb_ref, o_ref):
    la = la_ref[...]
    la_hi, la_lo = _split2(la)
    a0 = a0_ref[...]
    pre0 = a0[0:1, :] + _dot(la_hi, wa0_hi_ref[...]) + _dot(la_lo, wa0_hi_ref[...]) + _dot(la_hi, wa0_lo_ref[...])
    pre1 = a0[1:2, :] + _dot(la_hi, wa1_hi_ref[...]) + _dot(la_lo, wa1_hi_ref[...]) + _dot(la_hi, wa1_lo_ref[...])
    a_sum = _sigmoid(pre0) + _sigmoid(pre1)
    coef = r_ref[...] * rk_ref[...] * k_ref[...] * (2.0 + (a_sum - 2.0) * ka_ref[...])
    y = y0_ref[...] + y1_ref[...]
    g = _dot(_sigmoid(gl_ref[...]).astype(BF16), wg_ref[...])

    rr = lax.broadcasted_iota(jnp.int32, (SEG_W, SEG_W), 0) // HEAD
    cc = lax.broadcasted_iota(jnp.int32, (SEG_W, SEG_W), 1) // HEAD
    ones_seg = (rr == cc).astype(BF16)

    def segsum(t):
        hi, lo = _split2(t)
        return _dot(hi, ones_seg) + _dot(lo, ones_seg)

    for s in range(y.shape[1] // SEG_W):
        sl = slice(s * SEG_W, (s + 1) * SEG_W)
        y_s = y[:, sl]
        mu = segsum(y_s) * (1.0 / HEAD)
        yc = y_s - mu
        var = segsum(yc * yc) * (1.0 / HEAD)
        yn = yc * lax.rsqrt(var + LNX_EPS) * lw_ref[:, sl] + lb_ref[:, sl]
        bonus = segsum(coef[:, sl]) * v_ref[:, sl]
        o_ref[:, sl] = ((yn + bonus) * g[:, sl]).astype(o_ref.dtype)


def _readout_call(y0, y1, za, k_a, r_k, a0, wa0, wa1, w_gate, lnx_w, lnx_b, *, t_lat, d_rwkv, tm):
    lora_blk = 3 * d_rwkv // LORA_PAD
    row = lambda jb: pl.BlockSpec((tm, d_rwkv), lambda i: (i, jb))
    vec = pl.BlockSpec((1, d_rwkv), lambda i: (0, 0))
    mat = pl.BlockSpec((LORA_PAD, d_rwkv), lambda i: (0, 0))
    wa0_hi, wa0_lo = _split2(wa0)
    wa1_hi, wa1_lo = _split2(wa1)
    return pl.pallas_call(
        _readout_body, grid=(t_lat // tm,),
        in_specs=[row(0), row(0), row(0), row(1), row(2),
                  pl.BlockSpec((tm, LORA_PAD), lambda i: (i, lora_blk)),
                  pl.BlockSpec((tm, GATE_LORA), lambda i: (i, lora_blk + 1)),
                  vec, vec, pl.BlockSpec((2, d_rwkv), lambda i: (0, 0)),
                  mat, mat, mat, mat, pl.BlockSpec((GATE_LORA, d_rwkv), lambda i: (0, 0)), vec, vec],
        out_specs=pl.BlockSpec((tm, d_rwkv), lambda i: (i, 0)),
        out_shape=jax.ShapeDtypeStruct((t_lat, d_rwkv), BF16),
        compiler_params=_cparams("arbitrary"), name="rwkv_readout",
    )(y0, y1, za, za, za, za, za, k_a, r_k, a0, wa0_hi, wa0_lo, wa1_hi, wa1_lo, w_gate, lnx_w, lnx_b)


def _outproj_body(ya_ref, yb_ref, wa_ref, wb_ref, x_ref, g_ref, o_ref):
    acc = _dot(ya_ref[...], wa_ref[...]) + _dot(yb_ref[...], wb_ref[...])
    o_ref[...] = x_ref[...] + g_ref[0:1, :] * acc


def _outproj_call(ya, yb, w, x, mod, gate_blk0, *, tm, tn):
    t, da = ya.shape
    db = yb.shape[1]
    n = w.shape[1]
    assert da == db
    return pl.pallas_call(
        _outproj_body, grid=(t // tm, n // tn),
        in_specs=[pl.BlockSpec((tm, da), lambda i, j: (i, 0)),
                  pl.BlockSpec((tm, db), lambda i, j: (i, 0)),
                  pl.BlockSpec((da, tn), lambda i, j: (0, j)),
                  pl.BlockSpec((db, tn), lambda i, j: (1, j)),
                  pl.BlockSpec((tm, tn), lambda i, j: (i, j)),
                  pl.BlockSpec((MOD_ROWS, tn), lambda i, j: (0, gate_blk0 + j))],
        out_specs=pl.BlockSpec((tm, tn), lambda i, j: (i, j)),
        out_shape=jax.ShapeDtypeStruct((t, n), F32),
        compiler_params=_cparams("arbitrary", "arbitrary"), name="outproj_residual",
    )(ya, yb, w, w, x, mod)


def _ffn_up_body(h_ref, wg_ref, wu_ref, o_ref):
    h = h_ref[...]
    g = _dot(h, wg_ref[...])
    u = _dot(h, wu_ref[...])
    o_ref[...] = (g * _sigmoid(g) * u).astype(o_ref.dtype)


def _ffn_up_call(h, wg, wu, *, tm, tn):
    t, d = h.shape
    n = wg.shape[1]
    return pl.pallas_call(
        _ffn_up_body, grid=(t // tm, n // tn),
        in_specs=[pl.BlockSpec((tm, d), lambda i, j: (i, 0)),
                  pl.BlockSpec((d, tn), lambda i, j: (0, j)),
                  pl.BlockSpec((d, tn), lambda i, j: (0, j))],
        out_specs=pl.BlockSpec((tm, tn), lambda i, j: (i, j)),
        out_shape=jax.ShapeDtypeStruct((t, n), BF16),
        compiler_params=_cparams("arbitrary", "arbitrary"), name="ffn_gate_up",
    )(h, wg, wu)


def _ffn_down_body(a_ref, w_ref, x_ref, g_ref, nf_ref, o_ref):
    kk = pl.program_id(1)
    part = _dot(a_ref[...], w_ref[...])

    @pl.when(kk == 0)
    def _():
        o_ref[...] = part

    @pl.when(kk > 0)
    def _():
        o_ref[...] += part

    @pl.when(kk == pl.num_programs(1) - 1)
    def _():
        x2 = x_ref[...] + g_ref[0:1, :] * o_ref[...]
        o_ref[...] = x2 * lax.rsqrt(jnp.mean(x2 * x2, axis=-1, keepdims=True) + NORM_EPS) * nf_ref[...]


def _ffn_down_call(act, w, x, mod, gate_blk, norm_f, *, tm, tk):
    t, kdim = act.shape
    d = w.shape[1]
    return pl.pallas_call(
        _ffn_down_body, grid=(t // tm, kdim // tk),
        in_specs=[pl.BlockSpec((tm, tk), lambda i, k: (i, k)),
                  pl.BlockSpec((tk, d), lambda i, k: (k, 0)),
                  pl.BlockSpec((tm, d), lambda i, k: (i, 0), pipeline_mode=pl.Buffered(1)),
                  pl.BlockSpec((MOD_ROWS, d), lambda i, k: (0, gate_blk)),
                  pl.BlockSpec((1, d), lambda i, k: (0, 0))],
        out_specs=pl.BlockSpec((tm, d), lambda i, k: (i, 0)),
        out_shape=jax.ShapeDtypeStruct((t, d), F32),
        compiler_params=_cparams("arbitrary", "arbitrary"), name="ffn_down_residual_norm",
    )(act, w, x, mod, norm_f)


FF_ALIGN = 1024


def _tiles():
    return dict(mod_tn=1024, norm_tm=256, shift_tm=1408, shift_tn=512, conv_tm=1024, conv_tc=256,
                readout_tm=256, out_tm=1024, out_tn=512, up_tm=1024, up_tn=512, down_tm=512, down_tk=512)


def kernel(x, c, ctx, c_ctx, w_ada, b_ada, norm1, w_in, mu_shift, k_k, k_a, r_k, w0, w_decay_up, a0, w_iclr_up, w_gate_up, lnx_w, lnx_b, conv_w, w_out, norm2, w_ffn_gate, w_ffn_up, w_ffn_down, norm_f):
    assert x.shape[0] == 1 and w_ada.shape[0] == 1, "single batch element, single layer"
    tl = _tiles()
    t_lat, d = x.shape[1], x.shape[2]
    ctx_len = ctx.shape[1]
    d_rwkv = k_k.shape[1]
    d_conv = conv_w.shape[2]
    d_ff = w_ffn_gate.shape[2]
    assert t_lat % GRID_W == 0 and ctx_len % CHUNK == 0 and d_rwkv % SEG_W == 0
    x2d, ctx2d = x[0], ctx[0]

    off_v, off_wl = d_rwkv, 2 * d_rwkv
    off_al = off_wl + DECAY_LORA
    off_r = off_al + ICLR_LORA
    off_gl = off_r + d_rwkv
    rwkv_cols = off_gl + GATE_LORA
    wi = w_in[0]
    lora_pad = LORA_PAD - DECAY_LORA - ICLR_LORA
    w_shift = jnp.concatenate(
        [wi[:, :off_wl], wi[:, off_r:off_gl], wi[:, off_wl:off_r], jnp.zeros((d, lora_pad), wi.dtype),
         wi[:, off_gl:rwkv_cols]], axis=1).astype(BF16)
    ms = mu_shift[0]
    mu = jnp.concatenate([ms[:off_wl], ms[off_r:off_gl], ms[off_wl:off_r], jnp.zeros((lora_pad,), ms.dtype),
                          ms[off_gl:]])[None, :]
    tc = tl["conv_tc"]
    w_conv = wi[:, rwkv_cols:].reshape(d, 3, d_conv // tc, tc).transpose(0, 2, 1, 3).reshape(d, 3 * d_conv)
    w_conv = w_conv.astype(BF16)
    wd = [jnp.zeros((LORA_PAD, d_rwkv), F32).at[:DECAY_LORA].set(w_decay_up[0, i]) for i in range(2)]
    wa = [jnp.zeros((LORA_PAD, d_rwkv), F32).at[DECAY_LORA:DECAY_LORA + ICLR_LORA].set(w_iclr_up[0, i])
          for i in range(2)]
    ff_pad = -d_ff % FF_ALIGN
    wg = jnp.pad(w_ffn_gate[0], ((0, 0), (0, ff_pad))).astype(BF16)
    wu = jnp.pad(w_ffn_up[0], ((0, 0), (0, ff_pad))).astype(BF16)
    wdn = jnp.pad(w_ffn_down[0], ((0, ff_pad), (0, 0))).astype(BF16)
    wo = w_out[0].astype(BF16)

    cc = jnp.zeros((MOD_ROWS, d), F32).at[0].set(c[0]).at[1].set(c_ctx)
    mod = _mod_call(cc, w_ada[0], b_ada, tn=tl["mod_tn"])

    h_all = _norm_call(x2d, ctx2d, norm1, mod, 0, 1, tm=tl["norm_tm"])
    za = _inproj_shift_call(h_all, w_shift, mu, t_lat=t_lat, ctx_len=ctx_len, tm=tl["shift_tm"], tn=tl["shift_tn"])
    y_conv = _inproj_conv_call(h_all, w_conv, conv_w[0], t_lat=t_lat, tm=tl["conv_tm"], tc=tc)
    n_lat_chunks, n_ctx_chunks = t_lat // CHUNK, ctx_len // CHUNK
    ys = [_scan_call(za, k_k, k_a, w0[0, i:i + 1], a0[0, i:i + 1], wd[i], wa[i], rev=(i == 1),
                     n_lat_chunks=n_lat_chunks, n_ctx_chunks=n_ctx_chunks, d_rwkv=d_rwkv) for i in range(2)]
    y_rwkv = _readout_call(ys[0], ys[1], za, k_a, r_k.reshape(1, d_rwkv), a0[0], wa[0], wa[1],
                           w_gate_up[0].astype(BF16), lnx_w, lnx_b, t_lat=t_lat, d_rwkv=d_rwkv,
                           tm=tl["readout_tm"])
    assert d_rwkv == d_conv
    x1 = _outproj_call(y_rwkv, y_conv, wo, x2d, mod, 2 * d // tl["out_tn"], tm=tl["out_tm"], tn=tl["out_tn"])

    h2 = _norm_call(x1, None, norm2, mod, 3, 4, tm=tl["norm_tm"])
    act = _ffn_up_call(h2, wg, wu, tm=tl["up_tm"], tn=tl["up_tn"])
    out = _ffn_down_call(act, wdn, x1, mod, 5, norm_f[None, :], tm=tl["down_tm"], tk=tl["down_tk"])
    return out[None]
```

```python
import functools
import math

import jax
import jax.numpy as jnp
from jax import lax
from jax.experimental import pallas as pl
from jax.experimental.pallas import tpu as pltpu

F32 = jnp.float32
BF16 = jnp.bfloat16

LANES = 128
VMEM_LIMIT_BYTES = 56 * 1024 * 1024

HEAD = 64
PAIR = 2 * HEAD
CHUNK = 64
GRID_W = 64
DECAY_LORA = 96
ICLR_LORA = 96
GATE_LORA = 256
LORA_PAD = 256
NORM_EPS = 1e-6
LNX_EPS = 64e-5
EXP_M05 = math.exp(-0.5)


def _cparams(*sem):
    return pltpu.CompilerParams(dimension_semantics=sem, vmem_limit_bytes=VMEM_LIMIT_BYTES)


def _dot(a, b):
    return jnp.dot(a, b, preferred_element_type=F32)


def _bmm(a, b):
    return lax.dot_general(a, b, (((2,), (1,)), ((0,), (0,))), preferred_element_type=F32)


def _bmm_nt(a, b):
    return lax.dot_general(a, b, (((2,), (2,)), ((0,), (0,))), preferred_element_type=F32)


def _bmm_tn(a, b):
    return lax.dot_general(a, b, (((1,), (1,)), ((0,), (0,))), preferred_element_type=F32)


def _split2(x):
    hi = x.astype(BF16)
    lo = (x - hi.astype(F32)).astype(BF16)
    return hi, lo


def _sigmoid(x):
    return 1.0 / (1.0 + jnp.exp(-x))


def _scan_body(k_ref, v_ref, r_ref, la_ref, kk_ref, ka_ref, w0_ref, a0_ref, wd_hi_ref, wd_lo_ref,
               wa_hi_ref, wa_lo_ref, y_ref, h_ref, *, rev, n_pairs):
    c = pl.program_id(0)

    @pl.when(c == 0)
    def _():
        h_ref[...] = jnp.zeros_like(h_ref)

    k = k_ref[...]
    v = v_ref[...]
    r = r_ref[...]
    la = la_ref[...]
    la_t = jnp.tanh(la)
    lt_hi, lt_lo = _split2(la_t)
    la_hi, la_lo = _split2(la)
    wd_hi = wd_hi_ref[...]
    wa_hi = wa_hi_ref[...]
    dec_pre = w0_ref[...] + _dot(lt_hi, wd_hi) + _dot(lt_lo, wd_hi) + _dot(lt_hi, wd_lo_ref[...])
    icl_pre = a0_ref[...] + _dot(la_hi, wa_hi) + _dot(la_lo, wa_hi) + _dot(la_hi, wa_lo_ref[...])
    lw = -EXP_M05 * _sigmoid(dec_pre)
    a = _sigmoid(icl_pre)
    kd = k * (1.0 + (a - 1.0) * ka_ref[...])
    kkr = k * kk_ref[...]

    row = lax.broadcasted_iota(jnp.int32, (CHUNK, CHUNK), 0)
    col = lax.broadcasted_iota(jnp.int32, (CHUNK, CHUNK), 1)
    tri = ((col >= row) if rev else (col <= row)).astype(BF16)
    lw_hi, lw_lo = _split2(lw)
    L = _dot(tri, lw_hi) + _dot(tri, lw_lo)

    def st(x):
        return jnp.stack([x[:, p * PAIR:(p + 1) * PAIR] for p in range(n_pairs)], axis=0)

    lw_s, l_s, a_s, kd_s, kkr_s, v_s, r_s = (st(t) for t in (lw, L, a, kd, kkr, v, r))
    ltot = l_s[:, 0:1, :] if rev else l_s[:, CHUNK - 1:CHUNK, :]
    e_l = jnp.exp(l_s)
    e_nl = jnp.exp(-l_s)
    e_lx = jnp.exp(l_s - lw_s)
    e_tl = jnp.exp(ltot - l_s)
    g_tot = jnp.exp(ltot)

    t_i = lax.broadcasted_iota(jnp.int32, (1, CHUNK, PAIR), 1)
    lane = lax.broadcasted_iota(jnp.int32, (1, CHUNK, PAIR), 2)
    s_i = lane & (HEAD - 1)
    head0 = lane < HEAD
    strict = (s_i > t_i) if rev else (s_i < t_i)
    incl = (s_i >= t_i) if rev else (s_i <= t_i)
    eye = (s_i == t_i).astype(F32)
    rr = lax.broadcasted_iota(jnp.int32, (PAIR, PAIR), 0)
    cc = lax.broadcasted_iota(jnp.int32, (PAIR, PAIR), 1)
    same_head = (rr < HEAD) == (cc < HEAD)
    ones_bd = same_head.astype(BF16)

    def bd(x):
        z = jnp.zeros_like(x)
        return jnp.concatenate([jnp.where(head0, x, z), jnp.where(head0, z, x)], axis=1)

    sq_hi, sq_lo = _split2((kkr_s * kkr_s).reshape(n_pairs * CHUNK, PAIR))
    ss = (_dot(sq_hi, ones_bd) + _dot(sq_lo, ones_bd)).reshape(n_pairs, CHUNK, PAIR)
    kk_s = kkr_s / jnp.maximum(jnp.sqrt(ss), 1e-12)
    b_s = kk_s * a_s
    ah = (-kk_s * e_lx).astype(BF16)
    rh = (r_s * e_l).astype(BF16)
    bc = (b_s * e_nl).astype(BF16)
    kc = (kd_s * e_nl).astype(BF16)
    kt = (kd_s * e_tl).astype(BF16)
    bt = (b_s * e_tl).astype(BF16)
    v_b = v_s.astype(BF16)

    sc = _bmm_nt(jnp.concatenate([ah, rh], axis=1),
                 jnp.concatenate([bd(bc), bd(kc)], axis=1))
    m_ab = jnp.where(strict, sc[:, :CHUNK, :PAIR], 0.0)
    m_ak = jnp.where(strict, sc[:, :CHUNK, PAIR:], 0.0).astype(BF16)
    n_rb = jnp.where(incl, sc[:, CHUNK:, :PAIR], 0.0).astype(BF16)
    n_rk = jnp.where(incl, sc[:, CHUNK:, PAIR:], 0.0).astype(BF16)

    s_acc = eye + m_ab
    m_pow = m_ab.astype(BF16)
    m_pow = _bmm(m_pow, bd(m_pow))
    n_steps = int(math.log2(CHUNK)) - 1
    for j in range(n_steps):
        mb = m_pow.astype(BF16)
        if j < n_steps - 1:
            both = _bmm(mb, jnp.concatenate([bd(mb), bd(s_acc.astype(BF16))], axis=2))
            m_pow = both[:, :, :PAIR]
            s_acc = s_acc + both[:, :, PAIR:]
        else:
            s_acc = s_acc + _bmm(mb, bd(s_acc.astype(BF16)))
    t_inv = s_acc.astype(BF16)

    ht0 = h_ref[...]
    ht0_b = ht0.astype(BF16)
    v_bd = bd(v_b)
    w = _bmm_nt(ah, ht0_b) + _bmm(m_ak, v_bd)
    u_b = _bmm(t_inv, bd(w.astype(BF16))).astype(BF16)
    y = _bmm_nt(rh, ht0_b) + _bmm(jnp.concatenate([n_rk, n_rb], axis=2),
                                  jnp.concatenate([v_bd, bd(u_b)], axis=1))
    y_ref[...] = jnp.concatenate([y[p] for p in range(n_pairs)], axis=1)
    upd = _bmm_tn(jnp.concatenate([v_b, u_b], axis=1), jnp.concatenate([kt, bt], axis=1))
    h_ref[...] = ht0 * g_tot + jnp.where(same_head[None], upd, 0.0)


def _scan_call(za, kk, ka, w0, a0, wd, wa, *, rev, n_lat_chunks, n_ctx_chunks, d_rwkv):
    n_chunks = n_lat_chunks + n_ctx_chunks
    n_pairs = d_rwkv // PAIR
    lora_blk = 3 * d_rwkv // LORA_PAD

    def chunk_of(c):
        if rev:
            return jnp.where(c < n_ctx_chunks, n_chunks - 1 - c, n_chunks - 1 - c)
        return jnp.where(c < n_ctx_chunks, n_lat_chunks + c, c - n_ctx_chunks)

    wd_hi, wd_lo = _split2(wd)
    wa_hi, wa_lo = _split2(wa)
    vec = pl.BlockSpec((1, d_rwkv), lambda c: (0, 0))
    mat = pl.BlockSpec((LORA_PAD, d_rwkv), lambda c: (0, 0))
    return pl.pallas_call(
        functools.partial(_scan_body, rev=rev, n_pairs=n_pairs),
        grid=(n_chunks,),
        in_specs=[
            pl.BlockSpec((CHUNK, d_rwkv), lambda c: (chunk_of(c), 0)),
            pl.BlockSpec((CHUNK, d_rwkv), lambda c: (chunk_of(c), 1)),
            pl.BlockSpec((CHUNK, d_rwkv), lambda c: (chunk_of(c), 2)),
            pl.BlockSpec((CHUNK, LORA_PAD), lambda c: (chunk_of(c), lora_blk)),
            vec, vec, vec, vec, mat, mat, mat, mat,
        ],
        out_specs=pl.BlockSpec((CHUNK, d_rwkv), lambda c: (chunk_of(c), 0)),
        out_shape=jax.ShapeDtypeStruct((n_chunks * CHUNK, d_rwkv), F32),
        scratch_shapes=[pltpu.VMEM((n_pairs, PAIR, PAIR), F32)],
        compiler_params=_cparams("arbitrary"),
        name="wkv_scan_rev" if rev else "wkv_scan_fwd",
    )(za, za, za, za, kk, ka, w0, a0, wd_hi, wd_lo, wa_hi, wa_lo)


MOD_ROWS = 8


def _mod_body(cc_ref, w_ref, b_ref, o_ref):
    cc = cc_ref[...]
    s = cc * _sigmoid(cc)
    s_hi, s_lo = _split2(s)
    w = w_ref[...].astype(BF16)
    o_ref[...] = _dot(s_hi, w) + _dot(s_lo, w) + b_ref[...]


def _mod_call(cc, w_ada, b_ada, *, tn):
    d, n = w_ada.shape
    return pl.pallas_call(
        _mod_body,
        grid=(n // tn,),
        in_specs=[pl.BlockSpec((MOD_ROWS, d), lambda j: (0, 0)),
                  pl.BlockSpec((d, tn), lambda j: (0, j)),
                  pl.BlockSpec((1, tn), lambda j: (0, j))],
        out_specs=pl.BlockSpec((MOD_ROWS, tn), lambda j: (0, j)),
        out_shape=jax.ShapeDtypeStruct((MOD_ROWS, n), F32),
        compiler_params=_cparams("arbitrary"),
        name="adaln_mod",
    )(cc, w_ada, b_ada)


def _norm_mod(xf, gain, shift, scale):
    y = xf * lax.rsqrt(jnp.mean(xf * xf, axis=-1, keepdims=True) + NORM_EPS) * gain
    return y * (1.0 + scale) + shift


def _norm_body(x_ref, g_ref, sh_ref, sc_ref, o_ref):
    o_ref[...] = _norm_mod(x_ref[...], g_ref[...], sh_ref[0:1, :], sc_ref[0:1, :]).astype(o_ref.dtype)


def _norm_ctx_body(x_ref, ctx_ref, g_ref, sh_ref, sc_ref, o_ref, *, n_lat_blocks):
    i = pl.program_id(0)

    @pl.when(i < n_lat_blocks)
    def _():
        o_ref[...] = _norm_mod(x_ref[...], g_ref[...], sh_ref[0:1, :], sc_ref[0:1, :]).astype(o_ref.dtype)

    @pl.when(i >= n_lat_blocks)
    def _():
        o_ref[...] = _norm_mod(ctx_ref[...], g_ref[...], sh_ref[1:2, :], sc_ref[1:2, :]).astype(o_ref.dtype)


def _norm_call(x, ctx, gain, mod, shift_blk, scale_blk, *, tm):
    t, d = x.shape
    n_lat = t // tm
    vec = pl.BlockSpec((1, d), lambda i: (0, 0))
    sh = pl.BlockSpec((MOD_ROWS, d), lambda i: (0, shift_blk))
    sc = pl.BlockSpec((MOD_ROWS, d), lambda i: (0, scale_blk))
    if ctx is None:
        return pl.pallas_call(
            _norm_body, grid=(n_lat,),
            in_specs=[pl.BlockSpec((tm, d), lambda i: (i, 0)), vec, sh, sc],
            out_specs=pl.BlockSpec((tm, d), lambda i: (i, 0)),
            out_shape=jax.ShapeDtypeStruct((t, d), BF16),
            compiler_params=_cparams("arbitrary"), name="norm_mod",
        )(x, gain, mod, mod)
    tc = ctx.shape[0]
    n_ctx = tc // tm
    return pl.pallas_call(
        functools.partial(_norm_ctx_body, n_lat_blocks=n_lat), grid=(n_lat + n_ctx,),
        in_specs=[pl.BlockSpec((tm, d), lambda i: (jnp.minimum(i, n_lat - 1), 0)),
                  pl.BlockSpec((tm, d), lambda i: (jnp.maximum(i - n_lat, 0), 0)), vec, sh, sc],
        out_specs=pl.BlockSpec((tm, d), lambda i: (i, 0)),
        out_shape=jax.ShapeDtypeStruct((t + tc, d), BF16),
        compiler_params=_cparams("arbitrary"), name="norm_mod_ctx",
    )(x, ctx, gain, mod, mod)


def _neighbours(z, row0, t_lat, ctx_len):
    tm = z.shape[0]
    g = row0 + lax.broadcasted_iota(jnp.int32, (tm, 1), 0)
    is_ctx = g >= t_lat
    pos = jnp.where(is_ctx, g - t_lat, g & (GRID_W - 1))
    last = jnp.where(is_ctx, ctx_len - 1, GRID_W - 1)
    prev = jnp.where(pos == 0, 0.0, pltpu.roll(z, 1, 0))
    nxt = jnp.where(pos == last, 0.0, pltpu.roll(z, tm - 1, 0))
    return prev, nxt


def _inproj_shift_body(h_ref, *refs, t_lat, ctx_len, seg_starts):
    w_refs, mu_ref, o_ref = refs[:-2], refs[-2], refs[-1]
    j = pl.program_id(1)
    bounds = list(seg_starts[1:]) + [pl.num_programs(1)]
    for w_ref, lo, hi in zip(w_refs, seg_starts, bounds):
        @pl.when((j >= lo) & (j < hi))
        def _(w_ref=w_ref):
            z = _dot(h_ref[...], w_ref[...])
            prev, nxt = _neighbours(z, pl.program_id(0) * z.shape[0], t_lat, ctx_len)
            o_ref[...] = z + mu_ref[...] * (0.5 * (prev + nxt) - z)


def _inproj_shift_call(h_all, ws, mu, *, t_lat, ctx_len, tm, tn):
    t, d = h_all.shape
    seg_tiles = [w.shape[1] // tn for w in ws]
    seg_starts = [sum(seg_tiles[:s]) for s in range(len(ws))]
    n_tiles = sum(seg_tiles)

    def w_spec(lo, cnt):
        return pl.BlockSpec((d, tn), lambda i, j: (0, jnp.clip(j - lo, 0, cnt - 1)))

    return pl.pallas_call(
        functools.partial(_inproj_shift_body, t_lat=t_lat, ctx_len=ctx_len, seg_starts=tuple(seg_starts)),
        grid=(t // tm, n_tiles),
        in_specs=[pl.BlockSpec((tm, d), lambda i, j: (i, 0))]
                 + [w_spec(lo, cnt) for lo, cnt in zip(seg_starts, seg_tiles)]
                 + [pl.BlockSpec((1, tn), lambda i, j: (0, j))],
        out_specs=pl.BlockSpec((tm, tn), lambda i, j: (i, j)),
        out_shape=jax.ShapeDtypeStruct((t, n_tiles * tn), F32),
        compiler_params=_cparams("arbitrary", "arbitrary"), name="inproj_shift",
    )(h_all, *ws, mu)


def _inproj_conv_body(h_ref, wb_ref, wc_ref, wx_ref, cw_ref, o_ref, *, t_lat):
    h = h_ref[...]
    u = _dot(h, wc_ref[...]) * _dot(h, wx_ref[...])
    prev, nxt = _neighbours(u, pl.program_id(0) * u.shape[0], t_lat, 1)
    cw = cw_ref[...]
    conv = cw[0:1, :] * prev + cw[1:2, :] * u + cw[2:3, :] * nxt
    o_ref[...] = (_dot(h, wb_ref[...]) * conv).astype(o_ref.dtype)


def _inproj_conv_call(h_all, w, conv_w, *, t_lat, tm, tc):
    d = h_all.shape[1]
    d_conv = conv_w.shape[1]
    n_tiles = d_conv // tc
    cw = jnp.zeros((MOD_ROWS, d_conv), F32).at[:3].set(conv_w)
    return pl.pallas_call(
        functools.partial(_inproj_conv_body, t_lat=t_lat),
        grid=(t_lat // tm, n_tiles),
        in_specs=[pl.BlockSpec((tm, d), lambda i, j: (i, 0)),
                  pl.BlockSpec((d, tc), lambda i, j: (0, j)),
                  pl.BlockSpec((d, tc), lambda i, j: (0, n_tiles + j)),
                  pl.BlockSpec((d, tc), lambda i, j: (0, 2 * n_tiles + j)),
                  pl.BlockSpec((MOD_ROWS, tc), lambda i, j: (0, j))],
        out_specs=pl.BlockSpec((tm, tc), lambda i, j: (i, j)),
        out_shape=jax.ShapeDtypeStruct((t_lat, d_conv), BF16),
        compiler_params=_cparams("arbitrary", "arbitrary"), name="inproj_conv",
    )(h_all, w, w, w, cw)


SEG_W = 256


def _readout_body(y0_ref, y1_ref, k_ref, v_ref, r_ref, la_ref, gl_ref, ka_ref, rk_ref, a0_ref,
                  wa0_hi_ref, wa0_lo_ref, wa1_hi_ref, wa1_lo_ref, wg_ref, lw_ref, lb_ref, o_ref):
    la = la_ref[...]
    la_hi, la_lo = _split2(la)
    a0 = a0_ref[...]
    pre0 = a0[0:1, :] + _dot(la_hi, wa0_hi_ref[...]) + _dot(la_lo, wa0_hi_ref[...]) + _dot(la_hi, wa0_lo_ref[...])
    pre1 = a0[1:2, :] + _dot(la_hi, wa1_hi_ref[...]) + _dot(la_lo, wa1_hi_ref[...]) + _dot(la_hi, wa1_lo_ref[...])
    a_sum = _sigmoid(pre0) + _sigmoid(pre1)
    coef = r_ref[...] * rk_ref[...] * k_ref[...] * (2.0 + (a_sum - 2.0) * ka_ref[...])
    y = y0_ref[...] + y1_ref[...]
    g = _dot(_sigmoid(gl_ref[...]).astype(BF16), wg_ref[...])

    rr = lax.broadcasted_iota(jnp.int32, (SEG_W, SEG_W), 0) // HEAD
    cc = lax.broadcasted_iota(jnp.int32, (SEG_W, SEG_W), 1) // HEAD
    ones_seg = (rr == cc).astype(BF16)

    def segsum(t):
        hi, lo = _split2(t)
        return _dot(hi, ones_seg) + _dot(lo, ones_seg)

    for s in range(y.shape[1] // SEG_W):
        sl = slice(s * SEG_W, (s + 1) * SEG_W)
        y_s = y[:, sl]
        mu = segsum(y_s) * (1.0 / HEAD)
        yc = y_s - mu
        var = segsum(yc * yc) * (1.0 / HEAD)
        yn = yc * lax.rsqrt(var + LNX_EPS) * lw_ref[:, sl] + lb_ref[:, sl]
        bonus = segsum(coef[:, sl]) * v_ref[:, sl]
        o_ref[:, sl] = ((yn + bonus) * g[:, sl]).astype(o_ref.dtype)


def _readout_call(y0, y1, za, k_a, r_k, a0, wa0, wa1, w_gate, lnx_w, lnx_b, *, t_lat, d_rwkv, tm):
    lora_blk = 3 * d_rwkv // LORA_PAD
    row = lambda jb: pl.BlockSpec((tm, d_rwkv), lambda i: (i, jb))
    vec = pl.BlockSpec((1, d_rwkv), lambda i: (0, 0))
    mat = pl.BlockSpec((LORA_PAD, d_rwkv), lambda i: (0, 0))
    wa0_hi, wa0_lo = _split2(wa0)
    wa1_hi, wa1_lo = _split2(wa1)
    return pl.pallas_call(
        _readout_body, grid=(t_lat // tm,),
        in_specs=[row(0), row(0), row(0), row(1), row(2),
                  pl.BlockSpec((tm, LORA_PAD), lambda i: (i, lora_blk)),
                  pl.BlockSpec((tm, GATE_LORA), lambda i: (i, lora_blk + 1)),
                  vec, vec, pl.BlockSpec((2, d_rwkv), lambda i: (0, 0)),
                  mat, mat, mat, mat, pl.BlockSpec((GATE_LORA, d_rwkv), lambda i: (0, 0)), vec, vec],
        out_specs=pl.BlockSpec((tm, d_rwkv), lambda i: (i, 0)),
        out_shape=jax.ShapeDtypeStruct((t_lat, d_rwkv), BF16),
        compiler_params=_cparams("arbitrary"), name="rwkv_readout",
    )(y0, y1, za, za, za, za, za, k_a, r_k, a0, wa0_hi, wa0_lo, wa1_hi, wa1_lo, w_gate, lnx_w, lnx_b)


def _outproj_body(ya_ref, yb_ref, wa_ref, wb_ref, x_ref, g_ref, o_ref):
    acc = _dot(ya_ref[...], wa_ref[...]) + _dot(yb_ref[...], wb_ref[...])
    o_ref[...] = x_ref[...] + g_ref[0:1, :] * acc


def _outproj_call(ya, yb, w, x, mod, gate_blk0, *, tm, tn):
    t, da = ya.shape
    db = yb.shape[1]
    n = w.shape[1]
    assert da == db
    return pl.pallas_call(
        _outproj_body, grid=(t // tm, n // tn),
        in_specs=[pl.BlockSpec((tm, da), lambda i, j: (i, 0)),
                  pl.BlockSpec((tm, db), lambda i, j: (i, 0)),
                  pl.BlockSpec((da, tn), lambda i, j: (0, j)),
                  pl.BlockSpec((db, tn), lambda i, j: (1, j)),
                  pl.BlockSpec((tm, tn), lambda i, j: (i, j)),
                  pl.BlockSpec((MOD_ROWS, tn), lambda i, j: (0, gate_blk0 + j))],
        out_specs=pl.BlockSpec((tm, tn), lambda i, j: (i, j)),
        out_shape=jax.ShapeDtypeStruct((t, n), F32),
        compiler_params=_cparams("arbitrary", "arbitrary"), name="outproj_residual",
    )(ya, yb, w, w, x, mod)


def _ffn_up_body(h_ref, wg_ref, wu_ref, o_ref):
    h = h_ref[...]
    g = _dot(h, wg_ref[...].astype(BF16))
    u = _dot(h, wu_ref[...].astype(BF16))
    o_ref[...] = (g * _sigmoid(g) * u).astype(o_ref.dtype)


def _ffn_up_call(h, wg, wu, *, tm, tn):
    t, d = h.shape
    n = wg.shape[1]
    return pl.pallas_call(
        _ffn_up_body, grid=(t // tm, n // tn),
        in_specs=[pl.BlockSpec((tm, d), lambda i, j: (i, 0)),
                  pl.BlockSpec((d, tn), lambda i, j: (0, j)),
                  pl.BlockSpec((d, tn), lambda i, j: (0, j))],
        out_specs=pl.BlockSpec((tm, tn), lambda i, j: (i, j)),
        out_shape=jax.ShapeDtypeStruct((t, n), BF16),
        compiler_params=_cparams("arbitrary", "arbitrary"), name="ffn_gate_up",
    )(h, wg, wu)


def _ffn_down_body(a_ref, w_ref, x_ref, g_ref, o_ref):
    o_ref[...] = x_ref[...] + g_ref[0:1, :] * _dot(a_ref[...], w_ref[...])


def _ffn_down_call(act, w, x, mod, gate_blk0, *, tm, tn):
    t, kdim = act.shape
    d = w.shape[1]
    return pl.pallas_call(
        _ffn_down_body, grid=(t // tm, d // tn),
        in_specs=[pl.BlockSpec((tm, kdim), lambda i, j: (i, 0)),
                  pl.BlockSpec((kdim, tn), lambda i, j: (0, j)),
                  pl.BlockSpec((tm, tn), lambda i, j: (i, j)),
                  pl.BlockSpec((MOD_ROWS, tn), lambda i, j: (0, gate_blk0 + j))],
        out_specs=pl.BlockSpec((tm, tn), lambda i, j: (i, j)),
        out_shape=jax.ShapeDtypeStruct((t, d), F32),
        compiler_params=_cparams("arbitrary", "arbitrary"), name="ffn_down_residual",
    )(act, w, x, mod)


def _final_norm_body(x_ref, g_ref, o_ref):
    x = x_ref[...]
    o_ref[...] = x * lax.rsqrt(jnp.mean(x * x, axis=-1, keepdims=True) + NORM_EPS) * g_ref[...]


def _final_norm_call(x, gain, *, tm):
    t, d = x.shape
    return pl.pallas_call(
        _final_norm_body, grid=(t // tm,),
        in_specs=[pl.BlockSpec((tm, d), lambda i: (i, 0)), pl.BlockSpec((1, d), lambda i: (0, 0))],
        out_specs=pl.BlockSpec((tm, d), lambda i: (i, 0)),
        out_shape=jax.ShapeDtypeStruct((t, d), F32),
        compiler_params=_cparams("arbitrary"), name="final_norm",
    )(x, gain)


def _tiles():
    return dict(mod_tn=1024, norm_tm=256, shift_tm=1408, shift_tn=256, conv_tm=1024, conv_tc=256,
                readout_tm=256, out_tm=1024, out_tn=512, up_tm=1024, up_tn=256, down_tm=512, down_tn=256)


def kernel(x, c, ctx, c_ctx, w_ada, b_ada, norm1, w_in, mu_shift, k_k, k_a, r_k, w0, w_decay_up, a0, w_iclr_up, w_gate_up, lnx_w, lnx_b, conv_w, w_out, norm2, w_ffn_gate, w_ffn_up, w_ffn_down, norm_f):
    assert x.shape[0] == 1 and w_ada.shape[0] == 1, "single batch element, single layer"
    tl = _tiles()
    t_lat, d = x.shape[1], x.shape[2]
    ctx_len = ctx.shape[1]
    d_rwkv = k_k.shape[1]
    d_conv = conv_w.shape[2]
    d_ff = w_ffn_gate.shape[2]
    assert t_lat % GRID_W == 0 and ctx_len % CHUNK == 0 and d_rwkv % SEG_W == 0
    x2d, ctx2d = x[0], ctx[0]

    off_v, off_wl = d_rwkv, 2 * d_rwkv
    off_al = off_wl + DECAY_LORA
    off_r = off_al + ICLR_LORA
    off_gl = off_r + d_rwkv
    rwkv_cols = off_gl + GATE_LORA
    wi = w_in[0]
    lora_pad = LORA_PAD - DECAY_LORA - ICLR_LORA
    w_shift = [wi[:, :off_wl].astype(BF16), wi[:, off_r:off_gl].astype(BF16),
               jnp.concatenate([wi[:, off_wl:off_r].astype(BF16), jnp.zeros((d, lora_pad), BF16),
                                wi[:, off_gl:rwkv_cols].astype(BF16)], axis=1)]
    ms = mu_shift[0]
    mu = jnp.concatenate([ms[:off_wl], ms[off_r:off_gl], ms[off_wl:off_r], jnp.zeros((lora_pad,), ms.dtype),
                          ms[off_gl:]])[None, :]
    w_conv = wi[:, rwkv_cols:].astype(BF16)
    wd = [jnp.zeros((LORA_PAD, d_rwkv), F32).at[:DECAY_LORA].set(w_decay_up[0, i]) for i in range(2)]
    wa = [jnp.zeros((LORA_PAD, d_rwkv), F32).at[DECAY_LORA:DECAY_LORA + ICLR_LORA].set(w_iclr_up[0, i])
          for i in range(2)]
    wdn = w_ffn_down[0].astype(BF16)
    wo = w_out[0].astype(BF16)

    cc = jnp.zeros((MOD_ROWS, d), F32).at[0].set(c[0]).at[1].set(c_ctx)
    mod = _mod_call(cc, w_ada[0], b_ada, tn=tl["mod_tn"])

    h_all = _norm_call(x2d, ctx2d, norm1, mod, 0, 1, tm=tl["norm_tm"])
    za = _inproj_shift_call(h_all, w_shift, mu, t_lat=t_lat, ctx_len=ctx_len, tm=tl["shift_tm"], tn=tl["shift_tn"])
    y_conv = _inproj_conv_call(h_all, w_conv, conv_w[0], t_lat=t_lat, tm=tl["conv_tm"], tc=tl["conv_tc"])
    n_lat_chunks, n_ctx_chunks = t_lat // CHUNK, ctx_len // CHUNK
    ys = [_scan_call(za, k_k, k_a, w0[0, i:i + 1], a0[0, i:i + 1], wd[i], wa[i], rev=(i == 1),
                     n_lat_chunks=n_lat_chunks, n_ctx_chunks=n_ctx_chunks, d_rwkv=d_rwkv) for i in range(2)]
    y_rwkv = _readout_call(ys[0], ys[1], za, k_a, r_k.reshape(1, d_rwkv), a0[0], wa[0], wa[1],
                           w_gate_up[0].astype(BF16), lnx_w, lnx_b, t_lat=t_lat, d_rwkv=d_rwkv,
                           tm=tl["readout_tm"])
    assert d_rwkv == d_conv
    x1 = _outproj_call(y_rwkv, y_conv, wo, x2d, mod, 2 * d // tl["out_tn"], tm=tl["out_tm"], tn=tl["out_tn"])

    h2 = _norm_call(x1, None, norm2, mod, 3, 4, tm=tl["norm_tm"])
    act = _ffn_up_call(h2, w_ffn_gate[0], w_ffn_up[0], tm=tl["up_tm"], tn=tl["up_tn"])
    x2 = _ffn_down_call(act, wdn, x1, mod, 5 * d // tl["down_tn"], tm=tl["down_tm"], tn=tl["down_tn"])
    out = _final_norm_call(x2, norm_f[None, :], tm=tl["norm_tm"])
    return out[None]
```

```python
import functools
import math

import jax
import jax.numpy as jnp
from jax import lax
from jax.experimental import pallas as pl
from jax.experimental.pallas import tpu as pltpu

F32 = jnp.float32
BF16 = jnp.bfloat16

MXU_WIDTH = 256
SEG_W = MXU_WIDTH
VMEM_LIMIT_BYTES = 56 * 1024 * 1024

HEAD = 64
GROUP = 128
CHUNK = 64
GRID_W = 64
DECAY_LORA = 96
ICLR_LORA = 96
GATE_LORA = 256
LORA_TILE = 128
LORA_PAD = 2 * LORA_TILE
NORM_EPS = 1e-6
LNX_EPS = 64e-5
EXP_M05 = math.exp(-0.5)


def _cparams(*sem):
    return pltpu.CompilerParams(dimension_semantics=sem, vmem_limit_bytes=VMEM_LIMIT_BYTES)


def _dot(a, b):
    return jnp.dot(a, b, preferred_element_type=F32)


def _bmm(a, b):
    return lax.dot_general(a, b, (((2,), (1,)), ((0,), (0,))), preferred_element_type=F32)


def _bmm_nt(a, b):
    return lax.dot_general(a, b, (((2,), (2,)), ((0,), (0,))), preferred_element_type=F32)


def _bmm_tn(a, b):
    return lax.dot_general(a, b, (((1,), (1,)), ((0,), (0,))), preferred_element_type=F32)


def _split2(x):
    hi = x.astype(BF16)
    lo = (x - hi.astype(F32)).astype(BF16)
    return hi, lo


def _sigmoid(x):
    return 1.0 / (1.0 + jnp.exp(-x))


def _head_sums(x):
    rows, d = x.shape
    n_seg = d // SEG_W
    xs = jnp.concatenate([x[:, s * SEG_W:(s + 1) * SEG_W] for s in range(n_seg)], axis=0)
    rr = lax.broadcasted_iota(jnp.int32, (SEG_W, SEG_W), 0) // HEAD
    cc = lax.broadcasted_iota(jnp.int32, (SEG_W, SEG_W), 1) // HEAD
    ones_seg = (rr == cc).astype(BF16)
    hi, lo = _split2(xs)
    ss = _dot(hi, ones_seg) + _dot(lo, ones_seg)
    return jnp.concatenate([ss[s * rows:(s + 1) * rows] for s in range(n_seg)], axis=1)


def _scan_body(k_ref, v_ref, r_ref, la_ref, kk_ref, ka_ref, w0_ref, a0_ref, wd_ref, wa_ref,
               y_ref, h_ref, *, rev, n_groups):
    c = pl.program_id(0)

    @pl.when(c == 0)
    def _():
        h_ref[...] = jnp.zeros_like(h_ref)

    k = k_ref[...]
    v = v_ref[...]
    r = r_ref[...]
    la = la_ref[...]
    dec_pre = w0_ref[...] + _dot(jnp.tanh(la[:, :LORA_TILE]).astype(BF16), wd_ref[...])
    icl_pre = a0_ref[...] + _dot(la[:, LORA_TILE:].astype(BF16), wa_ref[...])
    lw = -EXP_M05 * _sigmoid(dec_pre)
    a = _sigmoid(icl_pre)
    kd = k * (1.0 + (a - 1.0) * ka_ref[...])
    kkr = k * kk_ref[...]

    row = lax.broadcasted_iota(jnp.int32, (CHUNK, CHUNK), 0)
    col = lax.broadcasted_iota(jnp.int32, (CHUNK, CHUNK), 1)
    tri = ((col >= row) if rev else (col <= row)).astype(BF16)
    lw_hi, lw_lo = _split2(lw)
    L = _dot(tri, lw_hi) + _dot(tri, lw_lo)

    def st(x):
        return jnp.stack([x[:, p * GROUP:(p + 1) * GROUP] for p in range(n_groups)], axis=0)

    lw_s, l_s, a_s, kd_s, v_s, r_s = (st(t) for t in (lw, L, a, kd, v, r))
    ltot = l_s[:, 0:1, :] if rev else l_s[:, CHUNK - 1:CHUNK, :]
    e_l = jnp.exp(l_s)
    e_nl = jnp.exp(-l_s)
    e_lx = jnp.exp(l_s - lw_s)
    e_tl = jnp.exp(ltot - l_s)
    g_tot = jnp.exp(ltot)

    t_i = lax.broadcasted_iota(jnp.int32, (1, CHUNK, GROUP), 1)
    lane = lax.broadcasted_iota(jnp.int32, (1, CHUNK, GROUP), 2)
    s_i = lane & (HEAD - 1)
    head_of_lane = lane // HEAD
    strict = (s_i > t_i) if rev else (s_i < t_i)
    incl = (s_i >= t_i) if rev else (s_i <= t_i)
    eye = (s_i == t_i).astype(F32)
    rr = lax.broadcasted_iota(jnp.int32, (GROUP, GROUP), 0)
    cc = lax.broadcasted_iota(jnp.int32, (GROUP, GROUP), 1)
    same_head = (rr // HEAD) == (cc // HEAD)

    def bd(x):
        z = jnp.zeros_like(x)
        return jnp.concatenate([jnp.where(head_of_lane == g, x, z) for g in range(GROUP // HEAD)], axis=1)

    kk_s = st(kkr / jnp.maximum(jnp.sqrt(_head_sums(kkr * kkr)), 1e-12))
    b_s = kk_s * a_s
    ah = (-kk_s * e_lx).astype(BF16)
    rh = (r_s * e_l).astype(BF16)
    bc = (b_s * e_nl).astype(BF16)
    kc = (kd_s * e_nl).astype(BF16)
    kt = (kd_s * e_tl).astype(BF16)
    bt = (b_s * e_tl).astype(BF16)
    v_b = v_s.astype(BF16)

    sc = _bmm_nt(jnp.concatenate([ah, rh], axis=1),
                 jnp.concatenate([bd(bc), bd(kc)], axis=1))
    m_ab = jnp.where(strict, sc[:, :CHUNK, :GROUP], 0.0)
    m_ak = jnp.where(strict, sc[:, :CHUNK, GROUP:], 0.0).astype(BF16)
    n_rb = jnp.where(incl, sc[:, CHUNK:, :GROUP], 0.0).astype(BF16)
    n_rk = jnp.where(incl, sc[:, CHUNK:, GROUP:], 0.0).astype(BF16)

    s_acc = eye + m_ab
    m_pow = m_ab.astype(BF16)
    m_pow = _bmm(m_pow, bd(m_pow))
    n_steps = int(math.log2(CHUNK)) - 1
    for j in range(n_steps):
        mb = m_pow.astype(BF16)
        if j < n_steps - 1:
            both = _bmm(mb, jnp.concatenate([bd(mb), bd(s_acc.astype(BF16))], axis=2))
            m_pow = both[:, :, :GROUP]
            s_acc = s_acc + both[:, :, GROUP:]
        else:
            s_acc = s_acc + _bmm(mb, bd(s_acc.astype(BF16)))
    t_inv = s_acc.astype(BF16)

    ht0 = h_ref[...]
    ht0_b = ht0.astype(BF16)
    v_bd = bd(v_b)
    from_h0 = _bmm_nt(jnp.concatenate([ah, rh], axis=1), ht0_b)
    w = from_h0[:, :CHUNK] + _bmm(m_ak, v_bd)
    u_b = _bmm(t_inv, bd(w.astype(BF16))).astype(BF16)
    y = from_h0[:, CHUNK:] + _bmm(jnp.concatenate([n_rk, n_rb], axis=2),
                                  jnp.concatenate([v_bd, bd(u_b)], axis=1))
    y_ref[...] = jnp.concatenate([y[p] for p in range(n_groups)], axis=1)
    upd = _bmm_tn(jnp.concatenate([v_b, u_b], axis=1), jnp.concatenate([kt, bt], axis=1))
    h_ref[...] = ht0 * g_tot + jnp.where(same_head[None], upd, 0.0)


def _scan_call(za, kk, ka, w0, a0, wd, wa, *, rev, n_lat_chunks, n_ctx_chunks, d_rwkv):
    n_chunks = n_lat_chunks + n_ctx_chunks
    n_groups = d_rwkv // GROUP
    lora_blk = 3 * d_rwkv // LORA_PAD

    assert CHUNK == HEAD and d_rwkv % GROUP == 0

    def chunk_of(c):
        if rev:
            return n_chunks - 1 - c
        return jnp.where(c < n_ctx_chunks, n_lat_chunks + c, c - n_ctx_chunks)

    vec = pl.BlockSpec((1, d_rwkv), lambda c: (0, 0))
    mat = pl.BlockSpec((LORA_TILE, d_rwkv), lambda c: (0, 0))
    return pl.pallas_call(
        functools.partial(_scan_body, rev=rev, n_groups=n_groups),
        grid=(n_chunks,),
        in_specs=[
            pl.BlockSpec((CHUNK, d_rwkv), lambda c: (chunk_of(c), 0)),
            pl.BlockSpec((CHUNK, d_rwkv), lambda c: (chunk_of(c), 1)),
            pl.BlockSpec((CHUNK, d_rwkv), lambda c: (chunk_of(c), 2)),
            pl.BlockSpec((CHUNK, LORA_PAD), lambda c: (chunk_of(c), lora_blk)),
            vec, vec, vec, vec, mat, mat,
        ],
        out_specs=pl.BlockSpec((CHUNK, d_rwkv), lambda c: (chunk_of(c), 0)),
        out_shape=jax.ShapeDtypeStruct((n_chunks * CHUNK, d_rwkv), F32),
        scratch_shapes=[pltpu.VMEM((n_groups, GROUP, GROUP), F32)],
        compiler_params=_cparams("arbitrary"),
        name="wkv_scan_rev" if rev else "wkv_scan_fwd",
    )(za, za, za, za, kk, ka, w0, a0, wd, wa)


MOD_ROWS = 8


def _mod_body(cc_ref, w_ref, b_ref, o_ref):
    cc = cc_ref[...]
    s = cc * _sigmoid(cc)
    s_hi, s_lo = _split2(s)
    w = w_ref[...].astype(BF16)
    o_ref[...] = _dot(s_hi, w) + _dot(s_lo, w) + b_ref[...]


def _mod_call(cc, w_ada, b_ada, *, tn):
    d, n = w_ada.shape
    return pl.pallas_call(
        _mod_body,
        grid=(n // tn,),
        in_specs=[pl.BlockSpec((MOD_ROWS, d), lambda j: (0, 0)),
                  pl.BlockSpec((d, tn), lambda j: (0, j)),
                  pl.BlockSpec((1, tn), lambda j: (0, j))],
        out_specs=pl.BlockSpec((MOD_ROWS, tn), lambda j: (0, j)),
        out_shape=jax.ShapeDtypeStruct((MOD_ROWS, n), F32),
        compiler_params=_cparams("arbitrary"),
        name="adaln_mod",
    )(cc, w_ada, b_ada)


def _norm_mod(xf, gain, shift, scale):
    y = xf * lax.rsqrt(jnp.mean(xf * xf, axis=-1, keepdims=True) + NORM_EPS) * gain
    return y * (1.0 + scale) + shift


def _norm_ctx_body(x_ref, ctx_ref, g_ref, sh_ref, sc_ref, o_ref, *, n_lat_blocks):
    i = pl.program_id(0)

    @pl.when(i < n_lat_blocks)
    def _():
        o_ref[...] = _norm_mod(x_ref[...], g_ref[...], sh_ref[0:1, :], sc_ref[0:1, :]).astype(o_ref.dtype)

    @pl.when(i >= n_lat_blocks)
    def _():
        o_ref[...] = _norm_mod(ctx_ref[...], g_ref[...], sh_ref[1:2, :], sc_ref[1:2, :]).astype(o_ref.dtype)


def _norm_call(x, ctx, gain, mod, shift_blk, scale_blk, *, tm):
    t, d = x.shape
    n_lat = t // tm
    vec = pl.BlockSpec((1, d), lambda i: (0, 0))
    sh = pl.BlockSpec((MOD_ROWS, d), lambda i: (0, shift_blk))
    sc = pl.BlockSpec((MOD_ROWS, d), lambda i: (0, scale_blk))
    tc = ctx.shape[0]
    n_ctx = tc // tm
    return pl.pallas_call(
        functools.partial(_norm_ctx_body, n_lat_blocks=n_lat), grid=(n_lat + n_ctx,),
        in_specs=[pl.BlockSpec((tm, d), lambda i: (jnp.minimum(i, n_lat - 1), 0)),
                  pl.BlockSpec((tm, d), lambda i: (jnp.maximum(i - n_lat, 0), 0)), vec, sh, sc],
        out_specs=pl.BlockSpec((tm, d), lambda i: (i, 0)),
        out_shape=jax.ShapeDtypeStruct((t + tc, d), BF16),
        compiler_params=_cparams("arbitrary"), name="norm_mod_ctx",
    )(x, ctx, gain, mod, mod)


def _neighbours(z, row0, t_lat, ctx_len):
    tm = z.shape[0]
    g = row0 + lax.broadcasted_iota(jnp.int32, (tm, 1), 0)
    is_ctx = g >= t_lat
    pos = jnp.where(is_ctx, g - t_lat, g & (GRID_W - 1))
    last = jnp.where(is_ctx, ctx_len - 1, GRID_W - 1)
    prev = jnp.where(pos == 0, 0.0, pltpu.roll(z, 1, 0))
    nxt = jnp.where(pos == last, 0.0, pltpu.roll(z, tm - 1, 0))
    return prev, nxt


def _inproj_shift_body(h_ref, *refs, t_lat, ctx_len, seg_starts):
    w_refs, mu_ref, o_ref = refs[:-2], refs[-2], refs[-1]
    j = pl.program_id(1)
    bounds = list(seg_starts[1:]) + [pl.num_programs(1)]
    for w_ref, lo, hi in zip(w_refs, seg_starts, bounds):
        @pl.when((j >= lo) & (j < hi))
        def _(w_ref=w_ref):
            z = _dot(h_ref[...], w_ref[...])
            prev, nxt = _neighbours(z, pl.program_id(0) * z.shape[0], t_lat, ctx_len)
            o_ref[...] = z + mu_ref[...] * (0.5 * (prev + nxt) - z)


def _inproj_shift_call(h_all, ws, mu, *, t_lat, ctx_len, tm, tn):
    t, d = h_all.shape
    seg_tiles = [w.shape[1] // tn for w in ws]
    seg_starts = [sum(seg_tiles[:s]) for s in range(len(ws))]
    n_tiles = sum(seg_tiles)

    def w_spec(lo, cnt):
        return pl.BlockSpec((d, tn), lambda i, j: (0, jnp.clip(j - lo, 0, cnt - 1)))

    return pl.pallas_call(
        functools.partial(_inproj_shift_body, t_lat=t_lat, ctx_len=ctx_len, seg_starts=tuple(seg_starts)),
        grid=(t // tm, n_tiles),
        in_specs=[pl.BlockSpec((tm, d), lambda i, j: (i, 0))]
                 + [w_spec(lo, cnt) for lo, cnt in zip(seg_starts, seg_tiles)]
                 + [pl.BlockSpec((1, tn), lambda i, j: (0, j))],
        out_specs=pl.BlockSpec((tm, tn), lambda i, j: (i, j)),
        out_shape=jax.ShapeDtypeStruct((t, n_tiles * tn), F32),
        compiler_params=_cparams("arbitrary", "arbitrary"), name="inproj_shift",
    )(h_all, *ws, mu)


def _inproj_conv_body(h_ref, wb_ref, wc_ref, wx_ref, cw_ref, o_ref, *, t_lat):
    h = h_ref[...]
    u = _dot(h, wc_ref[...]) * _dot(h, wx_ref[...])
    prev, nxt = _neighbours(u, pl.program_id(0) * u.shape[0], t_lat, 1)
    cw = cw_ref[...]
    conv = cw[0:1, :] * prev + cw[1:2, :] * u + cw[2:3, :] * nxt
    o_ref[...] = (_dot(h, wb_ref[...]) * conv).astype(o_ref.dtype)


def _inproj_conv_call(h_all, w, conv_w, *, t_lat, tm, tc):
    d = h_all.shape[1]
    d_conv = conv_w.shape[1]
    n_tiles = d_conv // tc
    cw = jnp.zeros((MOD_ROWS, d_conv), F32).at[:3].set(conv_w)
    return pl.pallas_call(
        functools.partial(_inproj_conv_body, t_lat=t_lat),
        grid=(t_lat // tm, n_tiles),
        in_specs=[pl.BlockSpec((tm, d), lambda i, j: (i, 0)),
                  pl.BlockSpec((d, tc), lambda i, j: (0, j)),
                  pl.BlockSpec((d, tc), lambda i, j: (0, n_tiles + j)),
                  pl.BlockSpec((d, tc), lambda i, j: (0, 2 * n_tiles + j)),
                  pl.BlockSpec((MOD_ROWS, tc), lambda i, j: (0, j))],
        out_specs=pl.BlockSpec((tm, tc), lambda i, j: (i, j)),
        out_shape=jax.ShapeDtypeStruct((t_lat, d_conv), BF16),
        compiler_params=_cparams("arbitrary", "arbitrary"), name="inproj_conv",
    )(h_all, w, w, w, cw)


def _readout_body(y0_ref, y1_ref, k_ref, v_ref, r_ref, la_ref, gl_ref, ka_ref, rk_ref, a0_ref,
                  wa0_ref, wa1_ref, wg_ref, lw_ref, lb_ref, o_ref):
    la = la_ref[:, LORA_TILE:].astype(BF16)
    a0 = a0_ref[...]
    a_sum = _sigmoid(a0[0:1, :] + _dot(la, wa0_ref[...])) + _sigmoid(a0[1:2, :] + _dot(la, wa1_ref[...]))
    coef = r_ref[...] * rk_ref[...] * k_ref[...] * (2.0 + (a_sum - 2.0) * ka_ref[...])
    y = y0_ref[...] + y1_ref[...]
    g = _dot(_sigmoid(gl_ref[...]).astype(BF16), wg_ref[...])

    yc = y - _head_sums(y) * (1.0 / HEAD)
    var = _head_sums(yc * yc) * (1.0 / HEAD)
    yn = yc * lax.rsqrt(var + LNX_EPS) * lw_ref[...] + lb_ref[...]
    bonus = _head_sums(coef) * v_ref[...]
    o_ref[...] = ((yn + bonus) * g).astype(o_ref.dtype)


def _readout_call(y0, y1, za, k_a, r_k, a0, wa0, wa1, w_gate, lnx_w, lnx_b, *, t_lat, d_rwkv, tm):
    lora_blk = 3 * d_rwkv // LORA_PAD
    row = lambda jb: pl.BlockSpec((tm, d_rwkv), lambda i: (i, jb))
    vec = pl.BlockSpec((1, d_rwkv), lambda i: (0, 0))
    mat = pl.BlockSpec((LORA_TILE, d_rwkv), lambda i: (0, 0))
    return pl.pallas_call(
        _readout_body, grid=(t_lat // tm,),
        in_specs=[row(0), row(0), row(0), row(1), row(2),
                  pl.BlockSpec((tm, LORA_PAD), lambda i: (i, lora_blk)),
                  pl.BlockSpec((tm, GATE_LORA), lambda i: (i, lora_blk + 1)),
                  vec, vec, pl.BlockSpec((2, d_rwkv), lambda i: (0, 0)),
                  mat, mat, pl.BlockSpec((GATE_LORA, d_rwkv), lambda i: (0, 0)), vec, vec],
        out_specs=pl.BlockSpec((tm, d_rwkv), lambda i: (i, 0)),
        out_shape=jax.ShapeDtypeStruct((t_lat, d_rwkv), BF16),
        compiler_params=_cparams("arbitrary"), name="rwkv_readout",
    )(y0, y1, za, za, za, za, za, k_a, r_k, a0, wa0, wa1, w_gate, lnx_w, lnx_b)


def _outproj_body(ya_ref, yb_ref, wa_ref, wb_ref, x_ref, g_ref, n2_ref, sh_ref, sc_ref, o_ref, h_ref, rows_ref):
    j = pl.program_id(1)
    n_tiles, _, tn = rows_ref.shape
    acc = _dot(ya_ref[...], wa_ref[...]) + _dot(yb_ref[...], wb_ref[...])
    x1 = x_ref[...] + g_ref[0:1, :] * acc
    o_ref[...] = x1
    rows_ref[j] = x1

    @pl.when(j == n_tiles - 1)
    def _():
        ss = sum(jnp.sum(rows_ref[s] * rows_ref[s], axis=-1, keepdims=True) for s in range(n_tiles))
        rs = lax.rsqrt(ss * (1.0 / (n_tiles * tn)) + NORM_EPS)
        for s in range(n_tiles):
            sl = slice(s * tn, (s + 1) * tn)
            y = rows_ref[s] * rs * n2_ref[:, sl]
            h_ref[:, sl] = (y * (1.0 + sc_ref[0:1, sl]) + sh_ref[0:1, sl]).astype(h_ref.dtype)


def _outproj_call(ya, yb, w, x, mod, gate_blk0, gain2, shift_blk, scale_blk, *, tm, tn):
    t, da = ya.shape
    db = yb.shape[1]
    n = w.shape[1]
    assert da == db
    return pl.pallas_call(
        _outproj_body, grid=(t // tm, n // tn),
        in_specs=[pl.BlockSpec((tm, da), lambda i, j: (i, 0)),
                  pl.BlockSpec((tm, db), lambda i, j: (i, 0)),
                  pl.BlockSpec((da, tn), lambda i, j: (0, j)),
                  pl.BlockSpec((db, tn), lambda i, j: (1, j)),
                  pl.BlockSpec((tm, tn), lambda i, j: (i, j)),
                  pl.BlockSpec((MOD_ROWS, tn), lambda i, j: (0, gate_blk0 + j)),
                  pl.BlockSpec((1, n), lambda i, j: (0, 0)),
                  pl.BlockSpec((MOD_ROWS, n), lambda i, j: (0, shift_blk)),
                  pl.BlockSpec((MOD_ROWS, n), lambda i, j: (0, scale_blk))],
        out_specs=[pl.BlockSpec((tm, tn), lambda i, j: (i, j)),
                   pl.BlockSpec((tm, n), lambda i, j: (i, 0))],
        out_shape=[jax.ShapeDtypeStruct((t, n), F32), jax.ShapeDtypeStruct((t, n), BF16)],
        scratch_shapes=[pltpu.VMEM((n // tn, tm, tn), F32)],
        compiler_params=_cparams("arbitrary", "arbitrary"), name="outproj_residual_norm",
    )(ya, yb, w, w, x, mod, gain2, mod, mod)


def _ffn_up_body(h_ref, wg_ref, wu_ref, o_ref):
    h = h_ref[...]
    g = _dot(h, wg_ref[...].astype(BF16))
    u = _dot(h, wu_ref[...].astype(BF16))
    o_ref[...] = (g * _sigmoid(g) * u).astype(o_ref.dtype)


def _ffn_up_call(h, wg, wu, *, tm, tn):
    t, d = h.shape
    n = wg.shape[1]
    return pl.pallas_call(
        _ffn_up_body, grid=(t // tm, n // tn),
        in_specs=[pl.BlockSpec((tm, d), lambda i, j: (i, 0)),
                  pl.BlockSpec((d, tn), lambda i, j: (0, j)),
                  pl.BlockSpec((d, tn), lambda i, j: (0, j))],
        out_specs=pl.BlockSpec((tm, tn), lambda i, j: (i, j)),
        out_shape=jax.ShapeDtypeStruct((t, n), BF16),
        compiler_params=_cparams("arbitrary", "arbitrary"), name="ffn_gate_up",
    )(h, wg, wu)


def _ffn_down_body(a_ref, w_ref, x_ref, g_ref, o_ref):
    o_ref[...] = x_ref[...] + g_ref[0:1, :] * _dot(a_ref[...], w_ref[...])


def _ffn_down_call(act, w, x, mod, gate_blk0, *, tm, tn):
    t, kdim = act.shape
    d = w.shape[1]
    return pl.pallas_call(
        _ffn_down_body, grid=(t // tm, d // tn),
        in_specs=[pl.BlockSpec((tm, kdim), lambda i, j: (i, 0)),
                  pl.BlockSpec((kdim, tn), lambda i, j: (0, j)),
                  pl.BlockSpec((tm, tn), lambda i, j: (i, j)),
                  pl.BlockSpec((MOD_ROWS, tn), lambda i, j: (0, gate_blk0 + j))],
        out_specs=pl.BlockSpec((tm, tn), lambda i, j: (i, j)),
        out_shape=jax.ShapeDtypeStruct((t, d), F32),
        compiler_params=_cparams("arbitrary", "arbitrary"), name="ffn_down_residual",
    )(act, w, x, mod)


def _final_norm_body(x_ref, g_ref, o_ref):
    x = x_ref[...]
    o_ref[...] = x * lax.rsqrt(jnp.mean(x * x, axis=-1, keepdims=True) + NORM_EPS) * g_ref[...]


def _final_norm_call(x, gain, *, tm):
    t, d = x.shape
    return pl.pallas_call(
        _final_norm_body, grid=(t // tm,),
        in_specs=[pl.BlockSpec((tm, d), lambda i: (i, 0)), pl.BlockSpec((1, d), lambda i: (0, 0))],
        out_specs=pl.BlockSpec((tm, d), lambda i: (i, 0)),
        out_shape=jax.ShapeDtypeStruct((t, d), F32),
        compiler_params=_cparams("arbitrary"), name="final_norm",
    )(x, gain)


def _tiles():
    return dict(mod_tn=1024, norm_tm=256, shift_tm=1408, shift_tn=256, conv_tm=1024, conv_tc=256,
                readout_tm=256, out_tm=512, out_tn=512, up_tm=1024, up_tn=256, down_tm=512, down_tn=256)


def kernel(x, c, ctx, c_ctx, w_ada, b_ada, norm1, w_in, mu_shift, k_k, k_a, r_k, w0, w_decay_up, a0, w_iclr_up, w_gate_up, lnx_w, lnx_b, conv_w, w_out, norm2, w_ffn_gate, w_ffn_up, w_ffn_down, norm_f):
    assert x.shape[0] == 1 and w_ada.shape[0] == 1, "single batch element, single layer"
    tl = _tiles()
    t_lat, d = x.shape[1], x.shape[2]
    ctx_len = ctx.shape[1]
    d_rwkv = k_k.shape[1]
    d_conv = conv_w.shape[2]
    d_ff = w_ffn_gate.shape[2]
    assert t_lat % GRID_W == 0 and ctx_len % CHUNK == 0 and d_rwkv % SEG_W == 0
    x2d, ctx2d = x[0], ctx[0]

    off_v, off_wl = d_rwkv, 2 * d_rwkv
    off_al = off_wl + DECAY_LORA
    off_r = off_al + ICLR_LORA
    off_gl = off_r + d_rwkv
    rwkv_cols = off_gl + GATE_LORA
    wi = w_in[0]
    zd = jnp.zeros((d, LORA_TILE - DECAY_LORA), BF16)
    zi = jnp.zeros((d, LORA_TILE - ICLR_LORA), BF16)
    w_shift = [wi[:, :off_wl].astype(BF16), wi[:, off_r:off_gl].astype(BF16),
               jnp.concatenate([wi[:, off_wl:off_al].astype(BF16), zd, wi[:, off_al:off_r].astype(BF16), zi,
                                wi[:, off_gl:rwkv_cols].astype(BF16)], axis=1)]
    ms = mu_shift[0]
    mu = jnp.concatenate([ms[:off_wl], ms[off_r:off_gl], ms[off_wl:off_al], zd[0].astype(ms.dtype),
                          ms[off_al:off_r], zi[0].astype(ms.dtype), ms[off_gl:]])[None, :]
    w_conv = wi[:, rwkv_cols:].astype(BF16)
    wd = [jnp.zeros((LORA_TILE, d_rwkv), BF16).at[:DECAY_LORA].set(w_decay_up[0, i].astype(BF16)) for i in range(2)]
    wa = [jnp.zeros((LORA_TILE, d_rwkv), BF16).at[:ICLR_LORA].set(w_iclr_up[0, i].astype(BF16)) for i in range(2)]
    wdn = w_ffn_down[0].astype(BF16)
    wo = w_out[0].astype(BF16)

    cc = jnp.zeros((MOD_ROWS, d), F32).at[0].set(c[0]).at[1].set(c_ctx)
    mod = _mod_call(cc, w_ada[0], b_ada, tn=tl["mod_tn"])

    h_all = _norm_call(x2d, ctx2d, norm1, mod, 0, 1, tm=tl["norm_tm"])
    za = _inproj_shift_call(h_all, w_shift, mu, t_lat=t_lat, ctx_len=ctx_len, tm=tl["shift_tm"], tn=tl["shift_tn"])
    y_conv = _inproj_conv_call(h_all, w_conv, conv_w[0], t_lat=t_lat, tm=tl["conv_tm"], tc=tl["conv_tc"])
    n_lat_chunks, n_ctx_chunks = t_lat // CHUNK, ctx_len // CHUNK
    ys = [_scan_call(za, k_k, k_a, w0[0, i:i + 1], a0[0, i:i + 1], wd[i], wa[i], rev=(i == 1),
                     n_lat_chunks=n_lat_chunks, n_ctx_chunks=n_ctx_chunks, d_rwkv=d_rwkv) for i in range(2)]
    y_rwkv = _readout_call(ys[0], ys[1], za, k_a, r_k.reshape(1, d_rwkv), a0[0], wa[0], wa[1],
                           w_gate_up[0].astype(BF16), lnx_w, lnx_b, t_lat=t_lat, d_rwkv=d_rwkv,
                           tm=tl["readout_tm"])
    assert d_rwkv == d_conv
    x1, h2 = _outproj_call(y_rwkv, y_conv, wo, x2d, mod, 2 * d // tl["out_tn"], norm2, 3, 4,
                           tm=tl["out_tm"], tn=tl["out_tn"])

    act = _ffn_up_call(h2, w_ffn_gate[0], w_ffn_up[0], tm=tl["up_tm"], tn=tl["up_tn"])
    x2 = _ffn_down_call(act, wdn, x1, mod, 5 * d // tl["down_tn"], tm=tl["down_tm"], tn=tl["down_tn"])
    out = _final_norm_call(x2, norm_f[None, :], tm=tl["norm_tm"])
    return out[None]
```

```python
import functools
import math

import jax
import jax.numpy as jnp
from jax import lax
from jax.experimental import pallas as pl
from jax.experimental.pallas import tpu as pltpu

F32 = jnp.float32
BF16 = jnp.bfloat16

LANES = 128
MXU_WIDTH = 256
SEG_W = MXU_WIDTH
VMEM_LIMIT_BYTES = 56 * 1024 * 1024

HEAD = 64
GROUP = 128
CHUNK = 64
GRID_W = 64
DECAY_LORA = 96
ICLR_LORA = 96
GATE_LORA = 256
LORA_TILE = 128
LORA_PAD = 2 * LORA_TILE
NORM_EPS = 1e-6
LNX_EPS = 64e-5
EXP_M05 = math.exp(-0.5)


def _cparams(*sem):
    return pltpu.CompilerParams(dimension_semantics=sem, vmem_limit_bytes=VMEM_LIMIT_BYTES)


def _dot(a, b):
    return jnp.dot(a, b, preferred_element_type=F32)


def _bmm(a, b):
    return lax.dot_general(a, b, (((2,), (1,)), ((0,), (0,))), preferred_element_type=F32)


def _bmm_nt(a, b):
    return lax.dot_general(a, b, (((2,), (2,)), ((0,), (0,))), preferred_element_type=F32)


def _bmm_tn(a, b):
    return lax.dot_general(a, b, (((1,), (1,)), ((0,), (0,))), preferred_element_type=F32)


def _split2(x):
    hi = x.astype(BF16)
    lo = (x - hi.astype(F32)).astype(BF16)
    return hi, lo


def _sigmoid(x):
    return 1.0 / (1.0 + jnp.exp(-x))


def _head_sums(x):
    rows, d = x.shape
    n_seg = d // SEG_W
    xs = jnp.concatenate([x[:, s * SEG_W:(s + 1) * SEG_W] for s in range(n_seg)], axis=0)
    rr = lax.broadcasted_iota(jnp.int32, (SEG_W, SEG_W), 0) // HEAD
    cc = lax.broadcasted_iota(jnp.int32, (SEG_W, SEG_W), 1) // HEAD
    ones_seg = (rr == cc).astype(BF16)
    hi, lo = _split2(xs)
    ss = _dot(hi, ones_seg) + _dot(lo, ones_seg)
    return jnp.concatenate([ss[s * rows:(s + 1) * rows] for s in range(n_seg)], axis=1)


def _lora_r_window(win, d_rwkv):
    dec = win[:, :LORA_TILE]
    icl = win[:, DECAY_LORA:DECAY_LORA + LORA_TILE]
    r0 = DECAY_LORA + ICLR_LORA
    return dec, icl, win[:, r0:r0 + d_rwkv]


def _scan_body(k_ref, v_ref, win_ref, kk_ref, ka_ref, w0_ref, a0_ref, wd_ref, wa_ref,
               y_ref, h_ref, *, rev, n_groups):
    c = pl.program_id(0)

    @pl.when(c == 0)
    def _():
        h_ref[...] = jnp.zeros_like(h_ref)

    k = k_ref[...]
    v = v_ref[...]
    la_dec, la_icl, r = _lora_r_window(win_ref[...], k.shape[1])
    dec_pre = w0_ref[...] + _dot(jnp.tanh(la_dec).astype(BF16), wd_ref[...])
    icl_pre = a0_ref[...] + _dot(la_icl.astype(BF16), wa_ref[...])
    lw = -EXP_M05 * _sigmoid(dec_pre)
    a = _sigmoid(icl_pre)
    kd = k * (1.0 + (a - 1.0) * ka_ref[...])
    kkr = k * kk_ref[...]

    row = lax.broadcasted_iota(jnp.int32, (CHUNK, CHUNK), 0)
    col = lax.broadcasted_iota(jnp.int32, (CHUNK, CHUNK), 1)
    tri = ((col >= row) if rev else (col <= row)).astype(BF16)
    lw_hi, lw_lo = _split2(lw)
    L = _dot(tri, lw_hi) + _dot(tri, lw_lo)

    def st(x):
        return jnp.stack([x[:, p * GROUP:(p + 1) * GROUP] for p in range(n_groups)], axis=0)

    lw_s, l_s, a_s, kd_s, v_s, r_s = (st(t) for t in (lw, L, a, kd, v, r))
    ltot = l_s[:, 0:1, :] if rev else l_s[:, CHUNK - 1:CHUNK, :]
    e_l = jnp.exp(l_s)
    e_nl = jnp.exp(-l_s)
    e_lx = jnp.exp(l_s - lw_s)
    e_tl = jnp.exp(ltot - l_s)
    g_tot = jnp.exp(ltot)

    t_i = lax.broadcasted_iota(jnp.int32, (1, CHUNK, GROUP), 1)
    lane = lax.broadcasted_iota(jnp.int32, (1, CHUNK, GROUP), 2)
    s_i = lane & (HEAD - 1)
    head_of_lane = lane // HEAD
    strict = (s_i > t_i) if rev else (s_i < t_i)
    incl = (s_i >= t_i) if rev else (s_i <= t_i)
    eye = (s_i == t_i).astype(F32)
    rr = lax.broadcasted_iota(jnp.int32, (GROUP, GROUP), 0)
    cc = lax.broadcasted_iota(jnp.int32, (GROUP, GROUP), 1)
    same_head = (rr // HEAD) == (cc // HEAD)

    def bd(x):
        z = jnp.zeros_like(x)
        return jnp.concatenate([jnp.where(head_of_lane == g, x, z) for g in range(GROUP // HEAD)], axis=1)

    kk_s = st(kkr / jnp.maximum(jnp.sqrt(_head_sums(kkr * kkr)), 1e-12))
    b_s = kk_s * a_s
    ah = (-kk_s * e_lx).astype(BF16)
    rh = (r_s * e_l).astype(BF16)
    bc = (b_s * e_nl).astype(BF16)
    kc = (kd_s * e_nl).astype(BF16)
    kt = (kd_s * e_tl).astype(BF16)
    bt = (b_s * e_tl).astype(BF16)
    v_b = v_s.astype(BF16)

    sc = _bmm_nt(jnp.concatenate([ah, rh], axis=1),
                 jnp.concatenate([bd(bc), bd(kc)], axis=1))
    m_ab = jnp.where(strict, sc[:, :CHUNK, :GROUP], 0.0)
    m_ak = jnp.where(strict, sc[:, :CHUNK, GROUP:], 0.0).astype(BF16)
    n_rb = jnp.where(incl, sc[:, CHUNK:, :GROUP], 0.0).astype(BF16)
    n_rk = jnp.where(incl, sc[:, CHUNK:, GROUP:], 0.0).astype(BF16)

    s_acc = eye + m_ab
    m_pow = m_ab.astype(BF16)
    m_pow = _bmm(m_pow, bd(m_pow))
    n_steps = int(math.log2(CHUNK)) - 1
    for j in range(n_steps):
        mb = m_pow.astype(BF16)
        if j < n_steps - 1:
            both = _bmm(mb, jnp.concatenate([bd(mb), bd(s_acc.astype(BF16))], axis=2))
            m_pow = both[:, :, :GROUP]
            s_acc = s_acc + both[:, :, GROUP:]
        else:
            s_acc = s_acc + _bmm(mb, bd(s_acc.astype(BF16)))
    t_inv = s_acc.astype(BF16)

    ht0 = h_ref[...]
    ht0_b = ht0.astype(BF16)
    v_bd = bd(v_b)
    from_h0 = _bmm_nt(jnp.concatenate([ah, rh], axis=1), ht0_b)
    w = from_h0[:, :CHUNK] + _bmm(m_ak, v_bd)
    u_b = _bmm(t_inv, bd(w.astype(BF16))).astype(BF16)
    y = from_h0[:, CHUNK:] + _bmm(jnp.concatenate([n_rk, n_rb], axis=2),
                                  jnp.concatenate([v_bd, bd(u_b)], axis=1))
    y_ref[...] = jnp.concatenate([y[p] for p in range(n_groups)], axis=1)
    upd = _bmm_tn(jnp.concatenate([v_b, u_b], axis=1), jnp.concatenate([kt, bt], axis=1))
    h_ref[...] = ht0 * g_tot + jnp.where(same_head[None], upd, 0.0)


def _window_spec(rows, row_of, col0, width):
    assert col0 % LANES == 0 and width % LANES == 0
    return pl.BlockSpec((pl.Element(rows), pl.Element(width)),
                        lambda c: (pl.multiple_of(row_of(c) * rows, rows), col0))


def _lora_r_width(d_rwkv):
    return -(-(DECAY_LORA + ICLR_LORA + d_rwkv) // LANES) * LANES


def _scan_call(za, kk, ka, w0, a0, wd, wa, *, rev, n_lat_chunks, n_ctx_chunks, d_rwkv):
    n_chunks = n_lat_chunks + n_ctx_chunks
    n_groups = d_rwkv // GROUP
    assert CHUNK == HEAD and d_rwkv % GROUP == 0

    def chunk_of(c):
        if rev:
            return n_chunks - 1 - c
        return jnp.where(c < n_ctx_chunks, n_lat_chunks + c, c - n_ctx_chunks)

    vec = pl.BlockSpec((1, d_rwkv), lambda c: (0, 0))
    mat = pl.BlockSpec((LORA_TILE, d_rwkv), lambda c: (0, 0))
    return pl.pallas_call(
        functools.partial(_scan_body, rev=rev, n_groups=n_groups),
        grid=(n_chunks,),
        in_specs=[
            pl.BlockSpec((CHUNK, d_rwkv), lambda c: (chunk_of(c), 0)),
            pl.BlockSpec((CHUNK, d_rwkv), lambda c: (chunk_of(c), 1)),
            _window_spec(CHUNK, chunk_of, 2 * d_rwkv, _lora_r_width(d_rwkv)),
            vec, vec, vec, vec, mat, mat,
        ],
        out_specs=pl.BlockSpec((CHUNK, d_rwkv), lambda c: (chunk_of(c), 0)),
        out_shape=jax.ShapeDtypeStruct((n_chunks * CHUNK, d_rwkv), F32),
        scratch_shapes=[pltpu.VMEM((n_groups, GROUP, GROUP), F32)],
        compiler_params=_cparams("arbitrary"),
        name="wkv_scan_rev" if rev else "wkv_scan_fwd",
    )(za, za, za, kk, ka, w0, a0, wd, wa)


MOD_ROWS = 8


def _mod_body(cc_ref, w_ref, b_ref, o_ref):
    cc = cc_ref[...]
    s = cc * _sigmoid(cc)
    s_hi, s_lo = _split2(s)
    w = w_ref[...].astype(BF16)
    o_ref[...] = _dot(s_hi, w) + _dot(s_lo, w) + b_ref[...]


def _mod_call(cc, w_ada, b_ada, *, tn):
    d, n = w_ada.shape
    return pl.pallas_call(
        _mod_body,
        grid=(n // tn,),
        in_specs=[pl.BlockSpec((MOD_ROWS, d), lambda j: (0, 0)),
                  pl.BlockSpec((d, tn), lambda j: (0, j)),
                  pl.BlockSpec((1, tn), lambda j: (0, j))],
        out_specs=pl.BlockSpec((MOD_ROWS, tn), lambda j: (0, j)),
        out_shape=jax.ShapeDtypeStruct((MOD_ROWS, n), F32),
        compiler_params=_cparams("arbitrary"),
        name="adaln_mod",
    )(cc, w_ada, b_ada)


def _norm_mod(xf, gain, shift, scale):
    y = xf * lax.rsqrt(jnp.mean(xf * xf, axis=-1, keepdims=True) + NORM_EPS) * gain
    return y * (1.0 + scale) + shift


def _norm_ctx_body(x_ref, ctx_ref, g_ref, sh_ref, sc_ref, o_ref, *, n_lat_blocks):
    i = pl.program_id(0)

    @pl.when(i < n_lat_blocks)
    def _():
        o_ref[...] = _norm_mod(x_ref[...], g_ref[...], sh_ref[0:1, :], sc_ref[0:1, :]).astype(o_ref.dtype)

    @pl.when(i >= n_lat_blocks)
    def _():
        o_ref[...] = _norm_mod(ctx_ref[...], g_ref[...], sh_ref[1:2, :], sc_ref[1:2, :]).astype(o_ref.dtype)


def _norm_call(x, ctx, gain, mod, shift_blk, scale_blk, *, tm):
    t, d = x.shape
    n_lat = t // tm
    vec = pl.BlockSpec((1, d), lambda i: (0, 0))
    sh = pl.BlockSpec((MOD_ROWS, d), lambda i: (0, shift_blk))
    sc = pl.BlockSpec((MOD_ROWS, d), lambda i: (0, scale_blk))
    tc = ctx.shape[0]
    n_ctx = tc // tm
    return pl.pallas_call(
        functools.partial(_norm_ctx_body, n_lat_blocks=n_lat), grid=(n_lat + n_ctx,),
        in_specs=[pl.BlockSpec((tm, d), lambda i: (jnp.minimum(i, n_lat - 1), 0)),
                  pl.BlockSpec((tm, d), lambda i: (jnp.maximum(i - n_lat, 0), 0)), vec, sh, sc],
        out_specs=pl.BlockSpec((tm, d), lambda i: (i, 0)),
        out_shape=jax.ShapeDtypeStruct((t + tc, d), BF16),
        compiler_params=_cparams("arbitrary"), name="norm_mod_ctx",
    )(x, ctx, gain, mod, mod)


def _neighbours(z, row0, t_lat, ctx_len):
    tm = z.shape[0]
    g = row0 + lax.broadcasted_iota(jnp.int32, (tm, 1), 0)
    is_ctx = g >= t_lat
    pos = jnp.where(is_ctx, g - t_lat, g & (GRID_W - 1))
    last = jnp.where(is_ctx, ctx_len - 1, GRID_W - 1)
    prev = jnp.where(pos == 0, 0.0, pltpu.roll(z, 1, 0))
    nxt = jnp.where(pos == last, 0.0, pltpu.roll(z, tm - 1, 0))
    return prev, nxt


def _inproj_shift_body(h_ref, w_ref, mu_ref, o_ref, *, t_lat, ctx_len, n_sub):
    w = w_ref[...].astype(BF16)
    tm = h_ref.shape[0]
    ts = tm // n_sub
    for s in range(n_sub):
        rows = slice(s * ts, (s + 1) * ts)
        z = _dot(h_ref[rows, :], w)
        prev, nxt = _neighbours(z, pl.program_id(0) * tm + s * ts, t_lat, ctx_len)
        o_ref[rows, :] = z + mu_ref[...] * (0.5 * (prev + nxt) - z)


def _inproj_shift_call(h_all, w_in, mu, *, n_cols, t_lat, ctx_len, tm, tn, n_sub):
    t, d = h_all.shape
    assert n_cols % tn == 0 and n_cols <= w_in.shape[1] and tm % (n_sub * GRID_W) == 0
    return pl.pallas_call(
        functools.partial(_inproj_shift_body, t_lat=t_lat, ctx_len=ctx_len, n_sub=n_sub),
        grid=(t // tm, n_cols // tn),
        in_specs=[pl.BlockSpec((tm, d), lambda i, j: (i, 0)),
                  pl.BlockSpec((d, tn), lambda i, j: (0, j)),
                  pl.BlockSpec((1, tn), lambda i, j: (0, j))],
        out_specs=pl.BlockSpec((tm, tn), lambda i, j: (i, j)),
        out_shape=jax.ShapeDtypeStruct((t, n_cols), F32),
        compiler_params=_cparams("arbitrary", "arbitrary"), name="inproj_shift",
    )(h_all, w_in, mu)


def _inproj_conv_body(h_ref, wb_ref, wc_ref, wx_ref, cw_ref, o_ref, *, t_lat, lead, d_conv):
    h = h_ref[...]
    u = _dot(h, wc_ref[...].astype(BF16)) * _dot(h, wx_ref[...].astype(BF16))
    prev, nxt = _neighbours(u, pl.program_id(0) * u.shape[0], t_lat, 1)
    cw = cw_ref[...]
    conv = cw[0:1, :] * prev + cw[1:2, :] * u + cw[2:3, :] * nxt
    y = _dot(h, wb_ref[...].astype(BF16)) * conv
    tc = o_ref.shape[1]
    ch = pl.program_id(1) * tc - lead + lax.broadcasted_iota(jnp.int32, (1, tc), 1)
    o_ref[...] = jnp.where((ch >= 0) & (ch < d_conv), y, 0.0).astype(o_ref.dtype)


def _inproj_conv_call(h_all, w_in, conv_w, *, col0, t_lat, tm, tc):
    d = h_all.shape[1]
    d_conv = conv_w.shape[1]
    assert d_conv % tc == 0
    lead = col0 % tc
    blk0 = col0 // tc
    per = d_conv // tc
    n_tiles = per + (1 if lead else 0)
    cw = jnp.pad(conv_w, ((0, MOD_ROWS - 3), (lead, n_tiles * tc - d_conv - lead)))
    y = pl.pallas_call(
        functools.partial(_inproj_conv_body, t_lat=t_lat, lead=lead, d_conv=d_conv),
        grid=(t_lat // tm, n_tiles),
        in_specs=[pl.BlockSpec((tm, d), lambda i, j: (i, 0)),
                  pl.BlockSpec((d, tc), lambda i, j: (0, blk0 + j)),
                  pl.BlockSpec((d, tc), lambda i, j: (0, blk0 + per + j)),
                  pl.BlockSpec((d, tc), lambda i, j: (0, blk0 + 2 * per + j)),
                  pl.BlockSpec((MOD_ROWS, tc), lambda i, j: (0, j))],
        out_specs=pl.BlockSpec((tm, tc), lambda i, j: (i, j)),
        out_shape=jax.ShapeDtypeStruct((t_lat, n_tiles * tc), BF16),
        compiler_params=_cparams("arbitrary", "arbitrary"), name="inproj_conv",
    )(h_all, w_in, w_in, w_in, cw)
    return y, lead


def _readout_body(y0_ref, y1_ref, k_ref, v_ref, win_ref, gwin_ref, ka_ref, rk_ref, a0_ref,
                  wa0_ref, wa1_ref, wg_ref, lw_ref, lb_ref, o_ref, *, gate_lead):
    _, la_icl, r = _lora_r_window(win_ref[...], k_ref.shape[1])
    la = la_icl.astype(BF16)
    a0 = a0_ref[...]
    a_sum = _sigmoid(a0[0:1, :] + _dot(la, wa0_ref[...])) + _sigmoid(a0[1:2, :] + _dot(la, wa1_ref[...]))
    coef = r * rk_ref[...] * k_ref[...] * (2.0 + (a_sum - 2.0) * ka_ref[...])
    y = y0_ref[...] + y1_ref[...]
    gl = gwin_ref[:, gate_lead:gate_lead + GATE_LORA]
    g = _dot(_sigmoid(gl).astype(BF16), wg_ref[...])

    yc = y - _head_sums(y) * (1.0 / HEAD)
    var = _head_sums(yc * yc) * (1.0 / HEAD)
    yn = yc * lax.rsqrt(var + LNX_EPS) * lw_ref[...] + lb_ref[...]
    bonus = _head_sums(coef) * v_ref[...]
    o_ref[...] = ((yn + bonus) * g).astype(o_ref.dtype)


def _readout_call(y0, y1, za, k_a, r_k, a0, wa0, wa1, w_gate, lnx_w, lnx_b, *, t_lat, d_rwkv, tm):
    row = lambda jb: pl.BlockSpec((tm, d_rwkv), lambda i: (i, jb))
    vec = pl.BlockSpec((1, d_rwkv), lambda i: (0, 0))
    mat = pl.BlockSpec((LORA_TILE, d_rwkv), lambda i: (0, 0))
    off_gl = 3 * d_rwkv + DECAY_LORA + ICLR_LORA
    gate_lead = off_gl % LANES
    gate_w = -(-(gate_lead + GATE_LORA) // LANES) * LANES
    assert off_gl - gate_lead + gate_w <= za.shape[1]
    return pl.pallas_call(
        functools.partial(_readout_body, gate_lead=gate_lead), grid=(t_lat // tm,),
        in_specs=[row(0), row(0), row(0), row(1),
                  _window_spec(tm, lambda i: i, 2 * d_rwkv, _lora_r_width(d_rwkv)),
                  _window_spec(tm, lambda i: i, off_gl - gate_lead, gate_w),
                  vec, vec, pl.BlockSpec((2, d_rwkv), lambda i: (0, 0)),
                  mat, mat, pl.BlockSpec((GATE_LORA, d_rwkv), lambda i: (0, 0)), vec, vec],
        out_specs=pl.BlockSpec((tm, d_rwkv), lambda i: (i, 0)),
        out_shape=jax.ShapeDtypeStruct((t_lat, d_rwkv), BF16),
        compiler_params=_cparams("arbitrary"), name="rwkv_readout",
    )(y0, y1, za, za, za, za, k_a, r_k, a0, wa0, wa1, w_gate, lnx_w, lnx_b)


def _outproj_body(ya_ref, yb_ref, wa_ref, wb_ref, x_ref, g_ref, n2_ref, sh_ref, sc_ref, o_ref, h_ref, rows_ref):
    j = pl.program_id(1)
    n_tiles, _, tn = rows_ref.shape
    acc = _dot(ya_ref[...], wa_ref[...]) + _dot(yb_ref[...], wb_ref[...])
    x1 = x_ref[...] + g_ref[0:1, :] * acc
    o_ref[...] = x1
    rows_ref[j] = x1

    @pl.when(j == n_tiles - 1)
    def _():
        ss = sum(jnp.sum(rows_ref[s] * rows_ref[s], axis=-1, keepdims=True) for s in range(n_tiles))
        rs = lax.rsqrt(ss * (1.0 / (n_tiles * tn)) + NORM_EPS)
        for s in range(n_tiles):
            sl = slice(s * tn, (s + 1) * tn)
            y = rows_ref[s] * rs * n2_ref[:, sl]
            h_ref[:, sl] = (y * (1.0 + sc_ref[0:1, sl]) + sh_ref[0:1, sl]).astype(h_ref.dtype)


def _outproj_call(ya, yb, wa, wb, x, mod, gate_blk0, gain2, shift_blk, scale_blk, *, tm, tn):
    t, da = ya.shape
    db = yb.shape[1]
    n = wa.shape[1]
    return pl.pallas_call(
        _outproj_body, grid=(t // tm, n // tn),
        in_specs=[pl.BlockSpec((tm, da), lambda i, j: (i, 0)),
                  pl.BlockSpec((tm, db), lambda i, j: (i, 0)),
                  pl.BlockSpec((da, tn), lambda i, j: (0, j)),
                  pl.BlockSpec((db, tn), lambda i, j: (0, j)),
                  pl.BlockSpec((tm, tn), lambda i, j: (i, j)),
                  pl.BlockSpec((MOD_ROWS, tn), lambda i, j: (0, gate_blk0 + j)),
                  pl.BlockSpec((1, n), lambda i, j: (0, 0)),
                  pl.BlockSpec((MOD_ROWS, n), lambda i, j: (0, shift_blk)),
                  pl.BlockSpec((MOD_ROWS, n), lambda i, j: (0, scale_blk))],
        out_specs=[pl.BlockSpec((tm, tn), lambda i, j: (i, j)),
                   pl.BlockSpec((tm, n), lambda i, j: (i, 0))],
        out_shape=[jax.ShapeDtypeStruct((t, n), F32), jax.ShapeDtypeStruct((t, n), BF16)],
        scratch_shapes=[pltpu.VMEM((n // tn, tm, tn), F32)],
        compiler_params=_cparams("arbitrary", "arbitrary"), name="outproj_residual_norm",
    )(ya, yb, wa, wb, x, mod, gain2, mod, mod)


def _ffn_up_body(h_ref, wg_ref, wu_ref, o_ref):
    h = h_ref[...]
    g = _dot(h, wg_ref[...].astype(BF16))
    u = _dot(h, wu_ref[...].astype(BF16))
    o_ref[...] = (g * _sigmoid(g) * u).astype(o_ref.dtype)


def _ffn_up_call(h, wg, wu, *, tm, tn):
    t, d = h.shape
    n = wg.shape[1]
    return pl.pallas_call(
        _ffn_up_body, grid=(t // tm, n // tn),
        in_specs=[pl.BlockSpec((tm, d), lambda i, j: (i, 0)),
                  pl.BlockSpec((d, tn), lambda i, j: (0, j)),
                  pl.BlockSpec((d, tn), lambda i, j: (0, j))],
        out_specs=pl.BlockSpec((tm, tn), lambda i, j: (i, j)),
        out_shape=jax.ShapeDtypeStruct((t, n), BF16),
        compiler_params=_cparams("arbitrary", "arbitrary"), name="ffn_gate_up",
    )(h, wg, wu)


def _ffn_down_body(a_ref, w_ref, x_ref, g_ref, o_ref):
    o_ref[...] = x_ref[...] + g_ref[0:1, :] * _dot(a_ref[...], w_ref[...])


def _ffn_down_call(act, w, x, mod, gate_blk0, *, tm, tn):
    t, kdim = act.shape
    d = w.shape[1]
    return pl.pallas_call(
        _ffn_down_body, grid=(t // tm, d // tn),
        in_specs=[pl.BlockSpec((tm, kdim), lambda i, j: (i, 0)),
                  pl.BlockSpec((kdim, tn), lambda i, j: (0, j)),
                  pl.BlockSpec((tm, tn), lambda i, j: (i, j)),
                  pl.BlockSpec((MOD_ROWS, tn), lambda i, j: (0, gate_blk0 + j))],
        out_specs=pl.BlockSpec((tm, tn), lambda i, j: (i, j)),
        out_shape=jax.ShapeDtypeStruct((t, d), F32),
        compiler_params=_cparams("arbitrary", "arbitrary"), name="ffn_down_residual",
    )(act, w, x, mod)


def _final_norm_body(x_ref, g_ref, o_ref):
    x = x_ref[...]
    o_ref[...] = x * lax.rsqrt(jnp.mean(x * x, axis=-1, keepdims=True) + NORM_EPS) * g_ref[...]


def _final_norm_call(x, gain, *, tm):
    t, d = x.shape
    return pl.pallas_call(
        _final_norm_body, grid=(t // tm,),
        in_specs=[pl.BlockSpec((tm, d), lambda i: (i, 0)), pl.BlockSpec((1, d), lambda i: (0, 0))],
        out_specs=pl.BlockSpec((tm, d), lambda i: (i, 0)),
        out_shape=jax.ShapeDtypeStruct((t, d), F32),
        compiler_params=_cparams("arbitrary"), name="final_norm",
    )(x, gain)


def _tiles():
    return dict(mod_tn=1024, norm_tm=256, shift_tm=1408, shift_tn=256, shift_sub=2, conv_tm=1024, conv_tc=256,
                readout_tm=256, out_tm=512, out_tn=512, up_tm=1024, up_tn=256, down_tm=512, down_tn=512)


def kernel(x, c, ctx, c_ctx, w_ada, b_ada, norm1, w_in, mu_shift, k_k, k_a, r_k, w0, w_decay_up, a0, w_iclr_up, w_gate_up, lnx_w, lnx_b, conv_w, w_out, norm2, w_ffn_gate, w_ffn_up, w_ffn_down, norm_f):
    assert x.shape[0] == 1 and w_ada.shape[0] == 1, "single batch element, single layer"
    tl = _tiles()
    t_lat, d = x.shape[1], x.shape[2]
    ctx_len = ctx.shape[1]
    d_rwkv = k_k.shape[1]
    d_conv = conv_w.shape[2]
    d_ff = w_ffn_gate.shape[2]
    assert t_lat % GRID_W == 0 and ctx_len % CHUNK == 0 and d_rwkv % SEG_W == 0
    x2d, ctx2d = x[0], ctx[0]

    rwkv_cols = 3 * d_rwkv + DECAY_LORA + ICLR_LORA + GATE_LORA
    wi = w_in[0]
    shift_cols = -(-rwkv_cols // tl["shift_tn"]) * tl["shift_tn"]
    assert shift_cols <= wi.shape[1]
    mu = jnp.pad(mu_shift[0], (0, shift_cols - rwkv_cols))[None, :]
    wd = [jnp.zeros((LORA_TILE, d_rwkv), BF16).at[:DECAY_LORA].set(w_decay_up[0, i].astype(BF16)) for i in range(2)]
    wa = [jnp.zeros((LORA_TILE, d_rwkv), BF16).at[:ICLR_LORA].set(w_iclr_up[0, i].astype(BF16)) for i in range(2)]
    wdn = w_ffn_down[0].astype(BF16)

    cc = jnp.zeros((MOD_ROWS, d), F32).at[0].set(c[0]).at[1].set(c_ctx)
    mod = _mod_call(cc, w_ada[0], b_ada, tn=tl["mod_tn"])

    h_all = _norm_call(x2d, ctx2d, norm1, mod, 0, 1, tm=tl["norm_tm"])
    za = _inproj_shift_call(h_all, wi, mu, n_cols=shift_cols, t_lat=t_lat, ctx_len=ctx_len,
                            tm=tl["shift_tm"], tn=tl["shift_tn"], n_sub=tl["shift_sub"])
    y_conv, lead = _inproj_conv_call(h_all, wi, conv_w[0], col0=rwkv_cols, t_lat=t_lat,
                                     tm=tl["conv_tm"], tc=tl["conv_tc"])
    n_lat_chunks, n_ctx_chunks = t_lat // CHUNK, ctx_len // CHUNK
    ys = [_scan_call(za, k_k, k_a, w0[0, i:i + 1], a0[0, i:i + 1], wd[i], wa[i], rev=(i == 1),
                     n_lat_chunks=n_lat_chunks, n_ctx_chunks=n_ctx_chunks, d_rwkv=d_rwkv) for i in range(2)]
    y_rwkv = _readout_call(ys[0], ys[1], za, k_a, r_k.reshape(1, d_rwkv), a0[0], wa[0], wa[1],
                           w_gate_up[0].astype(BF16), lnx_w, lnx_b, t_lat=t_lat, d_rwkv=d_rwkv,
                           tm=tl["readout_tm"])
    wo_a = w_out[0, :d_rwkv].astype(BF16)
    wo_b = jnp.pad(w_out[0, d_rwkv:], ((lead, y_conv.shape[1] - d_conv - lead), (0, 0))).astype(BF16)
    x1, h2 = _outproj_call(y_rwkv, y_conv, wo_a, wo_b, x2d, mod, 2 * d // tl["out_tn"], norm2, 3, 4,
                           tm=tl["out_tm"], tn=tl["out_tn"])

    act = _ffn_up_call(h2, w_ffn_gate[0], w_ffn_up[0], tm=tl["up_tm"], tn=tl["up_tn"])
    x2 = _ffn_down_call(act, wdn, x1, mod, 5 * d // tl["down_tn"], tm=tl["down_tm"], tn=tl["down_tn"])
    out = _final_norm_call(x2, norm_f[None, :], tm=tl["norm_tm"])
    return out[None]
```

```python
import functools
import math

import jax
import jax.numpy as jnp
from jax import lax
from jax.experimental import pallas as pl
from jax.experimental.pallas import tpu as pltpu

F32 = jnp.float32
BF16 = jnp.bfloat16

SUBLANES = 8
MXU_WIDTH = 256
SEG_W = MXU_WIDTH
VMEM_LIMIT_BYTES = 56 * 1024 * 1024

HEAD = 64
GROUP = 128
CHUNK = 64
GRID_W = 64
DECAY_LORA = 96
ICLR_LORA = 96
GATE_LORA = 256
LORA_TILE = 128
LORA_PAD = 2 * LORA_TILE
NORM_EPS = 1e-6
LNX_EPS = 64e-5
EXP_M05 = math.exp(-0.5)


def _cparams(*sem):
    return pltpu.CompilerParams(dimension_semantics=sem, vmem_limit_bytes=VMEM_LIMIT_BYTES)


def _dot(a, b):
    return jnp.dot(a, b, preferred_element_type=F32)


def _bmm(a, b):
    return lax.dot_general(a, b, (((2,), (1,)), ((0,), (0,))), preferred_element_type=F32)


def _bmm_nt(a, b):
    return lax.dot_general(a, b, (((2,), (2,)), ((0,), (0,))), preferred_element_type=F32)


def _bmm_tn(a, b):
    return lax.dot_general(a, b, (((1,), (1,)), ((0,), (0,))), preferred_element_type=F32)


def _split2(x):
    hi = x.astype(BF16)
    lo = (x - hi.astype(F32)).astype(BF16)
    return hi, lo


def _sigmoid(x):
    return 1.0 / (1.0 + jnp.exp(-x))


def _head_sums(x):
    rows, d = x.shape
    n_seg = d // SEG_W
    xs = jnp.concatenate([x[:, s * SEG_W:(s + 1) * SEG_W] for s in range(n_seg)], axis=0)
    rr = lax.broadcasted_iota(jnp.int32, (SEG_W, SEG_W), 0) // HEAD
    cc = lax.broadcasted_iota(jnp.int32, (SEG_W, SEG_W), 1) // HEAD
    ones_seg = (rr == cc).astype(BF16)
    hi, lo = _split2(xs)
    ss = _dot(hi, ones_seg) + _dot(lo, ones_seg)
    return jnp.concatenate([ss[s * rows:(s + 1) * rows] for s in range(n_seg)], axis=1)


def _lora_tiles(la):
    return la[:, :LORA_TILE], la[:, DECAY_LORA:DECAY_LORA + LORA_TILE]


def _scan_body(k_ref, v_ref, r_ref, la_ref, kk_ref, ka_ref, w0_ref, a0_ref, wd_ref, wa_ref,
               y_ref, h_ref, *, rev, n_groups):
    c = pl.program_id(0)

    @pl.when(c == 0)
    def _():
        h_ref[...] = jnp.zeros_like(h_ref)

    k = k_ref[...]
    v = v_ref[...]
    r = r_ref[...]
    la_dec, la_icl = _lora_tiles(la_ref[...])
    dec_pre = w0_ref[...] + _dot(jnp.tanh(la_dec).astype(BF16), wd_ref[...])
    icl_pre = a0_ref[...] + _dot(la_icl.astype(BF16), wa_ref[...])
    lw = -EXP_M05 * _sigmoid(dec_pre)
    a = _sigmoid(icl_pre)
    kd = k * (1.0 + (a - 1.0) * ka_ref[...])
    kkr = k * kk_ref[...]

    row = lax.broadcasted_iota(jnp.int32, (CHUNK, CHUNK), 0)
    col = lax.broadcasted_iota(jnp.int32, (CHUNK, CHUNK), 1)
    tri = ((col >= row) if rev else (col <= row)).astype(BF16)
    lw_hi, lw_lo = _split2(lw)
    L = _dot(tri, lw_hi) + _dot(tri, lw_lo)

    def st(x):
        return jnp.stack([x[:, p * GROUP:(p + 1) * GROUP] for p in range(n_groups)], axis=0)

    lw_s, l_s, a_s, kd_s, v_s, r_s = (st(t) for t in (lw, L, a, kd, v, r))
    ltot = l_s[:, 0:1, :] if rev else l_s[:, CHUNK - 1:CHUNK, :]
    e_l = jnp.exp(l_s)
    e_nl = jnp.exp(-l_s)
    e_lx = jnp.exp(l_s - lw_s)
    e_tl = jnp.exp(ltot - l_s)
    g_tot = jnp.exp(ltot)

    t_i = lax.broadcasted_iota(jnp.int32, (1, CHUNK, GROUP), 1)
    lane = lax.broadcasted_iota(jnp.int32, (1, CHUNK, GROUP), 2)
    s_i = lane & (HEAD - 1)
    head_of_lane = lane // HEAD
    strict = (s_i > t_i) if rev else (s_i < t_i)
    incl = (s_i >= t_i) if rev else (s_i <= t_i)
    eye = (s_i == t_i).astype(F32)
    rr = lax.broadcasted_iota(jnp.int32, (GROUP, GROUP), 0)
    cc = lax.broadcasted_iota(jnp.int32, (GROUP, GROUP), 1)
    same_head = (rr // HEAD) == (cc // HEAD)

    def bd(x):
        z = jnp.zeros_like(x)
        return jnp.concatenate([jnp.where(head_of_lane == g, x, z) for g in range(GROUP // HEAD)], axis=1)

    kk_s = st(kkr / jnp.maximum(jnp.sqrt(_head_sums(kkr * kkr)), 1e-12))
    b_s = kk_s * a_s
    ah = (-kk_s * e_lx).astype(BF16)
    rh = (r_s * e_l).astype(BF16)
    bc = (b_s * e_nl).astype(BF16)
    kc = (kd_s * e_nl).astype(BF16)
    kt = (kd_s * e_tl).astype(BF16)
    bt = (b_s * e_tl).astype(BF16)
    v_b = v_s.astype(BF16)

    sc = _bmm_nt(jnp.concatenate([ah, rh], axis=1),
                 jnp.concatenate([bd(bc), bd(kc)], axis=1))
    m_ab = jnp.where(strict, sc[:, :CHUNK, :GROUP], 0.0)
    m_ak = jnp.where(strict, sc[:, :CHUNK, GROUP:], 0.0).astype(BF16)
    n_rb = jnp.where(incl, sc[:, CHUNK:, :GROUP], 0.0).astype(BF16)
    n_rk = jnp.where(incl, sc[:, CHUNK:, GROUP:], 0.0).astype(BF16)

    s_acc = eye + m_ab
    m_pow = m_ab.astype(BF16)
    m_pow = _bmm(m_pow, bd(m_pow))
    n_steps = int(math.log2(CHUNK)) - 1
    for j in range(n_steps):
        mb = m_pow.astype(BF16)
        if j < n_steps - 1:
            both = _bmm(mb, jnp.concatenate([bd(mb), bd(s_acc.astype(BF16))], axis=2))
            m_pow = both[:, :, :GROUP]
            s_acc = s_acc + both[:, :, GROUP:]
        else:
            s_acc = s_acc + _bmm(mb, bd(s_acc.astype(BF16)))
    t_inv = s_acc.astype(BF16)

    ht0 = h_ref[...]
    ht0_b = ht0.astype(BF16)
    v_bd = bd(v_b)
    from_h0 = _bmm_nt(jnp.concatenate([ah, rh], axis=1), ht0_b)
    w = from_h0[:, :CHUNK] + _bmm(m_ak, v_bd)
    u_b = _bmm(t_inv, bd(w.astype(BF16))).astype(BF16)
    y = from_h0[:, CHUNK:] + _bmm(jnp.concatenate([n_rk, n_rb], axis=2),
                                  jnp.concatenate([v_bd, bd(u_b)], axis=1))
    y_ref[...] = jnp.concatenate([y[p] for p in range(n_groups)], axis=1)
    upd = _bmm_tn(jnp.concatenate([v_b, u_b], axis=1), jnp.concatenate([kt, bt], axis=1))
    h_ref[...] = ht0 * g_tot + jnp.where(same_head[None], upd, 0.0)


def _scan_call(za, kk, ka, w0, a0, wd, wa, *, rev, n_lat_chunks, n_ctx_chunks, d_rwkv):
    n_chunks = n_lat_chunks + n_ctx_chunks
    n_groups = d_rwkv // GROUP
    lora_blk = 3 * d_rwkv // LORA_PAD
    assert CHUNK == HEAD and d_rwkv % GROUP == 0

    def chunk_of(c):
        if rev:
            return n_chunks - 1 - c
        return jnp.where(c < n_ctx_chunks, n_lat_chunks + c, c - n_ctx_chunks)

    vec = pl.BlockSpec((1, d_rwkv), lambda c: (0, 0))
    mat = pl.BlockSpec((LORA_TILE, d_rwkv), lambda c: (0, 0))
    return pl.pallas_call(
        functools.partial(_scan_body, rev=rev, n_groups=n_groups),
        grid=(n_chunks,),
        in_specs=[
            pl.BlockSpec((CHUNK, d_rwkv), lambda c: (chunk_of(c), 0)),
            pl.BlockSpec((CHUNK, d_rwkv), lambda c: (chunk_of(c), 1)),
            pl.BlockSpec((CHUNK, d_rwkv), lambda c: (chunk_of(c), 2)),
            pl.BlockSpec((CHUNK, LORA_PAD), lambda c: (chunk_of(c), lora_blk)),
            vec, vec, vec, vec, mat, mat,
        ],
        out_specs=pl.BlockSpec((CHUNK, d_rwkv), lambda c: (chunk_of(c), 0)),
        out_shape=jax.ShapeDtypeStruct((n_chunks * CHUNK, d_rwkv), F32),
        scratch_shapes=[pltpu.VMEM((n_groups, GROUP, GROUP), F32)],
        compiler_params=_cparams("arbitrary"),
        name="wkv_scan_rev" if rev else "wkv_scan_fwd",
    )(za, za, za, za, kk, ka, w0, a0, wd, wa)


MOD_ROWS = 8


def _mod_body(cc_ref, w_ref, b_ref, o_ref):
    cc = cc_ref[...]
    s = cc * _sigmoid(cc)
    s_hi, s_lo = _split2(s)
    w = w_ref[...].astype(BF16)
    o_ref[...] = _dot(s_hi, w) + _dot(s_lo, w) + b_ref[...]


def _mod_call(cc, w_ada, b_ada, *, tn):
    d, n = w_ada.shape
    return pl.pallas_call(
        _mod_body,
        grid=(n // tn,),
        in_specs=[pl.BlockSpec((MOD_ROWS, d), lambda j: (0, 0)),
                  pl.BlockSpec((d, tn), lambda j: (0, j)),
                  pl.BlockSpec((1, tn), lambda j: (0, j))],
        out_specs=pl.BlockSpec((MOD_ROWS, tn), lambda j: (0, j)),
        out_shape=jax.ShapeDtypeStruct((MOD_ROWS, n), F32),
        compiler_params=_cparams("arbitrary"),
        name="adaln_mod",
    )(cc, w_ada, b_ada)


def _norm_mod(xf, gain, shift, scale):
    y = xf * lax.rsqrt(jnp.mean(xf * xf, axis=-1, keepdims=True) + NORM_EPS) * gain
    return y * (1.0 + scale) + shift


def _norm_ctx_body(x_ref, ctx_ref, g_ref, sh_ref, sc_ref, o_ref, *, n_lat_blocks):
    i = pl.program_id(0)

    @pl.when(i < n_lat_blocks)
    def _():
        o_ref[...] = _norm_mod(x_ref[...], g_ref[...], sh_ref[0:1, :], sc_ref[0:1, :]).astype(o_ref.dtype)

    @pl.when(i >= n_lat_blocks)
    def _():
        o_ref[...] = _norm_mod(ctx_ref[...], g_ref[...], sh_ref[1:2, :], sc_ref[1:2, :]).astype(o_ref.dtype)


def _norm_call(x, ctx, gain, mod, shift_blk, scale_blk, *, tm):
    t, d = x.shape
    n_lat = t // tm
    vec = pl.BlockSpec((1, d), lambda i: (0, 0))
    sh = pl.BlockSpec((MOD_ROWS, d), lambda i: (0, shift_blk))
    sc = pl.BlockSpec((MOD_ROWS, d), lambda i: (0, scale_blk))
    tc = ctx.shape[0]
    n_ctx = tc // tm
    return pl.pallas_call(
        functools.partial(_norm_ctx_body, n_lat_blocks=n_lat), grid=(n_lat + n_ctx,),
        in_specs=[pl.BlockSpec((tm, d), lambda i: (jnp.minimum(i, n_lat - 1), 0)),
                  pl.BlockSpec((tm, d), lambda i: (jnp.maximum(i - n_lat, 0), 0)), vec, sh, sc],
        out_specs=pl.BlockSpec((tm, d), lambda i: (i, 0)),
        out_shape=jax.ShapeDtypeStruct((t + tc, d), BF16),
        compiler_params=_cparams("arbitrary"), name="norm_mod_ctx",
    )(x, ctx, gain, mod, mod)


def _neighbours(z, row0, t_lat, ctx_len):
    tm = z.shape[0]
    g = row0 + lax.broadcasted_iota(jnp.int32, (tm, 1), 0)
    is_ctx = g >= t_lat
    pos = jnp.where(is_ctx, g - t_lat, g & (GRID_W - 1))
    last = jnp.where(is_ctx, ctx_len - 1, GRID_W - 1)
    prev = jnp.where(pos == 0, 0.0, pltpu.roll(z, 1, 0))
    nxt = jnp.where(pos == last, 0.0, pltpu.roll(z, tm - 1, 0))
    return prev, nxt


def _dot_nt(a, b):
    return lax.dot_general(a, b, (((1,), (1,)), ((), ())), preferred_element_type=F32)


def _rows_spec(rows, width, row_of):
    return pl.BlockSpec((pl.Element(rows), pl.Element(width)),
                        lambda i, j: (pl.multiple_of(row_of(j), SUBLANES), 0))


def _inproj_shift_body(h_ref, wt_ref, mu_ref, o_ref, *, t_lat, ctx_len, n_sub):
    w = wt_ref[...].astype(BF16)
    tm = h_ref.shape[0]
    ts = tm // n_sub
    for s in range(n_sub):
        rows = slice(s * ts, (s + 1) * ts)
        z = _dot_nt(h_ref[rows, :], w)
        prev, nxt = _neighbours(z, pl.program_id(0) * tm + s * ts, t_lat, ctx_len)
        o_ref[rows, :] = z + mu_ref[...] * (0.5 * (prev + nxt) - z)


def _inproj_shift_call(h_all, w_in_t, mu, segments, *, t_lat, ctx_len, tm, tn, n_sub):
    t, d = h_all.shape
    assert tm % (n_sub * GRID_W) == 0
    assert all(r % SUBLANES == 0 and n % tn == 0 and r + n <= w_in_t.shape[0] for r, n in segments)
    tile0 = [sum(n for _, n in segments[:s]) // tn for s in range(len(segments) + 1)]
    n_tiles = tile0[-1]

    def row_of(j):
        return sum(jnp.where((j >= lo) & (j < hi), r + tn * (j - lo), 0)
                   for (r, _), lo, hi in zip(segments, tile0[:-1], tile0[1:]))

    return pl.pallas_call(
        functools.partial(_inproj_shift_body, t_lat=t_lat, ctx_len=ctx_len, n_sub=n_sub),
        grid=(t // tm, n_tiles),
        in_specs=[pl.BlockSpec((tm, d), lambda i, j: (i, 0)),
                  _rows_spec(tn, d, row_of),
                  pl.BlockSpec((1, tn), lambda i, j: (0, j))],
        out_specs=pl.BlockSpec((tm, tn), lambda i, j: (i, j)),
        out_shape=jax.ShapeDtypeStruct((t, n_tiles * tn), F32),
        compiler_params=_cparams("arbitrary", "arbitrary"), name="inproj_shift",
    )(h_all, w_in_t, mu)


def _inproj_conv_body(h_ref, wb_ref, wc_ref, wx_ref, cw_ref, o_ref, *, t_lat):
    h = h_ref[...]
    u = _dot_nt(h, wc_ref[...].astype(BF16)) * _dot_nt(h, wx_ref[...].astype(BF16))
    prev, nxt = _neighbours(u, pl.program_id(0) * u.shape[0], t_lat, 1)
    cw = cw_ref[...]
    conv = cw[0:1, :] * prev + cw[1:2, :] * u + cw[2:3, :] * nxt
    o_ref[...] = (_dot_nt(h, wb_ref[...].astype(BF16)) * conv).astype(o_ref.dtype)


def _inproj_conv_call(h_all, w_in_t, conv_w, *, row0, t_lat, tm, tc):
    d = h_all.shape[1]
    d_conv = conv_w.shape[1]
    assert d_conv % tc == 0 and row0 % SUBLANES == 0 and row0 + 3 * d_conv <= w_in_t.shape[0]
    cw = jnp.pad(conv_w, ((0, MOD_ROWS - 3), (0, 0)))
    return pl.pallas_call(
        functools.partial(_inproj_conv_body, t_lat=t_lat),
        grid=(t_lat // tm, d_conv // tc),
        in_specs=[pl.BlockSpec((tm, d), lambda i, j: (i, 0)),
                  _rows_spec(tc, d, lambda j: row0 + tc * j),
                  _rows_spec(tc, d, lambda j: row0 + d_conv + tc * j),
                  _rows_spec(tc, d, lambda j: row0 + 2 * d_conv + tc * j),
                  pl.BlockSpec((MOD_ROWS, tc), lambda i, j: (0, j))],
        out_specs=pl.BlockSpec((tm, tc), lambda i, j: (i, j)),
        out_shape=jax.ShapeDtypeStruct((t_lat, d_conv), BF16),
        compiler_params=_cparams("arbitrary", "arbitrary"), name="inproj_conv",
    )(h_all, w_in_t, w_in_t, w_in_t, cw)


def _readout_body(y0_ref, y1_ref, k_ref, v_ref, r_ref, la_ref, gl_ref, ka_ref, rk_ref, a0_ref,
                  wa0_ref, wa1_ref, wg_ref, lw_ref, lb_ref, o_ref):
    la = _lora_tiles(la_ref[...])[1].astype(BF16)
    a0 = a0_ref[...]
    a_sum = _sigmoid(a0[0:1, :] + _dot(la, wa0_ref[...])) + _sigmoid(a0[1:2, :] + _dot(la, wa1_ref[...]))
    coef = r_ref[...] * rk_ref[...] * k_ref[...] * (2.0 + (a_sum - 2.0) * ka_ref[...])
    y = y0_ref[...] + y1_ref[...]
    g = _dot(_sigmoid(gl_ref[...]).astype(BF16), wg_ref[...])

    yc = y - _head_sums(y) * (1.0 / HEAD)
    var = _head_sums(yc * yc) * (1.0 / HEAD)
    yn = yc * lax.rsqrt(var + LNX_EPS) * lw_ref[...] + lb_ref[...]
    bonus = _head_sums(coef) * v_ref[...]
    o_ref[...] = ((yn + bonus) * g).astype(o_ref.dtype)


def _readout_call(y0, y1, za, k_a, r_k, a0, wa0, wa1, w_gate, lnx_w, lnx_b, *, t_lat, d_rwkv, tm):
    row = lambda jb: pl.BlockSpec((tm, d_rwkv), lambda i: (i, jb))
    vec = pl.BlockSpec((1, d_rwkv), lambda i: (0, 0))
    mat = pl.BlockSpec((LORA_TILE, d_rwkv), lambda i: (0, 0))
    lora_blk = 3 * d_rwkv // LORA_PAD
    assert GATE_LORA == LORA_PAD
    return pl.pallas_call(
        _readout_body, grid=(t_lat // tm,),
        in_specs=[row(0), row(0), row(0), row(1), row(2),
                  pl.BlockSpec((tm, LORA_PAD), lambda i: (i, lora_blk)),
                  pl.BlockSpec((tm, GATE_LORA), lambda i: (i, lora_blk + 1)),
                  vec, vec, pl.BlockSpec((2, d_rwkv), lambda i: (0, 0)),
                  mat, mat, pl.BlockSpec((GATE_LORA, d_rwkv), lambda i: (0, 0)), vec, vec],
        out_specs=pl.BlockSpec((tm, d_rwkv), lambda i: (i, 0)),
        out_shape=jax.ShapeDtypeStruct((t_lat, d_rwkv), BF16),
        compiler_params=_cparams("arbitrary"), name="rwkv_readout",
    )(y0, y1, za, za, za, za, za, k_a, r_k, a0, wa0, wa1, w_gate, lnx_w, lnx_b)


def _outproj_body(ya_ref, yb_ref, wa_ref, wb_ref, x_ref, g_ref, n2_ref, sh_ref, sc_ref, o_ref, h_ref, rows_ref):
    j = pl.program_id(1)
    n_tiles, _, tn = rows_ref.shape
    acc = _dot(ya_ref[...], wa_ref[...]) + _dot(yb_ref[...], wb_ref[...])
    x1 = x_ref[...] + g_ref[0:1, :] * acc
    o_ref[...] = x1
    rows_ref[j] = x1

    @pl.when(j == n_tiles - 1)
    def _():
        ss = sum(jnp.sum(rows_ref[s] * rows_ref[s], axis=-1, keepdims=True) for s in range(n_tiles))
        rs = lax.rsqrt(ss * (1.0 / (n_tiles * tn)) + NORM_EPS)
        for s in range(n_tiles):
            sl = slice(s * tn, (s + 1) * tn)
            y = rows_ref[s] * rs * n2_ref[:, sl]
            h_ref[:, sl] = (y * (1.0 + sc_ref[0:1, sl]) + sh_ref[0:1, sl]).astype(h_ref.dtype)


def _outproj_call(ya, yb, w, x, mod, gate_blk0, gain2, shift_blk, scale_blk, *, tm, tn):
    t, da = ya.shape
    db = yb.shape[1]
    n = w.shape[1]
    assert da == db
    return pl.pallas_call(
        _outproj_body, grid=(t // tm, n // tn),
        in_specs=[pl.BlockSpec((tm, da), lambda i, j: (i, 0)),
                  pl.BlockSpec((tm, db), lambda i, j: (i, 0)),
                  pl.BlockSpec((da, tn), lambda i, j: (0, j)),
                  pl.BlockSpec((db, tn), lambda i, j: (1, j)),
                  pl.BlockSpec((tm, tn), lambda i, j: (i, j)),
                  pl.BlockSpec((MOD_ROWS, tn), lambda i, j: (0, gate_blk0 + j)),
                  pl.BlockSpec((1, n), lambda i, j: (0, 0)),
                  pl.BlockSpec((MOD_ROWS, n), lambda i, j: (0, shift_blk)),
                  pl.BlockSpec((MOD_ROWS, n), lambda i, j: (0, scale_blk))],
        out_specs=[pl.BlockSpec((tm, tn), lambda i, j: (i, j)),
                   pl.BlockSpec((tm, n), lambda i, j: (i, 0))],
        out_shape=[jax.ShapeDtypeStruct((t, n), F32), jax.ShapeDtypeStruct((t, n), BF16)],
        scratch_shapes=[pltpu.VMEM((n // tn, tm, tn), F32)],
        compiler_params=_cparams("arbitrary", "arbitrary"), name="outproj_residual_norm",
    )(ya, yb, w, w, x, mod, gain2, mod, mod)


def _ffn_up_body(h_ref, wg_ref, wu_ref, o_ref):
    h = h_ref[...]
    g = _dot(h, wg_ref[...].astype(BF16))
    u = _dot(h, wu_ref[...].astype(BF16))
    o_ref[...] = (g * _sigmoid(g) * u).astype(o_ref.dtype)


def _ffn_up_call(h, wg, wu, *, tm, tn):
    t, d = h.shape
    n = wg.shape[1]
    return pl.pallas_call(
        _ffn_up_body, grid=(t // tm, n // tn),
        in_specs=[pl.BlockSpec((tm, d), lambda i, j: (i, 0)),
                  pl.BlockSpec((d, tn), lambda i, j: (0, j)),
                  pl.BlockSpec((d, tn), lambda i, j: (0, j))],
        out_specs=pl.BlockSpec((tm, tn), lambda i, j: (i, j)),
        out_shape=jax.ShapeDtypeStruct((t, n), BF16),
        compiler_params=_cparams("arbitrary", "arbitrary"), name="ffn_gate_up",
    )(h, wg, wu)


def _ffn_down_body(a_ref, w_ref, x_ref, g_ref, o_ref):
    o_ref[...] = x_ref[...] + g_ref[0:1, :] * _dot(a_ref[...], w_ref[...])


def _ffn_down_call(act, w, x, mod, gate_blk0, *, tm, tn):
    t, kdim = act.shape
    d = w.shape[1]
    return pl.pallas_call(
        _ffn_down_body, grid=(t // tm, d // tn),
        in_specs=[pl.BlockSpec((tm, kdim), lambda i, j: (i, 0)),
                  pl.BlockSpec((kdim, tn), lambda i, j: (0, j)),
                  pl.BlockSpec((tm, tn), lambda i, j: (i, j)),
                  pl.BlockSpec((MOD_ROWS, tn), lambda i, j: (0, gate_blk0 + j))],
        out_specs=pl.BlockSpec((tm, tn), lambda i, j: (i, j)),
        out_shape=jax.ShapeDtypeStruct((t, d), F32),
        compiler_params=_cparams("arbitrary", "arbitrary"), name="ffn_down_residual",
    )(act, w, x, mod)


def _final_norm_body(x_ref, g_ref, o_ref):
    x = x_ref[...]
    o_ref[...] = x * lax.rsqrt(jnp.mean(x * x, axis=-1, keepdims=True) + NORM_EPS) * g_ref[...]


def _final_norm_call(x, gain, *, tm):
    t, d = x.shape
    return pl.pallas_call(
        _final_norm_body, grid=(t // tm,),
        in_specs=[pl.BlockSpec((tm, d), lambda i: (i, 0)), pl.BlockSpec((1, d), lambda i: (0, 0))],
        out_specs=pl.BlockSpec((tm, d), lambda i: (i, 0)),
        out_shape=jax.ShapeDtypeStruct((t, d), F32),
        compiler_params=_cparams("arbitrary"), name="final_norm",
    )(x, gain)


def _tiles():
    return dict(mod_tn=1024, norm_tm=256, shift_tm=1408, shift_tn=256, shift_sub=2, conv_tm=1024, conv_tc=256,
                readout_tm=256, out_tm=512, out_tn=512, up_tm=1024, up_tn=256, down_tm=512, down_tn=512)


def kernel(x, c, ctx, c_ctx, w_ada, b_ada, norm1, w_in, mu_shift, k_k, k_a, r_k, w0, w_decay_up, a0, w_iclr_up, w_gate_up, lnx_w, lnx_b, conv_w, w_out, norm2, w_ffn_gate, w_ffn_up, w_ffn_down, norm_f):
    assert x.shape[0] == 1 and w_ada.shape[0] == 1, "single batch element, single layer"
    tl = _tiles()
    t_lat, d = x.shape[1], x.shape[2]
    ctx_len = ctx.shape[1]
    d_rwkv = k_k.shape[1]
    d_conv = conv_w.shape[2]
    d_ff = w_ffn_gate.shape[2]
    assert t_lat % GRID_W == 0 and ctx_len % CHUNK == 0 and d_rwkv % SEG_W == 0
    x2d, ctx2d = x[0], ctx[0]

    off_lora = 2 * d_rwkv
    off_r = off_lora + DECAY_LORA + ICLR_LORA
    off_gl = off_r + d_rwkv
    rwkv_cols = off_gl + GATE_LORA
    wi_t = jnp.swapaxes(w_in, 1, 2)[0]
    segments = [(0, off_lora), (off_r, d_rwkv), (off_lora, LORA_PAD), (off_gl, GATE_LORA)]
    ms = mu_shift[0]
    mu = jnp.concatenate([ms[r0:r0 + n] for r0, n in segments])[None, :]
    wd = [jnp.zeros((LORA_TILE, d_rwkv), BF16).at[:DECAY_LORA].set(w_decay_up[0, i].astype(BF16)) for i in range(2)]
    wa = [jnp.zeros((LORA_TILE, d_rwkv), BF16).at[:ICLR_LORA].set(w_iclr_up[0, i].astype(BF16)) for i in range(2)]
    wdn = w_ffn_down[0].astype(BF16)
    wo = w_out[0].astype(BF16)

    cc = jnp.zeros((MOD_ROWS, d), F32).at[0].set(c[0]).at[1].set(c_ctx)
    mod = _mod_call(cc, w_ada[0], b_ada, tn=tl["mod_tn"])

    h_all = _norm_call(x2d, ctx2d, norm1, mod, 0, 1, tm=tl["norm_tm"])
    za = _inproj_shift_call(h_all, wi_t, mu, segments, t_lat=t_lat, ctx_len=ctx_len,
                            tm=tl["shift_tm"], tn=tl["shift_tn"], n_sub=tl["shift_sub"])
    y_conv = _inproj_conv_call(h_all, wi_t, conv_w[0], row0=rwkv_cols, t_lat=t_lat,
                               tm=tl["conv_tm"], tc=tl["conv_tc"])
    n_lat_chunks, n_ctx_chunks = t_lat // CHUNK, ctx_len // CHUNK
    ys = [_scan_call(za, k_k, k_a, w0[0, i:i + 1], a0[0, i:i + 1], wd[i], wa[i], rev=(i == 1),
                     n_lat_chunks=n_lat_chunks, n_ctx_chunks=n_ctx_chunks, d_rwkv=d_rwkv) for i in range(2)]
    y_rwkv = _readout_call(ys[0], ys[1], za, k_a, r_k.reshape(1, d_rwkv), a0[0], wa[0], wa[1],
                           w_gate_up[0].astype(BF16), lnx_w, lnx_b, t_lat=t_lat, d_rwkv=d_rwkv,
                           tm=tl["readout_tm"])
    x1, h2 = _outproj_call(y_rwkv, y_conv, wo, x2d, mod, 2 * d // tl["out_tn"], norm2, 3, 4,
                           tm=tl["out_tm"], tn=tl["out_tn"])

    act = _ffn_up_call(h2, w_ffn_gate[0], w_ffn_up[0], tm=tl["up_tm"], tn=tl["up_tn"])
    x2 = _ffn_down_call(act, wdn, x1, mod, 5 * d // tl["down_tn"], tm=tl["down_tm"], tn=tl["down_tn"])
    out = _final_norm_call(x2, norm_f[None, :], tm=tl["norm_tm"])
    return out[None]
```

```python
import functools
import math

import jax
import jax.numpy as jnp
from jax import lax
from jax.experimental import pallas as pl
from jax.experimental.pallas import tpu as pltpu

F32 = jnp.float32
BF16 = jnp.bfloat16

SUBLANES = 8
MXU_WIDTH = 256
SEG_W = MXU_WIDTH
VMEM_LIMIT_BYTES = 56 * 1024 * 1024

HEAD = 64
GROUP = 128
CHUNK = 64
GRID_W = 64
DECAY_LORA = 96
ICLR_LORA = 96
GATE_LORA = 256
LORA_TILE = 128
LORA_PAD = 2 * LORA_TILE
NORM_EPS = 1e-6
LNX_EPS = 64e-5
EXP_M05 = math.exp(-0.5)


def _cparams(*sem):
    return pltpu.CompilerParams(dimension_semantics=sem, vmem_limit_bytes=VMEM_LIMIT_BYTES)


def _dot(a, b):
    return jnp.dot(a, b, preferred_element_type=F32)


def _bmm(a, b):
    return lax.dot_general(a, b, (((2,), (1,)), ((0,), (0,))), preferred_element_type=F32)


def _bmm_nt(a, b):
    return lax.dot_general(a, b, (((2,), (2,)), ((0,), (0,))), preferred_element_type=F32)


def _bmm_tn(a, b):
    return lax.dot_general(a, b, (((1,), (1,)), ((0,), (0,))), preferred_element_type=F32)


def _split2(x):
    hi = x.astype(BF16)
    lo = (x - hi.astype(F32)).astype(BF16)
    return hi, lo


def _sigmoid(x):
    return 1.0 / (1.0 + jnp.exp(-x))


def _head_sums(x):
    rows, d = x.shape
    n_seg = d // SEG_W
    xs = jnp.concatenate([x[:, s * SEG_W:(s + 1) * SEG_W] for s in range(n_seg)], axis=0)
    rr = lax.broadcasted_iota(jnp.int32, (SEG_W, SEG_W), 0) // HEAD
    cc = lax.broadcasted_iota(jnp.int32, (SEG_W, SEG_W), 1) // HEAD
    ones_seg = (rr == cc).astype(BF16)
    hi, lo = _split2(xs)
    ss = _dot(hi, ones_seg) + _dot(lo, ones_seg)
    return jnp.concatenate([ss[s * rows:(s + 1) * rows] for s in range(n_seg)], axis=1)


def _lora_tiles(la):
    return la[:, :LORA_TILE], la[:, DECAY_LORA:DECAY_LORA + LORA_TILE]


def _scan_chunk(lw, l_cum, a, kd, kk, v, r, ht0, *, rev, n_groups):
    def st(x):
        return jnp.stack([x[:, p * GROUP:(p + 1) * GROUP] for p in range(n_groups)], axis=0)

    lw_s, l_s, a_s, kd_s, kk_s, v_s, r_s = (st(t) for t in (lw, l_cum, a, kd, kk, v, r))
    ltot = l_s[:, 0:1, :] if rev else l_s[:, CHUNK - 1:CHUNK, :]
    e_l = jnp.exp(l_s)
    e_nl = jnp.exp(-l_s)
    e_lx = jnp.exp(l_s - lw_s)
    e_tl = jnp.exp(ltot - l_s)
    g_tot = jnp.exp(ltot)

    t_i = lax.broadcasted_iota(jnp.int32, (1, CHUNK, GROUP), 1)
    lane = lax.broadcasted_iota(jnp.int32, (1, CHUNK, GROUP), 2)
    s_i = lane & (HEAD - 1)
    head_of_lane = lane // HEAD
    strict = (s_i > t_i) if rev else (s_i < t_i)
    incl = (s_i >= t_i) if rev else (s_i <= t_i)
    eye = (s_i == t_i).astype(F32)
    rr = lax.broadcasted_iota(jnp.int32, (GROUP, GROUP), 0)
    cc = lax.broadcasted_iota(jnp.int32, (GROUP, GROUP), 1)
    same_head = (rr // HEAD) == (cc // HEAD)

    def bd(x):
        z = jnp.zeros_like(x)
        return jnp.concatenate([jnp.where(head_of_lane == g, x, z) for g in range(GROUP // HEAD)], axis=1)

    b_s = kk_s * a_s
    ah = (-kk_s * e_lx).astype(BF16)
    rh = (r_s * e_l).astype(BF16)
    bc = (b_s * e_nl).astype(BF16)
    kc = (kd_s * e_nl).astype(BF16)
    kt = (kd_s * e_tl).astype(BF16)
    bt = (b_s * e_tl).astype(BF16)
    v_b = v_s.astype(BF16)

    sc = _bmm_nt(jnp.concatenate([ah, rh], axis=1),
                 jnp.concatenate([bd(bc), bd(kc)], axis=1))
    m_ab = jnp.where(strict, sc[:, :CHUNK, :GROUP], 0.0)
    m_ak = jnp.where(strict, sc[:, :CHUNK, GROUP:], 0.0).astype(BF16)
    n_rb = jnp.where(incl, sc[:, CHUNK:, :GROUP], 0.0).astype(BF16)
    n_rk = jnp.where(incl, sc[:, CHUNK:, GROUP:], 0.0).astype(BF16)

    s_acc = eye + m_ab
    m_pow = m_ab.astype(BF16)
    m_pow = _bmm(m_pow, bd(m_pow))
    n_steps = int(math.log2(CHUNK)) - 1
    for j in range(n_steps):
        mb = m_pow.astype(BF16)
        if j < n_steps - 1:
            both = _bmm(mb, jnp.concatenate([bd(mb), bd(s_acc.astype(BF16))], axis=2))
            m_pow = both[:, :, :GROUP]
            s_acc = s_acc + both[:, :, GROUP:]
        else:
            s_acc = s_acc + _bmm(mb, bd(s_acc.astype(BF16)))
    t_inv = s_acc.astype(BF16)

    ht0_b = ht0.astype(BF16)
    v_bd = bd(v_b)
    from_h0 = _bmm_nt(jnp.concatenate([ah, rh], axis=1), ht0_b)
    w = from_h0[:, :CHUNK] + _bmm(m_ak, v_bd)
    u_b = _bmm(t_inv, bd(w.astype(BF16))).astype(BF16)
    y = from_h0[:, CHUNK:] + _bmm(jnp.concatenate([n_rk, n_rb], axis=2),
                                  jnp.concatenate([v_bd, bd(u_b)], axis=1))
    upd = _bmm_tn(jnp.concatenate([v_b, u_b], axis=1), jnp.concatenate([kt, bt], axis=1))
    ht1 = ht0 * g_tot + jnp.where(same_head[None], upd, 0.0)
    return jnp.concatenate([y[p] for p in range(n_groups)], axis=1), ht1


def _scan_body(k_ref, v_ref, r_ref, la_ref, kk_ref, ka_ref, w0_ref, a0_ref, wd_ref, wa_ref,
               y_ref, h_ref, *, rev, n_groups):
    c = pl.program_id(0)

    @pl.when(c == 0)
    def _():
        h_ref[...] = jnp.zeros_like(h_ref)

    k = k_ref[...]
    v = v_ref[...]
    r = r_ref[...]
    la_dec, la_icl = _lora_tiles(la_ref[...])
    dec_pre = w0_ref[...] + _dot(jnp.tanh(la_dec).astype(BF16), wd_ref[...])
    icl_pre = a0_ref[...] + _dot(la_icl.astype(BF16), wa_ref[...])
    lw = -EXP_M05 * _sigmoid(dec_pre)
    a = _sigmoid(icl_pre)
    kd = k * (1.0 + (a - 1.0) * ka_ref[...])
    kkr = k * kk_ref[...]
    kk = kkr / jnp.maximum(jnp.sqrt(_head_sums(kkr * kkr)), 1e-12)

    rows = k.shape[0]
    row = lax.broadcasted_iota(jnp.int32, (rows, rows), 0)
    col = lax.broadcasted_iota(jnp.int32, (rows, rows), 1)
    in_order = (col >= row) if rev else (col <= row)
    tri = (in_order & ((row // CHUNK) == (col // CHUNK))).astype(BF16)
    lw_hi, lw_lo = _split2(lw)
    l_cum = _dot(tri, lw_hi) + _dot(tri, lw_lo)

    n_sub = rows // CHUNK
    ht = h_ref[...]
    ys = [None] * n_sub
    for s in (reversed(range(n_sub)) if rev else range(n_sub)):
        sl = slice(s * CHUNK, (s + 1) * CHUNK)
        ys[s], ht = _scan_chunk(lw[sl], l_cum[sl], a[sl], kd[sl], kk[sl], v[sl], r[sl], ht,
                                rev=rev, n_groups=n_groups)
    y_ref[...] = jnp.concatenate(ys, axis=0)
    h_ref[...] = ht


def _scan_call(za, kk, ka, w0, a0, wd, wa, *, rev, n_lat_chunks, n_ctx_chunks, d_rwkv, n_sub):
    assert n_lat_chunks % n_sub == 0 and n_ctx_chunks % n_sub == 0
    n_lat_blk, n_ctx_blk = n_lat_chunks // n_sub, n_ctx_chunks // n_sub
    n_blk = n_lat_blk + n_ctx_blk
    rows = n_sub * CHUNK
    n_groups = d_rwkv // GROUP
    lora_blk = 3 * d_rwkv // LORA_PAD
    assert CHUNK == HEAD and d_rwkv % GROUP == 0

    def blk_of(c):
        if rev:
            return n_blk - 1 - c
        return jnp.where(c < n_ctx_blk, n_lat_blk + c, c - n_ctx_blk)

    vec = pl.BlockSpec((1, d_rwkv), lambda c: (0, 0))
    mat = pl.BlockSpec((LORA_TILE, d_rwkv), lambda c: (0, 0))
    return pl.pallas_call(
        functools.partial(_scan_body, rev=rev, n_groups=n_groups),
        grid=(n_blk,),
        in_specs=[
            pl.BlockSpec((rows, d_rwkv), lambda c: (blk_of(c), 0)),
            pl.BlockSpec((rows, d_rwkv), lambda c: (blk_of(c), 1)),
            pl.BlockSpec((rows, d_rwkv), lambda c: (blk_of(c), 2)),
            pl.BlockSpec((rows, LORA_PAD), lambda c: (blk_of(c), lora_blk)),
            vec, vec, vec, vec, mat, mat,
        ],
        out_specs=pl.BlockSpec((rows, d_rwkv), lambda c: (blk_of(c), 0)),
        out_shape=jax.ShapeDtypeStruct((n_blk * rows, d_rwkv), F32),
        scratch_shapes=[pltpu.VMEM((n_groups, GROUP, GROUP), F32)],
        compiler_params=_cparams("arbitrary"),
        name="wkv_scan_rev" if rev else "wkv_scan_fwd",
    )(za, za, za, za, kk, ka, w0, a0, wd, wa)


MOD_ROWS = 8


def _mod_body(cc_ref, w_ref, b_ref, o_ref):
    cc = cc_ref[...]
    s = cc * _sigmoid(cc)
    s_hi, s_lo = _split2(s)
    w = w_ref[...].astype(BF16)
    o_ref[...] = _dot(s_hi, w) + _dot(s_lo, w) + b_ref[...]


def _mod_call(cc, w_ada, b_ada, *, tn):
    d, n = w_ada.shape
    return pl.pallas_call(
        _mod_body,
        grid=(n // tn,),
        in_specs=[pl.BlockSpec((MOD_ROWS, d), lambda j: (0, 0)),
                  pl.BlockSpec((d, tn), lambda j: (0, j)),
                  pl.BlockSpec((1, tn), lambda j: (0, j))],
        out_specs=pl.BlockSpec((MOD_ROWS, tn), lambda j: (0, j)),
        out_shape=jax.ShapeDtypeStruct((MOD_ROWS, n), F32),
        compiler_params=_cparams("arbitrary"),
        name="adaln_mod",
    )(cc, w_ada, b_ada)


def _norm_mod(xf, gain, shift, scale):
    y = xf * lax.rsqrt(jnp.mean(xf * xf, axis=-1, keepdims=True) + NORM_EPS) * gain
    return y * (1.0 + scale) + shift


def _norm_ctx_body(x_ref, ctx_ref, g_ref, sh_ref, sc_ref, o_ref, *, n_lat_blocks):
    i = pl.program_id(0)

    @pl.when(i < n_lat_blocks)
    def _():
        o_ref[...] = _norm_mod(x_ref[...], g_ref[...], sh_ref[0:1, :], sc_ref[0:1, :]).astype(o_ref.dtype)

    @pl.when(i >= n_lat_blocks)
    def _():
        o_ref[...] = _norm_mod(ctx_ref[...], g_ref[...], sh_ref[1:2, :], sc_ref[1:2, :]).astype(o_ref.dtype)


def _norm_call(x, ctx, gain, mod, shift_blk, scale_blk, *, tm):
    t, d = x.shape
    n_lat = t // tm
    vec = pl.BlockSpec((1, d), lambda i: (0, 0))
    sh = pl.BlockSpec((MOD_ROWS, d), lambda i: (0, shift_blk))
    sc = pl.BlockSpec((MOD_ROWS, d), lambda i: (0, scale_blk))
    tc = ctx.shape[0]
    n_ctx = tc // tm
    return pl.pallas_call(
        functools.partial(_norm_ctx_body, n_lat_blocks=n_lat), grid=(n_lat + n_ctx,),
        in_specs=[pl.BlockSpec((tm, d), lambda i: (jnp.minimum(i, n_lat - 1), 0)),
                  pl.BlockSpec((tm, d), lambda i: (jnp.maximum(i - n_lat, 0), 0)), vec, sh, sc],
        out_specs=pl.BlockSpec((tm, d), lambda i: (i, 0)),
        out_shape=jax.ShapeDtypeStruct((t + tc, d), BF16),
        compiler_params=_cparams("arbitrary"), name="norm_mod_ctx",
    )(x, ctx, gain, mod, mod)


def _neighbours(z, row0, t_lat, ctx_len):
    tm = z.shape[0]
    g = row0 + lax.broadcasted_iota(jnp.int32, (tm, 1), 0)
    is_ctx = g >= t_lat
    pos = jnp.where(is_ctx, g - t_lat, g & (GRID_W - 1))
    last = jnp.where(is_ctx, ctx_len - 1, GRID_W - 1)
    prev = jnp.where(pos == 0, 0.0, pltpu.roll(z, 1, 0))
    nxt = jnp.where(pos == last, 0.0, pltpu.roll(z, tm - 1, 0))
    return prev, nxt


def _dot_nt(a, b):
    return lax.dot_general(a, b, (((1,), (1,)), ((), ())), preferred_element_type=F32)


def _rows_spec(rows, width, row_of):
    return pl.BlockSpec((pl.Element(rows), pl.Element(width)),
                        lambda i, j: (pl.multiple_of(row_of(j), SUBLANES), 0))


def _inproj_shift_body(h_ref, wt_ref, mu_ref, o_ref, *, t_lat, ctx_len, n_sub):
    w = wt_ref[...].astype(BF16)
    tm = h_ref.shape[0]
    ts = tm // n_sub
    for s in range(n_sub):
        rows = slice(s * ts, (s + 1) * ts)
        z = _dot_nt(h_ref[rows, :], w)
        prev, nxt = _neighbours(z, pl.program_id(0) * tm + s * ts, t_lat, ctx_len)
        o_ref[rows, :] = z + mu_ref[...] * (0.5 * (prev + nxt) - z)


def _inproj_shift_call(h_all, w_in_t, mu, segments, *, t_lat, ctx_len, tm, tn, n_sub):
    t, d = h_all.shape
    assert tm % (n_sub * GRID_W) == 0
    assert all(r % SUBLANES == 0 and n % tn == 0 and r + n <= w_in_t.shape[0] for r, n in segments)
    tile0 = [sum(n for _, n in segments[:s]) // tn for s in range(len(segments) + 1)]
    n_tiles = tile0[-1]

    def row_of(j):
        return sum(jnp.where((j >= lo) & (j < hi), r + tn * (j - lo), 0)
                   for (r, _), lo, hi in zip(segments, tile0[:-1], tile0[1:]))

    return pl.pallas_call(
        functools.partial(_inproj_shift_body, t_lat=t_lat, ctx_len=ctx_len, n_sub=n_sub),
        grid=(t // tm, n_tiles),
        in_specs=[pl.BlockSpec((tm, d), lambda i, j: (i, 0)),
                  _rows_spec(tn, d, row_of),
                  pl.BlockSpec((1, tn), lambda i, j: (0, j))],
        out_specs=pl.BlockSpec((tm, tn), lambda i, j: (i, j)),
        out_shape=jax.ShapeDtypeStruct((t, n_tiles * tn), F32),
        compiler_params=_cparams("arbitrary", "arbitrary"), name="inproj_shift",
    )(h_all, w_in_t, mu)


def _inproj_conv_body(h_ref, wb_ref, wc_ref, wx_ref, cw_ref, o_ref, *, t_lat):
    h = h_ref[...]
    u = _dot_nt(h, wc_ref[...].astype(BF16)) * _dot_nt(h, wx_ref[...].astype(BF16))
    prev, nxt = _neighbours(u, pl.program_id(0) * u.shape[0], t_lat, 1)
    cw = cw_ref[...]
    conv = cw[0:1, :] * prev + cw[1:2, :] * u + cw[2:3, :] * nxt
    o_ref[...] = (_dot_nt(h, wb_ref[...].astype(BF16)) * conv).astype(o_ref.dtype)


def _inproj_conv_call(h_all, w_in_t, conv_w, *, row0, t_lat, tm, tc):
    d = h_all.shape[1]
    d_conv = conv_w.shape[1]
    assert d_conv % tc == 0 and row0 % SUBLANES == 0 and row0 + 3 * d_conv <= w_in_t.shape[0]
    cw = jnp.pad(conv_w, ((0, MOD_ROWS - 3), (0, 0)))
    return pl.pallas_call(
        functools.partial(_inproj_conv_body, t_lat=t_lat),
        grid=(t_lat // tm, d_conv // tc),
        in_specs=[pl.BlockSpec((tm, d), lambda i, j: (i, 0)),
                  _rows_spec(tc, d, lambda j: row0 + tc * j),
                  _rows_spec(tc, d, lambda j: row0 + d_conv + tc * j),
                  _rows_spec(tc, d, lambda j: row0 + 2 * d_conv + tc * j),
                  pl.BlockSpec((MOD_ROWS, tc), lambda i, j: (0, j))],
        out_specs=pl.BlockSpec((tm, tc), lambda i, j: (i, j)),
        out_shape=jax.ShapeDtypeStruct((t_lat, d_conv), BF16),
        compiler_params=_cparams("arbitrary", "arbitrary"), name="inproj_conv",
    )(h_all, w_in_t, w_in_t, w_in_t, cw)


def _readout_body(y0_ref, y1_ref, k_ref, v_ref, r_ref, la_ref, gl_ref, ka_ref, rk_ref, a0_ref,
                  wa0_ref, wa1_ref, wg_ref, lw_ref, lb_ref, o_ref):
    la = _lora_tiles(la_ref[...])[1].astype(BF16)
    a0 = a0_ref[...]
    a_sum = _sigmoid(a0[0:1, :] + _dot(la, wa0_ref[...])) + _sigmoid(a0[1:2, :] + _dot(la, wa1_ref[...]))
    coef = r_ref[...] * rk_ref[...] * k_ref[...] * (2.0 + (a_sum - 2.0) * ka_ref[...])
    y = y0_ref[...] + y1_ref[...]
    g = _dot(_sigmoid(gl_ref[...]).astype(BF16), wg_ref[...])

    yc = y - _head_sums(y) * (1.0 / HEAD)
    var = _head_sums(yc * yc) * (1.0 / HEAD)
    yn = yc * lax.rsqrt(var + LNX_EPS) * lw_ref[...] + lb_ref[...]
    bonus = _head_sums(coef) * v_ref[...]
    o_ref[...] = ((yn + bonus) * g).astype(o_ref.dtype)


def _readout_call(y0, y1, za, k_a, r_k, a0, wa0, wa1, w_gate, lnx_w, lnx_b, *, t_lat, d_rwkv, tm):
    row = lambda jb: pl.BlockSpec((tm, d_rwkv), lambda i: (i, jb))
    vec = pl.BlockSpec((1, d_rwkv), lambda i: (0, 0))
    mat = pl.BlockSpec((LORA_TILE, d_rwkv), lambda i: (0, 0))
    lora_blk = 3 * d_rwkv // LORA_PAD
    assert GATE_LORA == LORA_PAD
    return pl.pallas_call(
        _readout_body, grid=(t_lat // tm,),
        in_specs=[row(0), row(0), row(0), row(1), row(2),
                  pl.BlockSpec((tm, LORA_PAD), lambda i: (i, lora_blk)),
                  pl.BlockSpec((tm, GATE_LORA), lambda i: (i, lora_blk + 1)),
                  vec, vec, pl.BlockSpec((2, d_rwkv), lambda i: (0, 0)),
                  mat, mat, pl.BlockSpec((GATE_LORA, d_rwkv), lambda i: (0, 0)), vec, vec],
        out_specs=pl.BlockSpec((tm, d_rwkv), lambda i: (i, 0)),
        out_shape=jax.ShapeDtypeStruct((t_lat, d_rwkv), BF16),
        compiler_params=_cparams("arbitrary"), name="rwkv_readout",
    )(y0, y1, za, za, za, za, za, k_a, r_k, a0, wa0, wa1, w_gate, lnx_w, lnx_b)


def _outproj_body(ya_ref, yb_ref, wa_ref, wb_ref, x_ref, g_ref, n2_ref, sh_ref, sc_ref, o_ref, h_ref, rows_ref):
    j = pl.program_id(1)
    n_tiles, _, tn = rows_ref.shape
    acc = _dot(ya_ref[...], wa_ref[...]) + _dot(yb_ref[...], wb_ref[...])
    x1 = x_ref[...] + g_ref[0:1, :] * acc
    o_ref[...] = x1
    rows_ref[j] = x1

    @pl.when(j == n_tiles - 1)
    def _():
        ss = sum(jnp.sum(rows_ref[s] * rows_ref[s], axis=-1, keepdims=True) for s in range(n_tiles))
        rs = lax.rsqrt(ss * (1.0 / (n_tiles * tn)) + NORM_EPS)
        for s in range(n_tiles):
            sl = slice(s * tn, (s + 1) * tn)
            y = rows_ref[s] * rs * n2_ref[:, sl]
            h_ref[:, sl] = (y * (1.0 + sc_ref[0:1, sl]) + sh_ref[0:1, sl]).astype(h_ref.dtype)


def _outproj_call(ya, yb, w, x, mod, gate_blk0, gain2, shift_blk, scale_blk, *, tm, tn):
    t, da = ya.shape
    db = yb.shape[1]
    n = w.shape[1]
    assert da == db
    return pl.pallas_call(
        _outproj_body, grid=(t // tm, n // tn),
        in_specs=[pl.BlockSpec((tm, da), lambda i, j: (i, 0)),
                  pl.BlockSpec((tm, db), lambda i, j: (i, 0)),
                  pl.BlockSpec((da, tn), lambda i, j: (0, j)),
                  pl.BlockSpec((db, tn), lambda i, j: (1, j)),
                  pl.BlockSpec((tm, tn), lambda i, j: (i, j)),
                  pl.BlockSpec((MOD_ROWS, tn), lambda i, j: (0, gate_blk0 + j)),
                  pl.BlockSpec((1, n), lambda i, j: (0, 0)),
                  pl.BlockSpec((MOD_ROWS, n), lambda i, j: (0, shift_blk)),
                  pl.BlockSpec((MOD_ROWS, n), lambda i, j: (0, scale_blk))],
        out_specs=[pl.BlockSpec((tm, tn), lambda i, j: (i, j)),
                   pl.BlockSpec((tm, n), lambda i, j: (i, 0))],
        out_shape=[jax.ShapeDtypeStruct((t, n), F32), jax.ShapeDtypeStruct((t, n), BF16)],
        scratch_shapes=[pltpu.VMEM((n // tn, tm, tn), F32)],
        compiler_params=_cparams("arbitrary", "arbitrary"), name="outproj_residual_norm",
    )(ya, yb, w, w, x, mod, gain2, mod, mod)


def _ffn_up_body(h_ref, wg_ref, wu_ref, o_ref):
    h = h_ref[...]
    g = _dot(h, wg_ref[...].astype(BF16))
    u = _dot(h, wu_ref[...].astype(BF16))
    o_ref[...] = (g * _sigmoid(g) * u).astype(o_ref.dtype)


def _ffn_up_call(h, wg, wu, *, tm, tn):
    t, d = h.shape
    n = wg.shape[1]
    return pl.pallas_call(
        _ffn_up_body, grid=(t // tm, n // tn),
        in_specs=[pl.BlockSpec((tm, d), lambda i, j: (i, 0)),
                  pl.BlockSpec((d, tn), lambda i, j: (0, j)),
                  pl.BlockSpec((d, tn), lambda i, j: (0, j))],
        out_specs=pl.BlockSpec((tm, tn), lambda i, j: (i, j)),
        out_shape=jax.ShapeDtypeStruct((t, n), BF16),
        compiler_params=_cparams("arbitrary", "arbitrary"), name="ffn_gate_up",
    )(h, wg, wu)


def _ffn_down_body(a_ref, w_ref, x_ref, g_ref, o_ref):
    o_ref[...] = x_ref[...] + g_ref[0:1, :] * _dot(a_ref[...], w_ref[...])


def _ffn_down_call(act, w, x, mod, gate_blk0, *, tm, tn):
    t, kdim = act.shape
    d = w.shape[1]
    return pl.pallas_call(
        _ffn_down_body, grid=(t // tm, d // tn),
        in_specs=[pl.BlockSpec((tm, kdim), lambda i, j: (i, 0)),
                  pl.BlockSpec((kdim, tn), lambda i, j: (0, j)),
                  pl.BlockSpec((tm, tn), lambda i, j: (i, j)),
                  pl.BlockSpec((MOD_ROWS, tn), lambda i, j: (0, gate_blk0 + j))],
        out_specs=pl.BlockSpec((tm, tn), lambda i, j: (i, j)),
        out_shape=jax.ShapeDtypeStruct((t, d), F32),
        compiler_params=_cparams("arbitrary", "arbitrary"), name="ffn_down_residual",
    )(act, w, x, mod)


def _final_norm_body(x_ref, g_ref, o_ref):
    x = x_ref[...]
    o_ref[...] = x * lax.rsqrt(jnp.mean(x * x, axis=-1, keepdims=True) + NORM_EPS) * g_ref[...]


def _final_norm_call(x, gain, *, tm):
    t, d = x.shape
    return pl.pallas_call(
        _final_norm_body, grid=(t // tm,),
        in_specs=[pl.BlockSpec((tm, d), lambda i: (i, 0)), pl.BlockSpec((1, d), lambda i: (0, 0))],
        out_specs=pl.BlockSpec((tm, d), lambda i: (i, 0)),
        out_shape=jax.ShapeDtypeStruct((t, d), F32),
        compiler_params=_cparams("arbitrary"), name="final_norm",
    )(x, gain)


def _tiles():
    return dict(mod_tn=1024, norm_tm=256, shift_tm=1408, shift_tn=256, shift_sub=2, conv_tm=1024, conv_tc=256,
                scan_sub=4, readout_tm=256, out_tm=512, out_tn=1024, up_tm=1024, up_tn=256, down_tm=512, down_tn=512)


def kernel(x, c, ctx, c_ctx, w_ada, b_ada, norm1, w_in, mu_shift, k_k, k_a, r_k, w0, w_decay_up, a0, w_iclr_up, w_gate_up, lnx_w, lnx_b, conv_w, w_out, norm2, w_ffn_gate, w_ffn_up, w_ffn_down, norm_f):
    assert x.shape[0] == 1 and w_ada.shape[0] == 1, "single batch element, single layer"
    tl = _tiles()
    t_lat, d = x.shape[1], x.shape[2]
    ctx_len = ctx.shape[1]
    d_rwkv = k_k.shape[1]
    d_conv = conv_w.shape[2]
    d_ff = w_ffn_gate.shape[2]
    assert t_lat % GRID_W == 0 and ctx_len % CHUNK == 0 and d_rwkv % SEG_W == 0
    x2d, ctx2d = x[0], ctx[0]

    off_lora = 2 * d_rwkv
    off_r = off_lora + DECAY_LORA + ICLR_LORA
    off_gl = off_r + d_rwkv
    rwkv_cols = off_gl + GATE_LORA
    wi_t = jnp.swapaxes(w_in, 1, 2)[0]
    segments = [(0, off_lora), (off_r, d_rwkv), (off_lora, LORA_PAD), (off_gl, GATE_LORA)]
    ms = mu_shift[0]
    mu = jnp.concatenate([ms[r0:r0 + n] for r0, n in segments])[None, :]
    wd = [jnp.zeros((LORA_TILE, d_rwkv), BF16).at[:DECAY_LORA].set(w_decay_up[0, i].astype(BF16)) for i in range(2)]
    wa = [jnp.zeros((LORA_TILE, d_rwkv), BF16).at[:ICLR_LORA].set(w_iclr_up[0, i].astype(BF16)) for i in range(2)]
    wdn = w_ffn_down[0].astype(BF16)
    wo = w_out[0].astype(BF16)

    cc = jnp.zeros((MOD_ROWS, d), F32).at[0].set(c[0]).at[1].set(c_ctx)
    mod = _mod_call(cc, w_ada[0], b_ada, tn=tl["mod_tn"])

    h_all = _norm_call(x2d, ctx2d, norm1, mod, 0, 1, tm=tl["norm_tm"])
    za = _inproj_shift_call(h_all, wi_t, mu, segments, t_lat=t_lat, ctx_len=ctx_len,
                            tm=tl["shift_tm"], tn=tl["shift_tn"], n_sub=tl["shift_sub"])
    y_conv = _inproj_conv_call(h_all, wi_t, conv_w[0], row0=rwkv_cols, t_lat=t_lat,
                               tm=tl["conv_tm"], tc=tl["conv_tc"])
    n_lat_chunks, n_ctx_chunks = t_lat // CHUNK, ctx_len // CHUNK
    ys = [_scan_call(za, k_k, k_a, w0[0, i:i + 1], a0[0, i:i + 1], wd[i], wa[i], rev=(i == 1),
                     n_lat_chunks=n_lat_chunks, n_ctx_chunks=n_ctx_chunks, d_rwkv=d_rwkv,
                     n_sub=tl["scan_sub"]) for i in range(2)]
    y_rwkv = _readout_call(ys[0], ys[1], za, k_a, r_k.reshape(1, d_rwkv), a0[0], wa[0], wa[1],
                           w_gate_up[0].astype(BF16), lnx_w, lnx_b, t_lat=t_lat, d_rwkv=d_rwkv,
                           tm=tl["readout_tm"])
    x1, h2 = _outproj_call(y_rwkv, y_conv, wo, x2d, mod, 2 * d // tl["out_tn"], norm2, 3, 4,
                           tm=tl["out_tm"], tn=tl["out_tn"])

    act = _ffn_up_call(h2, w_ffn_gate[0], w_ffn_up[0], tm=tl["up_tm"], tn=tl["up_tn"])
    x2 = _ffn_down_call(act, wdn, x1, mod, 5 * d // tl["down_tn"], tm=tl["down_tm"], tn=tl["down_tn"])
    out = _final_norm_call(x2, norm_f[None, :], tm=tl["norm_tm"])
    return out[None]
```

```python
import functools
import math

import jax
import jax.numpy as jnp
from jax import lax
from jax.experimental import pallas as pl
from jax.experimental.pallas import tpu as pltpu

F32 = jnp.float32
BF16 = jnp.bfloat16

SUBLANES = 8
MXU_WIDTH = 256
SEG_W = MXU_WIDTH
VMEM_LIMIT_BYTES = 56 * 1024 * 1024

HEAD = 64
GROUP = 128
CHUNK = 64
GRID_W = 64
DECAY_LORA = 96
ICLR_LORA = 96
GATE_LORA = 256
LORA_TILE = 128
LORA_PAD = 2 * LORA_TILE
NORM_EPS = 1e-6
LNX_EPS = 64e-5
EXP_M05 = math.exp(-0.5)


def _cparams(*sem):
    return pltpu.CompilerParams(dimension_semantics=sem, vmem_limit_bytes=VMEM_LIMIT_BYTES)


def _dot(a, b):
    return jnp.dot(a, b, preferred_element_type=F32)


def _bmm(a, b):
    return lax.dot_general(a, b, (((2,), (1,)), ((0,), (0,))), preferred_element_type=F32)


def _bmm_nt(a, b):
    return lax.dot_general(a, b, (((2,), (2,)), ((0,), (0,))), preferred_element_type=F32)


def _bmm_tn(a, b):
    return lax.dot_general(a, b, (((1,), (1,)), ((0,), (0,))), preferred_element_type=F32)


def _split2(x):
    hi = x.astype(BF16)
    lo = (x - hi.astype(F32)).astype(BF16)
    return hi, lo


def _sigmoid(x):
    return 1.0 / (1.0 + jnp.exp(-x))


def _head_sums(x):
    rows, d = x.shape
    n_seg = d // SEG_W
    xs = jnp.concatenate([x[:, s * SEG_W:(s + 1) * SEG_W] for s in range(n_seg)], axis=0)
    rr = lax.broadcasted_iota(jnp.int32, (SEG_W, SEG_W), 0) // HEAD
    cc = lax.broadcasted_iota(jnp.int32, (SEG_W, SEG_W), 1) // HEAD
    ones_seg = (rr == cc).astype(BF16)
    hi, lo = _split2(xs)
    ss = _dot(hi, ones_seg) + _dot(lo, ones_seg)
    return jnp.concatenate([ss[s * rows:(s + 1) * rows] for s in range(n_seg)], axis=1)


def _lora_tiles(la):
    return la[:, :LORA_TILE], la[:, DECAY_LORA:DECAY_LORA + LORA_TILE]


def _scan_chunk(lw, l_cum, a, kd, kk, v, r, ht0, *, rev, n_groups):
    def st(x):
        return jnp.stack([x[:, p * GROUP:(p + 1) * GROUP] for p in range(n_groups)], axis=0)

    lw_s, l_s, a_s, kd_s, kk_s, v_s, r_s = (st(t) for t in (lw, l_cum, a, kd, kk, v, r))
    ltot = l_s[:, 0:1, :] if rev else l_s[:, CHUNK - 1:CHUNK, :]
    e_l = jnp.exp(l_s)
    e_nl = jnp.exp(-l_s)
    e_lx = jnp.exp(l_s - lw_s)
    e_tl = jnp.exp(ltot - l_s)
    g_tot = jnp.exp(ltot)

    t_i = lax.broadcasted_iota(jnp.int32, (1, CHUNK, GROUP), 1)
    lane = lax.broadcasted_iota(jnp.int32, (1, CHUNK, GROUP), 2)
    s_i = lane & (HEAD - 1)
    head_of_lane = lane // HEAD
    strict = (s_i > t_i) if rev else (s_i < t_i)
    incl = (s_i >= t_i) if rev else (s_i <= t_i)
    eye = (s_i == t_i).astype(F32)
    rr = lax.broadcasted_iota(jnp.int32, (GROUP, GROUP), 0)
    cc = lax.broadcasted_iota(jnp.int32, (GROUP, GROUP), 1)
    same_head = (rr // HEAD) == (cc // HEAD)

    def bd(x):
        z = jnp.zeros_like(x)
        return jnp.concatenate([jnp.where(head_of_lane == g, x, z) for g in range(GROUP // HEAD)], axis=1)

    b_s = kk_s * a_s
    ah = (-kk_s * e_lx).astype(BF16)
    rh = (r_s * e_l).astype(BF16)
    bc = (b_s * e_nl).astype(BF16)
    kc = (kd_s * e_nl).astype(BF16)
    kt = (kd_s * e_tl).astype(BF16)
    bt = (b_s * e_tl).astype(BF16)
    v_b = v_s.astype(BF16)

    sc = _bmm_nt(jnp.concatenate([ah, rh], axis=1),
                 jnp.concatenate([bd(bc), bd(kc)], axis=1))
    m_ab = jnp.where(strict, sc[:, :CHUNK, :GROUP], 0.0)
    m_ak = jnp.where(strict, sc[:, :CHUNK, GROUP:], 0.0).astype(BF16)
    n_rb = jnp.where(incl, sc[:, CHUNK:, :GROUP], 0.0).astype(BF16)
    n_rk = jnp.where(incl, sc[:, CHUNK:, GROUP:], 0.0).astype(BF16)

    s_acc = eye + m_ab
    m_pow = m_ab.astype(BF16)
    m_pow = _bmm(m_pow, bd(m_pow))
    n_steps = int(math.log2(CHUNK)) - 1
    for j in range(n_steps):
        mb = m_pow.astype(BF16)
        if j < n_steps - 1:
            both = _bmm(mb, jnp.concatenate([bd(mb), bd(s_acc.astype(BF16))], axis=2))
            m_pow = both[:, :, :GROUP]
            s_acc = s_acc + both[:, :, GROUP:]
        else:
            s_acc = s_acc + _bmm(mb, bd(s_acc.astype(BF16)))
    t_inv = s_acc.astype(BF16)

    ht0_b = ht0.astype(BF16)
    v_bd = bd(v_b)
    from_h0 = _bmm_nt(jnp.concatenate([ah, rh], axis=1), ht0_b)
    w = from_h0[:, :CHUNK] + _bmm(m_ak, v_bd)
    u_b = _bmm(t_inv, bd(w.astype(BF16))).astype(BF16)
    y = from_h0[:, CHUNK:] + _bmm(jnp.concatenate([n_rk, n_rb], axis=2),
                                  jnp.concatenate([v_bd, bd(u_b)], axis=1))
    upd = _bmm_tn(jnp.concatenate([v_b, u_b], axis=1), jnp.concatenate([kt, bt], axis=1))
    ht1 = ht0 * g_tot + jnp.where(same_head[None], upd, 0.0)
    return jnp.concatenate([y[p] for p in range(n_groups)], axis=1), ht1


def _scan_body(k_ref, v_ref, r_ref, la_ref, kk_ref, ka_ref, w0_ref, a0_ref, wd_ref, wa_ref,
               y_ref, h_ref, *, rev, n_groups):
    c = pl.program_id(0)

    @pl.when(c == 0)
    def _():
        h_ref[...] = jnp.zeros_like(h_ref)

    k = k_ref[...]
    v = v_ref[...]
    r = r_ref[...]
    la_dec, la_icl = _lora_tiles(la_ref[...])
    dec_pre = w0_ref[...] + _dot(jnp.tanh(la_dec).astype(BF16), wd_ref[...])
    icl_pre = a0_ref[...] + _dot(la_icl.astype(BF16), wa_ref[...])
    lw = -EXP_M05 * _sigmoid(dec_pre)
    a = _sigmoid(icl_pre)
    kd = k * (1.0 + (a - 1.0) * ka_ref[...])
    kkr = k * kk_ref[...]
    kk = kkr / jnp.maximum(jnp.sqrt(_head_sums(kkr * kkr)), 1e-12)

    rows = k.shape[0]
    row = lax.broadcasted_iota(jnp.int32, (rows, rows), 0)
    col = lax.broadcasted_iota(jnp.int32, (rows, rows), 1)
    in_order = (col >= row) if rev else (col <= row)
    tri = (in_order & ((row // CHUNK) == (col // CHUNK))).astype(BF16)
    lw_hi, lw_lo = _split2(lw)
    l_cum = _dot(tri, lw_hi) + _dot(tri, lw_lo)

    n_sub = rows // CHUNK
    ht = h_ref[...]
    ys = [None] * n_sub
    for s in (reversed(range(n_sub)) if rev else range(n_sub)):
        sl = slice(s * CHUNK, (s + 1) * CHUNK)
        ys[s], ht = _scan_chunk(lw[sl], l_cum[sl], a[sl], kd[sl], kk[sl], v[sl], r[sl], ht,
                                rev=rev, n_groups=n_groups)
    y_ref[...] = jnp.concatenate(ys, axis=0)
    h_ref[...] = ht


def _scan_call(za, kk, ka, w0, a0, wd, wa, *, rev, n_lat_chunks, n_ctx_chunks, d_rwkv, n_sub):
    assert n_lat_chunks % n_sub == 0 and n_ctx_chunks % n_sub == 0
    n_lat_blk, n_ctx_blk = n_lat_chunks // n_sub, n_ctx_chunks // n_sub
    n_blk = n_lat_blk + n_ctx_blk
    rows = n_sub * CHUNK
    n_groups = d_rwkv // GROUP
    lora_blk = 3 * d_rwkv // LORA_PAD
    assert CHUNK == HEAD and d_rwkv % GROUP == 0

    def blk_of(c):
        if rev:
            return n_blk - 1 - c
        return jnp.where(c < n_ctx_blk, n_lat_blk + c, c - n_ctx_blk)

    vec = pl.BlockSpec((1, d_rwkv), lambda c: (0, 0))
    mat = pl.BlockSpec((LORA_TILE, d_rwkv), lambda c: (0, 0))
    return pl.pallas_call(
        functools.partial(_scan_body, rev=rev, n_groups=n_groups),
        grid=(n_blk,),
        in_specs=[
            pl.BlockSpec((rows, d_rwkv), lambda c: (blk_of(c), 0)),
            pl.BlockSpec((rows, d_rwkv), lambda c: (blk_of(c), 1)),
            pl.BlockSpec((rows, d_rwkv), lambda c: (blk_of(c), 2)),
            pl.BlockSpec((rows, LORA_PAD), lambda c: (blk_of(c), lora_blk)),
            vec, vec, vec, vec, mat, mat,
        ],
        out_specs=pl.BlockSpec((rows, d_rwkv), lambda c: (blk_of(c), 0)),
        out_shape=jax.ShapeDtypeStruct((n_blk * rows, d_rwkv), F32),
        scratch_shapes=[pltpu.VMEM((n_groups, GROUP, GROUP), F32)],
        compiler_params=_cparams("arbitrary"),
        name="wkv_scan_rev" if rev else "wkv_scan_fwd",
    )(za, za, za, za, kk, ka, w0, a0, wd, wa)


MOD_ROWS = 8


def _mod_body(cc_ref, w_ref, b_ref, o_ref):
    cc = cc_ref[...]
    s = cc * _sigmoid(cc)
    s_hi, s_lo = _split2(s)
    w = w_ref[...].astype(BF16)
    o_ref[...] = _dot(s_hi, w) + _dot(s_lo, w) + b_ref[...]


def _mod_call(cc, w_ada, b_ada, *, tn):
    d, n = w_ada.shape
    return pl.pallas_call(
        _mod_body,
        grid=(n // tn,),
        in_specs=[pl.BlockSpec((MOD_ROWS, d), lambda j: (0, 0)),
                  pl.BlockSpec((d, tn), lambda j: (0, j)),
                  pl.BlockSpec((1, tn), lambda j: (0, j))],
        out_specs=pl.BlockSpec((MOD_ROWS, tn), lambda j: (0, j)),
        out_shape=jax.ShapeDtypeStruct((MOD_ROWS, n), F32),
        compiler_params=_cparams("arbitrary"),
        name="adaln_mod",
    )(cc, w_ada, b_ada)


def _norm_mod(xf, gain, shift, scale):
    y = xf * lax.rsqrt(jnp.mean(xf * xf, axis=-1, keepdims=True) + NORM_EPS) * gain
    return y * (1.0 + scale) + shift


def _norm_ctx_body(x_ref, ctx_ref, g_ref, sh_ref, sc_ref, o_ref, *, n_lat_blocks):
    i = pl.program_id(0)

    @pl.when(i < n_lat_blocks)
    def _():
        o_ref[...] = _norm_mod(x_ref[...], g_ref[...], sh_ref[0:1, :], sc_ref[0:1, :]).astype(o_ref.dtype)

    @pl.when(i >= n_lat_blocks)
    def _():
        o_ref[...] = _norm_mod(ctx_ref[...], g_ref[...], sh_ref[1:2, :], sc_ref[1:2, :]).astype(o_ref.dtype)


def _norm_call(x, ctx, gain, mod, shift_blk, scale_blk, *, tm):
    t, d = x.shape
    n_lat = t // tm
    vec = pl.BlockSpec((1, d), lambda i: (0, 0))
    sh = pl.BlockSpec((MOD_ROWS, d), lambda i: (0, shift_blk))
    sc = pl.BlockSpec((MOD_ROWS, d), lambda i: (0, scale_blk))
    tc = ctx.shape[0]
    n_ctx = tc // tm
    return pl.pallas_call(
        functools.partial(_norm_ctx_body, n_lat_blocks=n_lat), grid=(n_lat + n_ctx,),
        in_specs=[pl.BlockSpec((tm, d), lambda i: (jnp.minimum(i, n_lat - 1), 0)),
                  pl.BlockSpec((tm, d), lambda i: (jnp.maximum(i - n_lat, 0), 0)), vec, sh, sc],
        out_specs=pl.BlockSpec((tm, d), lambda i: (i, 0)),
        out_shape=jax.ShapeDtypeStruct((t + tc, d), BF16),
        compiler_params=_cparams("arbitrary"), name="norm_mod_ctx",
    )(x, ctx, gain, mod, mod)


def _neighbours(z, row0, t_lat, ctx_len):
    tm = z.shape[0]
    g = row0 + lax.broadcasted_iota(jnp.int32, (tm, 1), 0)
    is_ctx = g >= t_lat
    pos = jnp.where(is_ctx, g - t_lat, g & (GRID_W - 1))
    last = jnp.where(is_ctx, ctx_len - 1, GRID_W - 1)
    prev = jnp.where(pos == 0, 0.0, pltpu.roll(z, 1, 0))
    nxt = jnp.where(pos == last, 0.0, pltpu.roll(z, tm - 1, 0))
    return prev, nxt


def _dot_nt(a, b):
    return lax.dot_general(a, b, (((1,), (1,)), ((), ())), preferred_element_type=F32)


def _rows_spec(rows, width, row_of):
    return pl.BlockSpec((pl.Element(rows), pl.Element(width)),
                        lambda i, j: (pl.multiple_of(row_of(j), SUBLANES), 0))


def _inproj_shift_body(h_ref, wt_ref, mu_ref, o_ref, *, t_lat, ctx_len, n_sub):
    w = wt_ref[...].astype(BF16)
    tm = h_ref.shape[0]
    ts = tm // n_sub
    for s in range(n_sub):
        rows = slice(s * ts, (s + 1) * ts)
        z = _dot_nt(h_ref[rows, :], w)
        prev, nxt = _neighbours(z, pl.program_id(0) * tm + s * ts, t_lat, ctx_len)
        o_ref[rows, :] = z + mu_ref[...] * (0.5 * (prev + nxt) - z)


def _inproj_shift_call(h_all, w_in_t, mu, segments, *, t_lat, ctx_len, tm, tn, n_sub):
    t, d = h_all.shape
    assert tm % (n_sub * GRID_W) == 0
    assert all(r % SUBLANES == 0 and n % tn == 0 and r + n <= w_in_t.shape[0] for r, n in segments)
    tile0 = [sum(n for _, n in segments[:s]) // tn for s in range(len(segments) + 1)]
    n_tiles = tile0[-1]

    def row_of(j):
        return sum(jnp.where((j >= lo) & (j < hi), r + tn * (j - lo), 0)
                   for (r, _), lo, hi in zip(segments, tile0[:-1], tile0[1:]))

    return pl.pallas_call(
        functools.partial(_inproj_shift_body, t_lat=t_lat, ctx_len=ctx_len, n_sub=n_sub),
        grid=(t // tm, n_tiles),
        in_specs=[pl.BlockSpec((tm, d), lambda i, j: (i, 0), pipeline_mode=pl.Buffered(1)),
                  _rows_spec(tn, d, row_of),
                  pl.BlockSpec((1, tn), lambda i, j: (0, j))],
        out_specs=pl.BlockSpec((tm, tn), lambda i, j: (i, j)),
        out_shape=jax.ShapeDtypeStruct((t, n_tiles * tn), F32),
        compiler_params=_cparams("arbitrary", "arbitrary"), name="inproj_shift",
    )(h_all, w_in_t, mu)


def _inproj_conv_body(h_ref, wb_ref, wc_ref, wx_ref, cw_ref, o_ref, *, t_lat):
    h = h_ref[...]
    u = _dot_nt(h, wc_ref[...].astype(BF16)) * _dot_nt(h, wx_ref[...].astype(BF16))
    prev, nxt = _neighbours(u, pl.program_id(0) * u.shape[0], t_lat, 1)
    cw = cw_ref[...]
    conv = cw[0:1, :] * prev + cw[1:2, :] * u + cw[2:3, :] * nxt
    o_ref[...] = (_dot_nt(h, wb_ref[...].astype(BF16)) * conv).astype(o_ref.dtype)


def _inproj_conv_call(h_all, w_in_t, conv_w, *, row0, t_lat, tm, tc):
    d = h_all.shape[1]
    d_conv = conv_w.shape[1]
    assert d_conv % tc == 0 and row0 % SUBLANES == 0 and row0 + 3 * d_conv <= w_in_t.shape[0]
    cw = jnp.pad(conv_w, ((0, MOD_ROWS - 3), (0, 0)))
    return pl.pallas_call(
        functools.partial(_inproj_conv_body, t_lat=t_lat),
        grid=(t_lat // tm, d_conv // tc),
        in_specs=[pl.BlockSpec((tm, d), lambda i, j: (i, 0)),
                  _rows_spec(tc, d, lambda j: row0 + tc * j),
                  _rows_spec(tc, d, lambda j: row0 + d_conv + tc * j),
                  _rows_spec(tc, d, lambda j: row0 + 2 * d_conv + tc * j),
                  pl.BlockSpec((MOD_ROWS, tc), lambda i, j: (0, j))],
        out_specs=pl.BlockSpec((tm, tc), lambda i, j: (i, j)),
        out_shape=jax.ShapeDtypeStruct((t_lat, d_conv), BF16),
        compiler_params=_cparams("arbitrary", "arbitrary"), name="inproj_conv",
    )(h_all, w_in_t, w_in_t, w_in_t, cw)


def _readout_body(y0_ref, y1_ref, k_ref, v_ref, r_ref, la_ref, gl_ref, ka_ref, rk_ref, a0_ref,
                  wa0_ref, wa1_ref, wg_ref, lw_ref, lb_ref, o_ref):
    la = _lora_tiles(la_ref[...])[1].astype(BF16)
    a0 = a0_ref[...]
    a_sum = _sigmoid(a0[0:1, :] + _dot(la, wa0_ref[...])) + _sigmoid(a0[1:2, :] + _dot(la, wa1_ref[...]))
    coef = r_ref[...] * rk_ref[...] * k_ref[...] * (2.0 + (a_sum - 2.0) * ka_ref[...])
    y = y0_ref[...] + y1_ref[...]
    g = _dot(_sigmoid(gl_ref[...]).astype(BF16), wg_ref[...])

    yc = y - _head_sums(y) * (1.0 / HEAD)
    var = _head_sums(yc * yc) * (1.0 / HEAD)
    yn = yc * lax.rsqrt(var + LNX_EPS) * lw_ref[...] + lb_ref[...]
    bonus = _head_sums(coef) * v_ref[...]
    o_ref[...] = ((yn + bonus) * g).astype(o_ref.dtype)


def _readout_call(y0, y1, za, k_a, r_k, a0, wa0, wa1, w_gate, lnx_w, lnx_b, *, t_lat, d_rwkv, tm):
    row = lambda jb: pl.BlockSpec((tm, d_rwkv), lambda i: (i, jb))
    vec = pl.BlockSpec((1, d_rwkv), lambda i: (0, 0))
    mat = pl.BlockSpec((LORA_TILE, d_rwkv), lambda i: (0, 0))
    lora_blk = 3 * d_rwkv // LORA_PAD
    assert GATE_LORA == LORA_PAD
    return pl.pallas_call(
        _readout_body, grid=(t_lat // tm,),
        in_specs=[row(0), row(0), row(0), row(1), row(2),
                  pl.BlockSpec((tm, LORA_PAD), lambda i: (i, lora_blk)),
                  pl.BlockSpec((tm, GATE_LORA), lambda i: (i, lora_blk + 1)),
                  vec, vec, pl.BlockSpec((2, d_rwkv), lambda i: (0, 0)),
                  mat, mat, pl.BlockSpec((GATE_LORA, d_rwkv), lambda i: (0, 0)), vec, vec],
        out_specs=pl.BlockSpec((tm, d_rwkv), lambda i: (i, 0)),
        out_shape=jax.ShapeDtypeStruct((t_lat, d_rwkv), BF16),
        compiler_params=_cparams("arbitrary"), name="rwkv_readout",
    )(y0, y1, za, za, za, za, za, k_a, r_k, a0, wa0, wa1, w_gate, lnx_w, lnx_b)


def _outproj_body(ya_ref, yb_ref, wa_ref, wb_ref, x_ref, g_ref, n2_ref, sh_ref, sc_ref, o_ref, h_ref, rows_ref):
    j = pl.program_id(1)
    n_tiles, _, tn = rows_ref.shape
    acc = _dot(ya_ref[...], wa_ref[...]) + _dot(yb_ref[...], wb_ref[...])
    x1 = x_ref[...] + g_ref[0:1, :] * acc
    o_ref[...] = x1
    rows_ref[j] = x1

    @pl.when(j == n_tiles - 1)
    def _():
        ss = sum(jnp.sum(rows_ref[s] * rows_ref[s], axis=-1, keepdims=True) for s in range(n_tiles))
        rs = lax.rsqrt(ss * (1.0 / (n_tiles * tn)) + NORM_EPS)
        for s in range(n_tiles):
            sl = slice(s * tn, (s + 1) * tn)
            y = rows_ref[s] * rs * n2_ref[:, sl]
            h_ref[:, sl] = (y * (1.0 + sc_ref[0:1, sl]) + sh_ref[0:1, sl]).astype(h_ref.dtype)


def _outproj_call(ya, yb, w, x, mod, gate_blk0, gain2, shift_blk, scale_blk, *, tm, tn):
    t, da = ya.shape
    db = yb.shape[1]
    n = w.shape[1]
    assert da == db
    return pl.pallas_call(
        _outproj_body, grid=(t // tm, n // tn),
        in_specs=[pl.BlockSpec((tm, da), lambda i, j: (i, 0)),
                  pl.BlockSpec((tm, db), lambda i, j: (i, 0)),
                  pl.BlockSpec((da, tn), lambda i, j: (0, j)),
                  pl.BlockSpec((db, tn), lambda i, j: (1, j)),
                  pl.BlockSpec((tm, tn), lambda i, j: (i, j)),
                  pl.BlockSpec((MOD_ROWS, tn), lambda i, j: (0, gate_blk0 + j)),
                  pl.BlockSpec((1, n), lambda i, j: (0, 0)),
                  pl.BlockSpec((MOD_ROWS, n), lambda i, j: (0, shift_blk)),
                  pl.BlockSpec((MOD_ROWS, n), lambda i, j: (0, scale_blk))],
        out_specs=[pl.BlockSpec((tm, tn), lambda i, j: (i, j)),
                   pl.BlockSpec((tm, n), lambda i, j: (i, 0))],
        out_shape=[jax.ShapeDtypeStruct((t, n), F32), jax.ShapeDtypeStruct((t, n), BF16)],
        scratch_shapes=[pltpu.VMEM((n // tn, tm, tn), F32)],
        compiler_params=_cparams("arbitrary", "arbitrary"), name="outproj_residual_norm",
    )(ya, yb, w, w, x, mod, gain2, mod, mod)


def _ffn_up_body(h_ref, wg_ref, wu_ref, o_ref):
    h = h_ref[...]
    g = _dot(h, wg_ref[...].astype(BF16))
    u = _dot(h, wu_ref[...].astype(BF16))
    o_ref[...] = (g * _sigmoid(g) * u).astype(o_ref.dtype)


def _ffn_up_call(h, wg, wu, *, tm, tn):
    t, d = h.shape
    n = wg.shape[1]
    return pl.pallas_call(
        _ffn_up_body, grid=(t // tm, n // tn),
        in_specs=[pl.BlockSpec((tm, d), lambda i, j: (i, 0), pipeline_mode=pl.Buffered(1)),
                  pl.BlockSpec((d, tn), lambda i, j: (0, j)),
                  pl.BlockSpec((d, tn), lambda i, j: (0, j))],
        out_specs=pl.BlockSpec((tm, tn), lambda i, j: (i, j)),
        out_shape=jax.ShapeDtypeStruct((t, n), BF16),
        compiler_params=_cparams("arbitrary", "arbitrary"), name="ffn_gate_up",
    )(h, wg, wu)


def _ffn_down_body(a_ref, w_ref, x_ref, g_ref, o_ref):
    o_ref[...] = x_ref[...] + g_ref[0:1, :] * _dot(a_ref[...], w_ref[...])


def _ffn_down_call(act, w, x, mod, gate_blk0, *, tm, tn):
    t, kdim = act.shape
    d = w.shape[1]
    return pl.pallas_call(
        _ffn_down_body, grid=(t // tm, d // tn),
        in_specs=[pl.BlockSpec((tm, kdim), lambda i, j: (i, 0)),
                  pl.BlockSpec((kdim, tn), lambda i, j: (0, j)),
                  pl.BlockSpec((tm, tn), lambda i, j: (i, j)),
                  pl.BlockSpec((MOD_ROWS, tn), lambda i, j: (0, gate_blk0 + j))],
        out_specs=pl.BlockSpec((tm, tn), lambda i, j: (i, j)),
        out_shape=jax.ShapeDtypeStruct((t, d), F32),
        compiler_params=_cparams("arbitrary", "arbitrary"), name="ffn_down_residual",
    )(act, w, x, mod)


def _final_norm_body(x_ref, g_ref, o_ref):
    x = x_ref[...]
    o_ref[...] = x * lax.rsqrt(jnp.mean(x * x, axis=-1, keepdims=True) + NORM_EPS) * g_ref[...]


def _final_norm_call(x, gain, *, tm):
    t, d = x.shape
    return pl.pallas_call(
        _final_norm_body, grid=(t // tm,),
        in_specs=[pl.BlockSpec((tm, d), lambda i: (i, 0)), pl.BlockSpec((1, d), lambda i: (0, 0))],
        out_specs=pl.BlockSpec((tm, d), lambda i: (i, 0)),
        out_shape=jax.ShapeDtypeStruct((t, d), F32),
        compiler_params=_cparams("arbitrary"), name="final_norm",
    )(x, gain)


def _tiles():
    return dict(mod_tn=1024, norm_tm=256, shift_tm=2816, shift_tn=256, shift_sub=4, conv_tm=1024, conv_tc=256,
                scan_sub=4, readout_tm=256, out_tm=512, out_tn=1024, up_tm=2048, up_tn=256, down_tm=512, down_tn=512)


def kernel(x, c, ctx, c_ctx, w_ada, b_ada, norm1, w_in, mu_shift, k_k, k_a, r_k, w0, w_decay_up, a0, w_iclr_up, w_gate_up, lnx_w, lnx_b, conv_w, w_out, norm2, w_ffn_gate, w_ffn_up, w_ffn_down, norm_f):
    assert x.shape[0] == 1 and w_ada.shape[0] == 1, "single batch element, single layer"
    tl = _tiles()
    t_lat, d = x.shape[1], x.shape[2]
    ctx_len = ctx.shape[1]
    d_rwkv = k_k.shape[1]
    d_conv = conv_w.shape[2]
    d_ff = w_ffn_gate.shape[2]
    assert t_lat % GRID_W == 0 and ctx_len % CHUNK == 0 and d_rwkv % SEG_W == 0
    x2d, ctx2d = x[0], ctx[0]

    off_lora = 2 * d_rwkv
    off_r = off_lora + DECAY_LORA + ICLR_LORA
    off_gl = off_r + d_rwkv
    rwkv_cols = off_gl + GATE_LORA
    wi_t = jnp.swapaxes(w_in, 1, 2)[0]
    segments = [(0, off_lora), (off_r, d_rwkv), (off_lora, LORA_PAD), (off_gl, GATE_LORA)]
    ms = mu_shift[0]
    mu = jnp.concatenate([ms[r0:r0 + n] for r0, n in segments])[None, :]
    wd = [jnp.zeros((LORA_TILE, d_rwkv), BF16).at[:DECAY_LORA].set(w_decay_up[0, i].astype(BF16)) for i in range(2)]
    wa = [jnp.zeros((LORA_TILE, d_rwkv), BF16).at[:ICLR_LORA].set(w_iclr_up[0, i].astype(BF16)) for i in range(2)]
    wdn = w_ffn_down[0].astype(BF16)
    wo = w_out[0].astype(BF16)

    cc = jnp.zeros((MOD_ROWS, d), F32).at[0].set(c[0]).at[1].set(c_ctx)
    mod = _mod_call(cc, w_ada[0], b_ada, tn=tl["mod_tn"])

    h_all = _norm_call(x2d, ctx2d, norm1, mod, 0, 1, tm=tl["norm_tm"])
    za = _inproj_shift_call(h_all, wi_t, mu, segments, t_lat=t_lat, ctx_len=ctx_len,
                            tm=tl["shift_tm"], tn=tl["shift_tn"], n_sub=tl["shift_sub"])
    y_conv = _inproj_conv_call(h_all, wi_t, conv_w[0], row0=rwkv_cols, t_lat=t_lat,
                               tm=tl["conv_tm"], tc=tl["conv_tc"])
    n_lat_chunks, n_ctx_chunks = t_lat // CHUNK, ctx_len // CHUNK
    ys = [_scan_call(za, k_k, k_a, w0[0, i:i + 1], a0[0, i:i + 1], wd[i], wa[i], rev=(i == 1),
                     n_lat_chunks=n_lat_chunks, n_ctx_chunks=n_ctx_chunks, d_rwkv=d_rwkv,
                     n_sub=tl["scan_sub"]) for i in range(2)]
    y_rwkv = _readout_call(ys[0], ys[1], za, k_a, r_k.reshape(1, d_rwkv), a0[0], wa[0], wa[1],
                           w_gate_up[0].astype(BF16), lnx_w, lnx_b, t_lat=t_lat, d_rwkv=d_rwkv,
                           tm=tl["readout_tm"])
    x1, h2 = _outproj_call(y_rwkv, y_conv, wo, x2d, mod, 2 * d // tl["out_tn"], norm2, 3, 4,
                           tm=tl["out_tm"], tn=tl["out_tn"])

    act = _ffn_up_call(h2, w_ffn_gate[0], w_ffn_up[0], tm=tl["up_tm"], tn=tl["up_tn"])
    x2 = _ffn_down_call(act, wdn, x1, mod, 5 * d // tl["down_tn"], tm=tl["down_tm"], tn=tl["down_tn"])
    out = _final_norm_call(x2, norm_f[None, :], tm=tl["norm_tm"])
    return out[None]
```

```python
import functools
import math

import jax
import jax.numpy as jnp
from jax import lax
from jax.experimental import pallas as pl
from jax.experimental.pallas import tpu as pltpu

F32 = jnp.float32
BF16 = jnp.bfloat16

SUBLANES = 8
BF16_SUBLANES = 16
MXU_WIDTH = 256
SEG_W = MXU_WIDTH
VMEM_LIMIT_BYTES = 56 * 1024 * 1024

HEAD = 64
GROUP = 128
CHUNK = 64
GRID_W = 64
DECAY_LORA = 96
ICLR_LORA = 96
GATE_LORA = 256
LORA_TILE = 128
LORA_PAD = 2 * LORA_TILE
NORM_EPS = 1e-6
LNX_EPS = 64e-5
EXP_M05 = math.exp(-0.5)


def _cparams(*sem):
    return pltpu.CompilerParams(dimension_semantics=sem, vmem_limit_bytes=VMEM_LIMIT_BYTES)


def _dot(a, b):
    return jnp.dot(a, b, preferred_element_type=F32)


def _bmm(a, b):
    return lax.dot_general(a, b, (((2,), (1,)), ((0,), (0,))), preferred_element_type=F32)


def _bmm_nt(a, b):
    return lax.dot_general(a, b, (((2,), (2,)), ((0,), (0,))), preferred_element_type=F32)


def _bmm_tn(a, b):
    return lax.dot_general(a, b, (((1,), (1,)), ((0,), (0,))), preferred_element_type=F32)


def _split2(x):
    hi = x.astype(BF16)
    lo = (x - hi.astype(F32)).astype(BF16)
    return hi, lo


def _sigmoid(x):
    return 1.0 / (1.0 + jnp.exp(-x))


def _head_sums(x):
    rows, d = x.shape
    n_seg = d // SEG_W
    xs = jnp.concatenate([x[:, s * SEG_W:(s + 1) * SEG_W] for s in range(n_seg)], axis=0)
    rr = lax.broadcasted_iota(jnp.int32, (SEG_W, SEG_W), 0) // HEAD
    cc = lax.broadcasted_iota(jnp.int32, (SEG_W, SEG_W), 1) // HEAD
    ones_seg = (rr == cc).astype(BF16)
    hi, lo = _split2(xs)
    ss = _dot(hi, ones_seg) + _dot(lo, ones_seg)
    return jnp.concatenate([ss[s * rows:(s + 1) * rows] for s in range(n_seg)], axis=1)


def _lora_tiles(la):
    return la[:, :LORA_TILE], la[:, DECAY_LORA:DECAY_LORA + LORA_TILE]


def _scan_chunk(lw, l_cum, a, kd, kk, v, r, ht0, *, rev, n_groups):
    def st(x):
        return jnp.stack([x[:, p * GROUP:(p + 1) * GROUP] for p in range(n_groups)], axis=0)

    lw_s, l_s, a_s, kd_s, kk_s, v_s, r_s = (st(t) for t in (lw, l_cum, a, kd, kk, v, r))
    ltot = l_s[:, 0:1, :] if rev else l_s[:, CHUNK - 1:CHUNK, :]
    e_l = jnp.exp(l_s)
    e_nl = jnp.exp(-l_s)
    e_lx = jnp.exp(l_s - lw_s)
    e_tl = jnp.exp(ltot - l_s)
    g_tot = jnp.exp(ltot)

    t_i = lax.broadcasted_iota(jnp.int32, (1, CHUNK, GROUP), 1)
    lane = lax.broadcasted_iota(jnp.int32, (1, CHUNK, GROUP), 2)
    s_i = lane & (HEAD - 1)
    head_of_lane = lane // HEAD
    strict = (s_i > t_i) if rev else (s_i < t_i)
    incl = (s_i >= t_i) if rev else (s_i <= t_i)
    eye = (s_i == t_i).astype(F32)
    rr = lax.broadcasted_iota(jnp.int32, (GROUP, GROUP), 0)
    cc = lax.broadcasted_iota(jnp.int32, (GROUP, GROUP), 1)
    same_head = (rr // HEAD) == (cc // HEAD)

    def bd(x):
        z = jnp.zeros_like(x)
        return jnp.concatenate([jnp.where(head_of_lane == g, x, z) for g in range(GROUP // HEAD)], axis=1)

    b_s = kk_s * a_s
    ah = (-kk_s * e_lx).astype(BF16)
    rh = (r_s * e_l).astype(BF16)
    bc = (b_s * e_nl).astype(BF16)
    kc = (kd_s * e_nl).astype(BF16)
    kt = (kd_s * e_tl).astype(BF16)
    bt = (b_s * e_tl).astype(BF16)
    v_b = v_s.astype(BF16)

    sc = _bmm_nt(jnp.concatenate([ah, rh], axis=1),
                 jnp.concatenate([bd(bc), bd(kc)], axis=1))
    m_ab = jnp.where(strict, sc[:, :CHUNK, :GROUP], 0.0)
    m_ak = jnp.where(strict, sc[:, :CHUNK, GROUP:], 0.0).astype(BF16)
    n_rb = jnp.where(incl, sc[:, CHUNK:, :GROUP], 0.0).astype(BF16)
    n_rk = jnp.where(incl, sc[:, CHUNK:, GROUP:], 0.0).astype(BF16)

    s_acc = eye + m_ab
    m_pow = m_ab.astype(BF16)
    m_pow = _bmm(m_pow, bd(m_pow))
    n_steps = int(math.log2(CHUNK)) - 1
    for j in range(n_steps):
        mb = m_pow.astype(BF16)
        if j < n_steps - 1:
            both = _bmm(mb, jnp.concatenate([bd(mb), bd(s_acc.astype(BF16))], axis=2))
            m_pow = both[:, :, :GROUP]
            s_acc = s_acc + both[:, :, GROUP:]
        else:
            s_acc = s_acc + _bmm(mb, bd(s_acc.astype(BF16)))
    t_inv = s_acc.astype(BF16)

    ht0_b = ht0.astype(BF16)
    v_bd = bd(v_b)
    from_h0 = _bmm_nt(jnp.concatenate([ah, rh], axis=1), ht0_b)
    w = from_h0[:, :CHUNK] + _bmm(m_ak, v_bd)
    u_b = _bmm(t_inv, bd(w.astype(BF16))).astype(BF16)
    y = from_h0[:, CHUNK:] + _bmm(jnp.concatenate([n_rk, n_rb], axis=2),
                                  jnp.concatenate([v_bd, bd(u_b)], axis=1))
    upd = _bmm_tn(jnp.concatenate([v_b, u_b], axis=1), jnp.concatenate([kt, bt], axis=1))
    ht1 = ht0 * g_tot + jnp.where(same_head[None], upd, 0.0)
    return jnp.concatenate([y[p] for p in range(n_groups)], axis=1), ht1


def _scan_body(k_ref, v_ref, r_ref, la_ref, kk_ref, ka_ref, w0_ref, a0_ref, wd_ref, wa_ref,
               y_ref, h_ref, *, rev, n_groups):
    c = pl.program_id(0)

    @pl.when(c == 0)
    def _():
        h_ref[...] = jnp.zeros_like(h_ref)

    k = k_ref[...]
    v = v_ref[...]
    r = r_ref[...]
    la_dec, la_icl = _lora_tiles(la_ref[...])
    dec_pre = w0_ref[...] + _dot(jnp.tanh(la_dec).astype(BF16), wd_ref[...])
    icl_pre = a0_ref[...] + _dot(la_icl.astype(BF16), wa_ref[...])
    lw = -EXP_M05 * _sigmoid(dec_pre)
    a = _sigmoid(icl_pre)
    kd = k * (1.0 + (a - 1.0) * ka_ref[...])
    kkr = k * kk_ref[...]
    kk = kkr / jnp.maximum(jnp.sqrt(_head_sums(kkr * kkr)), 1e-12)

    rows = k.shape[0]
    row = lax.broadcasted_iota(jnp.int32, (rows, rows), 0)
    col = lax.broadcasted_iota(jnp.int32, (rows, rows), 1)
    in_order = (col >= row) if rev else (col <= row)
    tri = (in_order & ((row // CHUNK) == (col // CHUNK))).astype(BF16)
    lw_hi, lw_lo = _split2(lw)
    l_cum = _dot(tri, lw_hi) + _dot(tri, lw_lo)

    n_sub = rows // CHUNK
    ht = h_ref[...]
    ys = [None] * n_sub
    for s in (reversed(range(n_sub)) if rev else range(n_sub)):
        sl = slice(s * CHUNK, (s + 1) * CHUNK)
        ys[s], ht = _scan_chunk(lw[sl], l_cum[sl], a[sl], kd[sl], kk[sl], v[sl], r[sl], ht,
                                rev=rev, n_groups=n_groups)
    y_ref[...] = jnp.concatenate(ys, axis=0)
    h_ref[...] = ht


def _scan_call(za, kk, ka, w0, a0, wd, wa, *, rev, n_lat_chunks, n_ctx_chunks, d_rwkv, n_sub):
    assert n_lat_chunks % n_sub == 0 and n_ctx_chunks % n_sub == 0
    n_lat_blk, n_ctx_blk = n_lat_chunks // n_sub, n_ctx_chunks // n_sub
    n_blk = n_lat_blk + n_ctx_blk
    rows = n_sub * CHUNK
    n_groups = d_rwkv // GROUP
    lora_blk = 3 * d_rwkv // LORA_PAD
    assert CHUNK == HEAD and d_rwkv % GROUP == 0

    def blk_of(c):
        if rev:
            return n_blk - 1 - c
        return jnp.where(c < n_ctx_blk, n_lat_blk + c, c - n_ctx_blk)

    vec = pl.BlockSpec((1, d_rwkv), lambda c: (0, 0))
    mat = pl.BlockSpec((LORA_TILE, d_rwkv), lambda c: (0, 0))
    return pl.pallas_call(
        functools.partial(_scan_body, rev=rev, n_groups=n_groups),
        grid=(n_blk,),
        in_specs=[
            pl.BlockSpec((rows, d_rwkv), lambda c: (blk_of(c), 0)),
            pl.BlockSpec((rows, d_rwkv), lambda c: (blk_of(c), 1)),
            pl.BlockSpec((rows, d_rwkv), lambda c: (blk_of(c), 2)),
            pl.BlockSpec((rows, LORA_PAD), lambda c: (blk_of(c), lora_blk)),
            vec, vec, vec, vec, mat, mat,
        ],
        out_specs=pl.BlockSpec((rows, d_rwkv), lambda c: (blk_of(c), 0)),
        out_shape=jax.ShapeDtypeStruct((n_blk * rows, d_rwkv), F32),
        scratch_shapes=[pltpu.VMEM((n_groups, GROUP, GROUP), F32)],
        compiler_params=_cparams("arbitrary"),
        name="wkv_scan_rev" if rev else "wkv_scan_fwd",
    )(za, za, za, za, kk, ka, w0, a0, wd, wa)


MOD_ROWS = 8


def _mod_body(cc_ref, w_ref, b_ref, o_ref):
    cc = cc_ref[...]
    s = cc * _sigmoid(cc)
    s_hi, s_lo = _split2(s)
    w = w_ref[...].astype(BF16)
    o_ref[...] = _dot(s_hi, w) + _dot(s_lo, w) + b_ref[...]


def _mod_call(cc, w_ada, b_ada, *, tn):
    d, n = w_ada.shape
    return pl.pallas_call(
        _mod_body,
        grid=(n // tn,),
        in_specs=[pl.BlockSpec((MOD_ROWS, d), lambda j: (0, 0)),
                  pl.BlockSpec((d, tn), lambda j: (0, j)),
                  pl.BlockSpec((1, tn), lambda j: (0, j))],
        out_specs=pl.BlockSpec((MOD_ROWS, tn), lambda j: (0, j)),
        out_shape=jax.ShapeDtypeStruct((MOD_ROWS, n), F32),
        compiler_params=_cparams("arbitrary"),
        name="adaln_mod",
    )(cc, w_ada, b_ada)


def _norm_mod(xf, gain, shift, scale):
    y = xf * lax.rsqrt(jnp.mean(xf * xf, axis=-1, keepdims=True) + NORM_EPS) * gain
    return y * (1.0 + scale) + shift


def _norm_ctx_body(x_ref, ctx_ref, g_ref, sh_ref, sc_ref, o_ref, *, n_lat_blocks):
    i = pl.program_id(0)

    @pl.when(i < n_lat_blocks)
    def _():
        o_ref[...] = _norm_mod(x_ref[...], g_ref[...], sh_ref[0:1, :], sc_ref[0:1, :]).astype(o_ref.dtype)

    @pl.when(i >= n_lat_blocks)
    def _():
        o_ref[...] = _norm_mod(ctx_ref[...], g_ref[...], sh_ref[1:2, :], sc_ref[1:2, :]).astype(o_ref.dtype)


def _norm_call(x, ctx, gain, mod, shift_blk, scale_blk, *, tm):
    t, d = x.shape
    n_lat = t // tm
    vec = pl.BlockSpec((1, d), lambda i: (0, 0))
    sh = pl.BlockSpec((MOD_ROWS, d), lambda i: (0, shift_blk))
    sc = pl.BlockSpec((MOD_ROWS, d), lambda i: (0, scale_blk))
    tc = ctx.shape[0]
    n_ctx = tc // tm
    return pl.pallas_call(
        functools.partial(_norm_ctx_body, n_lat_blocks=n_lat), grid=(n_lat + n_ctx,),
        in_specs=[pl.BlockSpec((tm, d), lambda i: (jnp.minimum(i, n_lat - 1), 0)),
                  pl.BlockSpec((tm, d), lambda i: (jnp.maximum(i - n_lat, 0), 0)), vec, sh, sc],
        out_specs=pl.BlockSpec((tm, d), lambda i: (i, 0)),
        out_shape=jax.ShapeDtypeStruct((t + tc, d), BF16),
        compiler_params=_cparams("arbitrary"), name="norm_mod_ctx",
    )(x, ctx, gain, mod, mod)


def _neighbours(z, row0, t_lat, ctx_len):
    tm = z.shape[0]
    g = row0 + lax.broadcasted_iota(jnp.int32, (tm, 1), 0)
    is_ctx = g >= t_lat
    pos = jnp.where(is_ctx, g - t_lat, g & (GRID_W - 1))
    last = jnp.where(is_ctx, ctx_len - 1, GRID_W - 1)
    prev = jnp.where(pos == 0, 0.0, pltpu.roll(z, 1, 0))
    nxt = jnp.where(pos == last, 0.0, pltpu.roll(z, tm - 1, 0))
    return prev, nxt


def _dot_nt(a, b):
    return lax.dot_general(a, b, (((1,), (1,)), ((), ())), preferred_element_type=F32)


def _rows_spec(rows, width, row_of):
    return pl.BlockSpec((pl.Element(rows), pl.Element(width)),
                        lambda i, j: (pl.multiple_of(row_of(j), SUBLANES), 0))


def _inproj_shift_body(h_ref, wt_ref, mu_ref, o_ref, *, t_lat, ctx_len, n_sub):
    w = wt_ref[...].astype(BF16)
    tm = h_ref.shape[0]
    ts = tm // n_sub
    for s in range(n_sub):
        rows = slice(s * ts, (s + 1) * ts)
        z = _dot_nt(h_ref[rows, :], w)
        prev, nxt = _neighbours(z, pl.program_id(0) * tm + s * ts, t_lat, ctx_len)
        o_ref[rows, :] = z + mu_ref[...] * (0.5 * (prev + nxt) - z)


def _inproj_shift_call(h_all, w_in_t, mu, segments, *, t_lat, ctx_len, tm, tn, n_sub):
    t, d = h_all.shape
    assert tm % (n_sub * GRID_W) == 0
    assert all(r % SUBLANES == 0 and n % tn == 0 and r + n <= w_in_t.shape[0] for r, n in segments)
    tile0 = [sum(n for _, n in segments[:s]) // tn for s in range(len(segments) + 1)]
    n_tiles = tile0[-1]

    def row_of(j):
        return sum(jnp.where((j >= lo) & (j < hi), r + tn * (j - lo), 0)
                   for (r, _), lo, hi in zip(segments, tile0[:-1], tile0[1:]))

    return pl.pallas_call(
        functools.partial(_inproj_shift_body, t_lat=t_lat, ctx_len=ctx_len, n_sub=n_sub),
        grid=(t // tm, n_tiles),
        in_specs=[pl.BlockSpec((tm, d), lambda i, j: (i, 0), pipeline_mode=pl.Buffered(1)),
                  _rows_spec(tn, d, row_of),
                  pl.BlockSpec((1, tn), lambda i, j: (0, j))],
        out_specs=pl.BlockSpec((tm, tn), lambda i, j: (i, j)),
        out_shape=jax.ShapeDtypeStruct((t, n_tiles * tn), F32),
        compiler_params=_cparams("arbitrary", "arbitrary"), name="inproj_shift",
    )(h_all, w_in_t, mu)


def _inproj_conv_body(h_ref, wb_ref, wc_ref, wx_ref, cw_ref, wcast_ref, o_ref, wcast_o_ref, *, t_lat):
    wcast_o_ref[...] = wcast_ref[...].astype(BF16)
    h = h_ref[...]
    u = _dot_nt(h, wc_ref[...].astype(BF16)) * _dot_nt(h, wx_ref[...].astype(BF16))
    prev, nxt = _neighbours(u, pl.program_id(0) * u.shape[0], t_lat, 1)
    cw = cw_ref[...]
    conv = cw[0:1, :] * prev + cw[1:2, :] * u + cw[2:3, :] * nxt
    o_ref[...] = (_dot_nt(h, wb_ref[...].astype(BF16)) * conv).astype(o_ref.dtype)


def _inproj_conv_call(h_all, w_in_t, conv_w, w_cast, *, row0, t_lat, tm, tc):
    d = h_all.shape[1]
    d_conv = conv_w.shape[1]
    assert d_conv % tc == 0 and row0 % SUBLANES == 0 and row0 + 3 * d_conv <= w_in_t.shape[0]
    cw = jnp.pad(conv_w, ((0, MOD_ROWS - 3), (0, 0)))
    n_j = d_conv // tc
    c_in, c_out, c_shape = _cast_rows_specs(w_cast, (t_lat // tm) * n_j, lambda i, j: i * n_j + j)
    return pl.pallas_call(
        functools.partial(_inproj_conv_body, t_lat=t_lat),
        grid=(t_lat // tm, n_j),
        in_specs=[pl.BlockSpec((tm, d), lambda i, j: (i, 0)),
                  _rows_spec(tc, d, lambda j: row0 + tc * j),
                  _rows_spec(tc, d, lambda j: row0 + d_conv + tc * j),
                  _rows_spec(tc, d, lambda j: row0 + 2 * d_conv + tc * j),
                  pl.BlockSpec((MOD_ROWS, tc), lambda i, j: (0, j)),
                  c_in],
        out_specs=[pl.BlockSpec((tm, tc), lambda i, j: (i, j)), c_out],
        out_shape=[jax.ShapeDtypeStruct((t_lat, d_conv), BF16), c_shape],
        compiler_params=_cparams("arbitrary", "arbitrary"), name="inproj_conv",
    )(h_all, w_in_t, w_in_t, w_in_t, cw, w_cast)


def _readout_body(y0_ref, y1_ref, k_ref, v_ref, r_ref, la_ref, gl_ref, ka_ref, rk_ref, a0_ref,
                  wa0_ref, wa1_ref, wg_ref, lw_ref, lb_ref, o_ref):
    la = _lora_tiles(la_ref[...])[1].astype(BF16)
    a0 = a0_ref[...]
    a_sum = _sigmoid(a0[0:1, :] + _dot(la, wa0_ref[...])) + _sigmoid(a0[1:2, :] + _dot(la, wa1_ref[...]))
    coef = r_ref[...] * rk_ref[...] * k_ref[...] * (2.0 + (a_sum - 2.0) * ka_ref[...])
    y = y0_ref[...] + y1_ref[...]
    g = _dot(_sigmoid(gl_ref[...]).astype(BF16), wg_ref[...])

    yc = y - _head_sums(y) * (1.0 / HEAD)
    var = _head_sums(yc * yc) * (1.0 / HEAD)
    yn = yc * lax.rsqrt(var + LNX_EPS) * lw_ref[...] + lb_ref[...]
    bonus = _head_sums(coef) * v_ref[...]
    o_ref[...] = ((yn + bonus) * g).astype(o_ref.dtype)


def _readout_call(y0, y1, za, k_a, r_k, a0, wa0, wa1, w_gate, lnx_w, lnx_b, *, t_lat, d_rwkv, tm):
    row = lambda jb: pl.BlockSpec((tm, d_rwkv), lambda i: (i, jb))
    vec = pl.BlockSpec((1, d_rwkv), lambda i: (0, 0))
    mat = pl.BlockSpec((LORA_TILE, d_rwkv), lambda i: (0, 0))
    lora_blk = 3 * d_rwkv // LORA_PAD
    assert GATE_LORA == LORA_PAD
    return pl.pallas_call(
        _readout_body, grid=(t_lat // tm,),
        in_specs=[row(0), row(0), row(0), row(1), row(2),
                  pl.BlockSpec((tm, LORA_PAD), lambda i: (i, lora_blk)),
                  pl.BlockSpec((tm, GATE_LORA), lambda i: (i, lora_blk + 1)),
                  vec, vec, pl.BlockSpec((2, d_rwkv), lambda i: (0, 0)),
                  mat, mat, pl.BlockSpec((GATE_LORA, d_rwkv), lambda i: (0, 0)), vec, vec],
        out_specs=pl.BlockSpec((tm, d_rwkv), lambda i: (i, 0)),
        out_shape=jax.ShapeDtypeStruct((t_lat, d_rwkv), BF16),
        compiler_params=_cparams("arbitrary"), name="rwkv_readout",
    )(y0, y1, za, za, za, za, za, k_a, r_k, a0, wa0, wa1, w_gate, lnx_w, lnx_b)


def _outproj_body(ya_ref, yb_ref, wa_ref, wb_ref, x_ref, g_ref, n2_ref, sh_ref, sc_ref, o_ref, h_ref, rows_ref):
    j = pl.program_id(1)
    n_tiles, _, tn = rows_ref.shape
    acc = _dot(ya_ref[...], wa_ref[...]) + _dot(yb_ref[...], wb_ref[...])
    x1 = x_ref[...] + g_ref[0:1, :] * acc
    o_ref[...] = x1
    rows_ref[j] = x1

    @pl.when(j == n_tiles - 1)
    def _():
        ss = sum(jnp.sum(rows_ref[s] * rows_ref[s], axis=-1, keepdims=True) for s in range(n_tiles))
        rs = lax.rsqrt(ss * (1.0 / (n_tiles * tn)) + NORM_EPS)
        for s in range(n_tiles):
            sl = slice(s * tn, (s + 1) * tn)
            y = rows_ref[s] * rs * n2_ref[:, sl]
            h_ref[:, sl] = (y * (1.0 + sc_ref[0:1, sl]) + sh_ref[0:1, sl]).astype(h_ref.dtype)


def _outproj_call(ya, yb, w, x, mod, gate_blk0, gain2, shift_blk, scale_blk, *, tm, tn):
    t, da = ya.shape
    db = yb.shape[1]
    n = w.shape[1]
    assert da == db
    return pl.pallas_call(
        _outproj_body, grid=(t // tm, n // tn),
        in_specs=[pl.BlockSpec((tm, da), lambda i, j: (i, 0)),
                  pl.BlockSpec((tm, db), lambda i, j: (i, 0)),
                  pl.BlockSpec((da, tn), lambda i, j: (0, j)),
                  pl.BlockSpec((db, tn), lambda i, j: (1, j)),
                  pl.BlockSpec((tm, tn), lambda i, j: (i, j)),
                  pl.BlockSpec((MOD_ROWS, tn), lambda i, j: (0, gate_blk0 + j)),
                  pl.BlockSpec((1, n), lambda i, j: (0, 0)),
                  pl.BlockSpec((MOD_ROWS, n), lambda i, j: (0, shift_blk)),
                  pl.BlockSpec((MOD_ROWS, n), lambda i, j: (0, scale_blk))],
        out_specs=[pl.BlockSpec((tm, tn), lambda i, j: (i, j)),
                   pl.BlockSpec((tm, n), lambda i, j: (i, 0))],
        out_shape=[jax.ShapeDtypeStruct((t, n), F32), jax.ShapeDtypeStruct((t, n), BF16)],
        scratch_shapes=[pltpu.VMEM((n // tn, tm, tn), F32)],
        compiler_params=_cparams("arbitrary", "arbitrary"), name="outproj_residual_norm",
    )(ya, yb, w, w, x, mod, gain2, mod, mod)


def _cast_rows_specs(w, n_steps, step_of):
    rows, cols = w.shape
    assert rows % n_steps == 0 and (rows // n_steps) % BF16_SUBLANES == 0
    blk = (rows // n_steps, cols)
    return (pl.BlockSpec(blk, lambda i, j: (step_of(i, j), 0)), pl.BlockSpec(blk, lambda i, j: (step_of(i, j), 0)),
            jax.ShapeDtypeStruct(w.shape, BF16))


def _ffn_up_body(h_ref, wg_ref, wu_ref, wcast_ref, o_ref, wcast_o_ref):
    h = h_ref[...]
    g = _dot(h, wg_ref[...].astype(BF16))
    u = _dot(h, wu_ref[...].astype(BF16))
    o_ref[...] = (g * _sigmoid(g) * u).astype(o_ref.dtype)
    wcast_o_ref[...] = wcast_ref[...].astype(BF16)


def _ffn_up_call(h, wg, wu, w_cast, *, tm, tn):
    t, d = h.shape
    n = wg.shape[1]
    n_j = n // tn
    c_in, c_out, c_shape = _cast_rows_specs(w_cast, (t // tm) * n_j, lambda i, j: i * n_j + j)
    return pl.pallas_call(
        _ffn_up_body, grid=(t // tm, n_j),
        in_specs=[pl.BlockSpec((tm, d), lambda i, j: (i, 0), pipeline_mode=pl.Buffered(1)),
                  pl.BlockSpec((d, tn), lambda i, j: (0, j)),
                  pl.BlockSpec((d, tn), lambda i, j: (0, j)),
                  c_in],
        out_specs=[pl.BlockSpec((tm, tn), lambda i, j: (i, j)), c_out],
        out_shape=[jax.ShapeDtypeStruct((t, n), BF16), c_shape],
        compiler_params=_cparams("arbitrary", "arbitrary"), name="ffn_gate_up",
    )(h, wg, wu, w_cast)


def _ffn_down_body(a_ref, w_ref, x_ref, g_ref, o_ref):
    o_ref[...] = x_ref[...] + g_ref[0:1, :] * _dot(a_ref[...], w_ref[...])


def _ffn_down_call(act, w, x, mod, gate_blk0, *, tm, tn):
    t, kdim = act.shape
    d = w.shape[1]
    return pl.pallas_call(
        _ffn_down_body, grid=(t // tm, d // tn),
        in_specs=[pl.BlockSpec((tm, kdim), lambda i, j: (i, 0)),
                  pl.BlockSpec((kdim, tn), lambda i, j: (0, j)),
                  pl.BlockSpec((tm, tn), lambda i, j: (i, j)),
                  pl.BlockSpec((MOD_ROWS, tn), lambda i, j: (0, gate_blk0 + j))],
        out_specs=pl.BlockSpec((tm, tn), lambda i, j: (i, j)),
        out_shape=jax.ShapeDtypeStruct((t, d), F32),
        compiler_params=_cparams("arbitrary", "arbitrary"), name="ffn_down_residual",
    )(act, w, x, mod)


def _final_norm_body(x_ref, g_ref, o_ref):
    x = x_ref[...]
    o_ref[...] = x * lax.rsqrt(jnp.mean(x * x, axis=-1, keepdims=True) + NORM_EPS) * g_ref[...]


def _final_norm_call(x, gain, *, tm):
    t, d = x.shape
    return pl.pallas_call(
        _final_norm_body, grid=(t // tm,),
        in_specs=[pl.BlockSpec((tm, d), lambda i: (i, 0)), pl.BlockSpec((1, d), lambda i: (0, 0))],
        out_specs=pl.BlockSpec((tm, d), lambda i: (i, 0)),
        out_shape=jax.ShapeDtypeStruct((t, d), F32),
        compiler_params=_cparams("arbitrary"), name="final_norm",
    )(x, gain)


def _tiles():
    return dict(mod_tn=1024, norm_tm=256, shift_tm=2816, shift_tn=256, shift_sub=4, conv_tm=1024, conv_tc=256,
                scan_sub=4, readout_tm=256, out_tm=512, out_tn=1024, up_tm=2048, up_tn=256, down_tm=512, down_tn=512)


def kernel(x, c, ctx, c_ctx, w_ada, b_ada, norm1, w_in, mu_shift, k_k, k_a, r_k, w0, w_decay_up, a0, w_iclr_up, w_gate_up, lnx_w, lnx_b, conv_w, w_out, norm2, w_ffn_gate, w_ffn_up, w_ffn_down, norm_f):
    assert x.shape[0] == 1 and w_ada.shape[0] == 1, "single batch element, single layer"
    tl = _tiles()
    t_lat, d = x.shape[1], x.shape[2]
    ctx_len = ctx.shape[1]
    d_rwkv = k_k.shape[1]
    d_conv = conv_w.shape[2]
    d_ff = w_ffn_gate.shape[2]
    assert t_lat % GRID_W == 0 and ctx_len % CHUNK == 0 and d_rwkv % SEG_W == 0
    x2d, ctx2d = x[0], ctx[0]

    off_lora = 2 * d_rwkv
    off_r = off_lora + DECAY_LORA + ICLR_LORA
    off_gl = off_r + d_rwkv
    rwkv_cols = off_gl + GATE_LORA
    wi_t = jnp.swapaxes(w_in, 1, 2)[0]
    segments = [(0, off_lora), (off_r, d_rwkv), (off_lora, LORA_PAD), (off_gl, GATE_LORA)]
    ms = mu_shift[0]
    mu = jnp.concatenate([ms[r0:r0 + n] for r0, n in segments])[None, :]
    wd = [jnp.zeros((LORA_TILE, d_rwkv), BF16).at[:DECAY_LORA].set(w_decay_up[0, i].astype(BF16)) for i in range(2)]
    wa = [jnp.zeros((LORA_TILE, d_rwkv), BF16).at[:ICLR_LORA].set(w_iclr_up[0, i].astype(BF16)) for i in range(2)]

    cc = jnp.zeros((MOD_ROWS, d), F32).at[0].set(c[0]).at[1].set(c_ctx)
    mod = _mod_call(cc, w_ada[0], b_ada, tn=tl["mod_tn"])

    h_all = _norm_call(x2d, ctx2d, norm1, mod, 0, 1, tm=tl["norm_tm"])
    za = _inproj_shift_call(h_all, wi_t, mu, segments, t_lat=t_lat, ctx_len=ctx_len,
                            tm=tl["shift_tm"], tn=tl["shift_tn"], n_sub=tl["shift_sub"])
    y_conv, wo = _inproj_conv_call(h_all, wi_t, conv_w[0], w_out[0], row0=rwkv_cols, t_lat=t_lat,
                                   tm=tl["conv_tm"], tc=tl["conv_tc"])
    n_lat_chunks, n_ctx_chunks = t_lat // CHUNK, ctx_len // CHUNK
    ys = [_scan_call(za, k_k, k_a, w0[0, i:i + 1], a0[0, i:i + 1], wd[i], wa[i], rev=(i == 1),
                     n_lat_chunks=n_lat_chunks, n_ctx_chunks=n_ctx_chunks, d_rwkv=d_rwkv,
                     n_sub=tl["scan_sub"]) for i in range(2)]
    y_rwkv = _readout_call(ys[0], ys[1], za, k_a, r_k.reshape(1, d_rwkv), a0[0], wa[0], wa[1],
                           w_gate_up[0].astype(BF16), lnx_w, lnx_b, t_lat=t_lat, d_rwkv=d_rwkv,
                           tm=tl["readout_tm"])
    x1, h2 = _outproj_call(y_rwkv, y_conv, wo, x2d, mod, 2 * d // tl["out_tn"], norm2, 3, 4,
                           tm=tl["out_tm"], tn=tl["out_tn"])

    act, wdn = _ffn_up_call(h2, w_ffn_gate[0], w_ffn_up[0], w_ffn_down[0], tm=tl["up_tm"], tn=tl["up_tn"])
    x2 = _ffn_down_call(act, wdn, x1, mod, 5 * d // tl["down_tn"], tm=tl["down_tm"], tn=tl["down_tn"])
    out = _final_norm_call(x2, norm_f[None, :], tm=tl["norm_tm"])
    return out[None]
```

```python
import functools
import math

import jax
import jax.numpy as jnp
from jax import lax
from jax.experimental import pallas as pl
from jax.experimental.pallas import tpu as pltpu

F32 = jnp.float32
BF16 = jnp.bfloat16

SUBLANES = 8
BF16_SUBLANES = 16
MXU_WIDTH = 256
SEG_W = MXU_WIDTH
VMEM_LIMIT_BYTES = 56 * 1024 * 1024

HEAD = 64
GROUP = 128
CHUNK = 64
GRID_W = 64
DECAY_LORA = 96
ICLR_LORA = 96
GATE_LORA = 256
LORA_TILE = 128
LORA_PAD = 2 * LORA_TILE
NORM_EPS = 1e-6
LNX_EPS = 64e-5
EXP_M05 = math.exp(-0.5)


def _cparams(*sem):
    return pltpu.CompilerParams(dimension_semantics=sem, vmem_limit_bytes=VMEM_LIMIT_BYTES)


def _dot(a, b):
    return jnp.dot(a, b, preferred_element_type=F32)


def _bmm(a, b):
    return lax.dot_general(a, b, (((2,), (1,)), ((0,), (0,))), preferred_element_type=F32)


def _bmm_nt(a, b):
    return lax.dot_general(a, b, (((2,), (2,)), ((0,), (0,))), preferred_element_type=F32)


def _bmm_tn(a, b):
    return lax.dot_general(a, b, (((1,), (1,)), ((0,), (0,))), preferred_element_type=F32)


def _split2(x):
    hi = x.astype(BF16)
    lo = (x - hi.astype(F32)).astype(BF16)
    return hi, lo


def _sigmoid(x):
    return 1.0 / (1.0 + jnp.exp(-x))


def _head_sums(x):
    rows, d = x.shape
    n_seg = d // SEG_W
    xs = jnp.concatenate([x[:, s * SEG_W:(s + 1) * SEG_W] for s in range(n_seg)], axis=0)
    rr = lax.broadcasted_iota(jnp.int32, (SEG_W, SEG_W), 0) // HEAD
    cc = lax.broadcasted_iota(jnp.int32, (SEG_W, SEG_W), 1) // HEAD
    ones_seg = (rr == cc).astype(BF16)
    hi, lo = _split2(xs)
    ss = _dot(hi, ones_seg) + _dot(lo, ones_seg)
    return jnp.concatenate([ss[s * rows:(s + 1) * rows] for s in range(n_seg)], axis=1)


def _lora_tiles(la):
    return la[:, :LORA_TILE], la[:, DECAY_LORA:DECAY_LORA + LORA_TILE]


def _scan_chunk(lw, l_cum, a, kd, kk, v, r, ht0, *, rev, n_groups):
    def st(x):
        return jnp.stack([x[:, p * GROUP:(p + 1) * GROUP] for p in range(n_groups)], axis=0)

    lw_s, l_s, a_s, kd_s, kk_s, v_s, r_s = (st(t) for t in (lw, l_cum, a, kd, kk, v, r))
    ltot = l_s[:, 0:1, :] if rev else l_s[:, CHUNK - 1:CHUNK, :]
    e_l = jnp.exp(l_s)
    e_nl = jnp.exp(-l_s)
    e_lx = jnp.exp(l_s - lw_s)
    e_tl = jnp.exp(ltot - l_s)
    g_tot = jnp.exp(ltot)

    t_i = lax.broadcasted_iota(jnp.int32, (1, CHUNK, GROUP), 1)
    lane = lax.broadcasted_iota(jnp.int32, (1, CHUNK, GROUP), 2)
    s_i = lane & (HEAD - 1)
    head_of_lane = lane // HEAD
    strict = (s_i > t_i) if rev else (s_i < t_i)
    incl = (s_i >= t_i) if rev else (s_i <= t_i)
    eye = (s_i == t_i).astype(F32)
    rr = lax.broadcasted_iota(jnp.int32, (GROUP, GROUP), 0)
    cc = lax.broadcasted_iota(jnp.int32, (GROUP, GROUP), 1)
    same_head = (rr // HEAD) == (cc // HEAD)

    def bd(x):
        z = jnp.zeros_like(x)
        return jnp.concatenate([jnp.where(head_of_lane == g, x, z) for g in range(GROUP // HEAD)], axis=1)

    b_s = kk_s * a_s
    ah = (-kk_s * e_lx).astype(BF16)
    rh = (r_s * e_l).astype(BF16)
    bc = (b_s * e_nl).astype(BF16)
    kc = (kd_s * e_nl).astype(BF16)
    kt = (kd_s * e_tl).astype(BF16)
    bt = (b_s * e_tl).astype(BF16)
    v_b = v_s.astype(BF16)

    sc = _bmm_nt(jnp.concatenate([ah, rh], axis=1),
                 jnp.concatenate([bd(bc), bd(kc)], axis=1))
    m_ab = jnp.where(strict, sc[:, :CHUNK, :GROUP], 0.0)
    m_ak = jnp.where(strict, sc[:, :CHUNK, GROUP:], 0.0).astype(BF16)
    n_rb = jnp.where(incl, sc[:, CHUNK:, :GROUP], 0.0).astype(BF16)
    n_rk = jnp.where(incl, sc[:, CHUNK:, GROUP:], 0.0).astype(BF16)

    s_acc = eye + m_ab
    m_pow = m_ab.astype(BF16)
    m_pow = _bmm(m_pow, bd(m_pow))
    n_steps = int(math.log2(CHUNK)) - 1
    for j in range(n_steps):
        mb = m_pow.astype(BF16)
        if j < n_steps - 1:
            both = _bmm(mb, jnp.concatenate([bd(mb), bd(s_acc.astype(BF16))], axis=2))
            m_pow = both[:, :, :GROUP]
            s_acc = s_acc + both[:, :, GROUP:]
        else:
            s_acc = s_acc + _bmm(mb, bd(s_acc.astype(BF16)))
    t_inv = s_acc.astype(BF16)

    ht0_b = ht0.astype(BF16)
    v_bd = bd(v_b)
    from_h0 = _bmm_nt(jnp.concatenate([ah, rh], axis=1), ht0_b)
    w = from_h0[:, :CHUNK] + _bmm(m_ak, v_bd)
    u_b = _bmm(t_inv, bd(w.astype(BF16))).astype(BF16)
    y = from_h0[:, CHUNK:] + _bmm(jnp.concatenate([n_rk, n_rb], axis=2),
                                  jnp.concatenate([v_bd, bd(u_b)], axis=1))
    upd = _bmm_tn(jnp.concatenate([v_b, u_b], axis=1), jnp.concatenate([kt, bt], axis=1))
    ht1 = ht0 * g_tot + jnp.where(same_head[None], upd, 0.0)
    return jnp.concatenate([y[p] for p in range(n_groups)], axis=1), ht1


def _scan_body(k_ref, v_ref, r_ref, la_ref, kk_ref, ka_ref, w0_ref, a0_ref, wd_ref, wa_ref,
               y_ref, h_ref, *, rev, n_groups):
    c = pl.program_id(0)

    @pl.when(c == 0)
    def _():
        h_ref[...] = jnp.zeros_like(h_ref)

    k = k_ref[...]
    v = v_ref[...]
    r = r_ref[...]
    la_dec, la_icl = _lora_tiles(la_ref[...])
    dec_pre = w0_ref[...] + _dot(jnp.tanh(la_dec).astype(BF16), wd_ref[...])
    icl_pre = a0_ref[...] + _dot(la_icl.astype(BF16), wa_ref[...])
    lw = -EXP_M05 * _sigmoid(dec_pre)
    a = _sigmoid(icl_pre)
    kd = k * (1.0 + (a - 1.0) * ka_ref[...])
    kkr = k * kk_ref[...]
    kk = kkr / jnp.maximum(jnp.sqrt(_head_sums(kkr * kkr)), 1e-12)

    rows = k.shape[0]
    row = lax.broadcasted_iota(jnp.int32, (rows, rows), 0)
    col = lax.broadcasted_iota(jnp.int32, (rows, rows), 1)
    in_order = (col >= row) if rev else (col <= row)
    tri = (in_order & ((row // CHUNK) == (col // CHUNK))).astype(BF16)
    lw_hi, lw_lo = _split2(lw)
    l_cum = _dot(tri, lw_hi) + _dot(tri, lw_lo)

    n_sub = rows // CHUNK
    ht = h_ref[...]
    ys = [None] * n_sub
    for s in (reversed(range(n_sub)) if rev else range(n_sub)):
        sl = slice(s * CHUNK, (s + 1) * CHUNK)
        ys[s], ht = _scan_chunk(lw[sl], l_cum[sl], a[sl], kd[sl], kk[sl], v[sl], r[sl], ht,
                                rev=rev, n_groups=n_groups)
    y_ref[...] = jnp.concatenate(ys, axis=0)
    h_ref[...] = ht


def _scan_call(za, kk, ka, w0, a0, wd, wa, *, rev, n_lat_chunks, n_ctx_chunks, d_rwkv, n_sub):
    assert n_lat_chunks % n_sub == 0 and n_ctx_chunks % n_sub == 0
    n_lat_blk, n_ctx_blk = n_lat_chunks // n_sub, n_ctx_chunks // n_sub
    n_blk = n_lat_blk + n_ctx_blk
    rows = n_sub * CHUNK
    n_groups = d_rwkv // GROUP
    lora_blk = 3 * d_rwkv // LORA_PAD
    assert CHUNK == HEAD and d_rwkv % GROUP == 0

    def blk_of(c):
        if rev:
            return n_blk - 1 - c
        return jnp.where(c < n_ctx_blk, n_lat_blk + c, c - n_ctx_blk)

    vec =pl.BlockSpec((1, d_rwkv), lambda c: (0, 0))
    mat = pl.BlockSpec((LORA_TILE, d_rwkv), lambda c: (0, 0))
    return pl.pallas_call(
        functools.partial(_scan_body, rev=rev, n_groups=n_groups),
        grid=(n_blk,),
        in_specs=[
            pl.BlockSpec((rows, d_rwkv), lambda c: (blk_of(c), 0)),
            pl.BlockSpec((rows, d_rwkv), lambda c: (blk_of(c), 1)),
            pl.BlockSpec((rows, d_rwkv), lambda c: (blk_of(c), 2)),
            pl.BlockSpec((rows, LORA_PAD), lambda c: (blk_of(c), lora_blk)),
            vec, vec, vec, vec, mat, mat,
        ],
        out_specs=pl.BlockSpec((rows, d_rwkv), lambda c: (blk_of(c), 0)),
        out_shape=jax.ShapeDtypeStruct((n_blk * rows, d_rwkv), F32),
        scratch_shapes=[pltpu.VMEM((n_groups, GROUP, GROUP), F32)],
        compiler_params=_cparams("arbitrary"),
        name="wkv_scan_rev" if rev else "wkv_scan_fwd",
    )(za, za, za, za, kk, ka, w0, a0, wd, wa)


MOD_ROWS = 8


def _mod_body(cc_ref, w_ref, b_ref, o_ref):
    cc = cc_ref[...]
    s = cc * _sigmoid(cc)
    s_hi, s_lo = _split2(s)
    w = w_ref[...].astype(BF16)
    o_ref[...] = _dot(s_hi, w) + _dot(s_lo, w) + b_ref[...]


def _mod_call(cc, w_ada, b_ada, *, n_cols, tn):
    d, n = w_ada.shape[0], n_cols
    assert n % tn == 0 and n <= w_ada.shape[1]
    return pl.pallas_call(
        _mod_body,
        grid=(n // tn,),
        in_specs=[pl.BlockSpec((MOD_ROWS, d), lambda j: (0, 0)),
                  pl.BlockSpec((d, tn), lambda j: (0, j)),
                  pl.BlockSpec((1, tn), lambda j: (0, j))],
        out_specs=pl.BlockSpec((MOD_ROWS, tn), lambda j: (0, j)),
        out_shape=jax.ShapeDtypeStruct((MOD_ROWS, n), F32),
        compiler_params=_cparams("arbitrary"),
        name="adaln_mod",
    )(cc, w_ada, b_ada)


def _norm_mod(xf, gain, shift, scale):
    y = xf * lax.rsqrt(jnp.mean(xf * xf, axis=-1, keepdims=True) + NORM_EPS) * gain
    return y * (1.0 + scale) + shift


def _norm_ctx_body(x_ref, ctx_ref, g_ref, sh_ref, sc_ref, o_ref, *, n_lat_blocks):
    i = pl.program_id(0)

    @pl.when(i < n_lat_blocks)
    def _():
        o_ref[...] = _norm_mod(x_ref[...], g_ref[...], sh_ref[0:1, :], sc_ref[0:1, :]).astype(o_ref.dtype)

    @pl.when(i >= n_lat_blocks)
    def _():
        o_ref[...] = _norm_mod(ctx_ref[...], g_ref[...], sh_ref[1:2, :], sc_ref[1:2, :]).astype(o_ref.dtype)


def _norm_call(x, ctx, gain, mod, shift_blk, scale_blk, *, tm):
    t, d = x.shape
    n_lat = t // tm
    vec = pl.BlockSpec((1, d), lambda i: (0, 0))
    sh = pl.BlockSpec((MOD_ROWS, d), lambda i: (0, shift_blk))
    sc = pl.BlockSpec((MOD_ROWS, d), lambda i: (0, scale_blk))
    tc = ctx.shape[0]
    n_ctx = tc // tm
    return pl.pallas_call(
        functools.partial(_norm_ctx_body, n_lat_blocks=n_lat), grid=(n_lat + n_ctx,),
        in_specs=[pl.BlockSpec((tm, d), lambda i: (jnp.minimum(i, n_lat - 1), 0)),
                  pl.BlockSpec((tm, d), lambda i: (jnp.maximum(i - n_lat, 0), 0)), vec, sh, sc],
        out_specs=pl.BlockSpec((tm, d), lambda i: (i, 0)),
        out_shape=jax.ShapeDtypeStruct((t + tc, d), BF16),
        compiler_params=_cparams("arbitrary"), name="norm_mod_ctx",
    )(x, ctx, gain, mod, mod)


def _neighbours(z, row0, t_lat, ctx_len):
    tm = z.shape[0]
    g = row0 + lax.broadcasted_iota(jnp.int32, (tm, 1), 0)
    is_ctx = g >= t_lat
    pos = jnp.where(is_ctx, g - t_lat, g & (GRID_W - 1))
    last = jnp.where(is_ctx, ctx_len - 1, GRID_W - 1)
    prev = jnp.where(pos == 0, 0.0, pltpu.roll(z, 1, 0))
    nxt = jnp.where(pos == last, 0.0, pltpu.roll(z, tm - 1, 0))
    return prev, nxt


def _dot_nt(a, b):
    return lax.dot_general(a, b, (((1,), (1,)), ((), ())), preferred_element_type=F32)


def _rows_spec(rows, width, row_of):
    return pl.BlockSpec((pl.Element(rows), pl.Element(width)),
                        lambda i, j: (pl.multiple_of(row_of(j), SUBLANES), 0))


def _inproj_shift_body(h_ref, wt_ref, mu_ref, wada_ref, cb_ref, bada_ref, o_ref, mod_ref, silu_ref,
                       *, t_lat, ctx_len, n_sub):
    @pl.when((pl.program_id(0) == 0) & (pl.program_id(1) == 0))
    def _():
        cb = cb_ref[...]
        silu_ref[...] = cb * _sigmoid(cb)

    w_ada = wada_ref[...]
    n_acc = 16
    prod = (w_ada * silu_ref[...]).reshape(n_acc, w_ada.shape[0] // (n_acc * SUBLANES), SUBLANES, w_ada.shape[1])
    part = jnp.sum(jnp.sum(prod, axis=1), axis=0)
    mod_ref[...] = jnp.broadcast_to(jnp.sum(part, axis=0, keepdims=True) + bada_ref[...], mod_ref.shape)

    w = wt_ref[...].astype(BF16)
    tm = h_ref.shape[0]
    ts = tm // n_sub
    for s in range(n_sub):
        rows = slice(s * ts, (s + 1) * ts)
        z = _dot_nt(h_ref[rows, :], w)
        prev, nxt = _neighbours(z, pl.program_id(0) * tm + s * ts, t_lat, ctx_len)
        o_ref[rows, :] = z + mu_ref[...] * (0.5 * (prev + nxt) - z)


def _inproj_shift_call(h_all, w_in_t, mu, segments, w_ada, c_bcast, b_ada, *, t_lat, ctx_len, tm, tn, n_sub,
                       mod_col0, mod_cols, mod_tn):
    t, d = h_all.shape
    assert tm % (n_sub * GRID_W) == 0
    assert all(r % SUBLANES == 0 and n % tn == 0 and r + n <= w_in_t.shape[0] for r, n in segments)
    tile0 = [sum(n for _, n in segments[:s]) // tn for s in range(len(segments) + 1)]
    n_tiles = tile0[-1]

    def row_of(j):
        return sum(jnp.where((j >= lo) & (j < hi), r + tn * (j - lo), 0)
                   for (r, _), lo, hi in zip(segments, tile0[:-1], tile0[1:]))

    d_model = w_ada.shape[0]
    n_mod = mod_cols // mod_tn
    assert mod_cols % mod_tn == 0 and mod_col0 % mod_tn == 0 and n_mod <= (t // tm) * n_tiles
    assert c_bcast.shape == (d_model, mod_tn)

    def mod_tile(i, j):
        return jnp.minimum(i * n_tiles + j, n_mod - 1)

    return pl.pallas_call(
        functools.partial(_inproj_shift_body, t_lat=t_lat, ctx_len=ctx_len, n_sub=n_sub),
        grid=(t // tm, n_tiles),
        in_specs=[pl.BlockSpec((tm, d), lambda i, j: (i, 0), pipeline_mode=pl.Buffered(1)),
                  _rows_spec(tn, d, row_of),
                  pl.BlockSpec((1, tn), lambda i, j: (0, j)),
                  pl.BlockSpec((d_model, mod_tn), lambda i, j: (0, mod_col0 // mod_tn + mod_tile(i, j))),
                  pl.BlockSpec((d_model, mod_tn), lambda i, j: (0, 0)),
                  pl.BlockSpec((1, mod_tn), lambda i, j: (0, mod_col0 // mod_tn + mod_tile(i, j)))],
        out_specs=[pl.BlockSpec((tm, tn), lambda i, j: (i, j)),
                   pl.BlockSpec((MOD_ROWS, mod_tn), lambda i, j: (0, mod_tile(i, j)))],
        out_shape=[jax.ShapeDtypeStruct((t, n_tiles * tn), F32), jax.ShapeDtypeStruct((MOD_ROWS, mod_cols), F32)],
        scratch_shapes=[pltpu.VMEM((d_model, mod_tn), F32)],
        compiler_params=_cparams("arbitrary", "arbitrary"), name="inproj_shift",
    )(h_all, w_in_t, mu, w_ada, c_bcast, b_ada)


def _inproj_conv_body(h_ref, wb_ref, wc_ref, wx_ref, cw_ref, wcast_ref, o_ref, wcast_o_ref, *, t_lat):
    wcast_o_ref[...] = wcast_ref[...].astype(BF16)
    h = h_ref[...]
    u = _dot_nt(h, wc_ref[...].astype(BF16)) * _dot_nt(h, wx_ref[...].astype(BF16))
    prev, nxt = _neighbours(u, pl.program_id(0) * u.shape[0], t_lat, 1)
    cw = cw_ref[...]
    conv = cw[0:1, :] * prev + cw[1:2, :] * u + cw[2:3, :] * nxt
    o_ref[...] = (_dot_nt(h, wb_ref[...].astype(BF16)) * conv).astype(o_ref.dtype)


def _inproj_conv_call(h_all, w_in_t, conv_w, w_cast, *, row0, t_lat, tm, tc):
    d = h_all.shape[1]
    d_conv = conv_w.shape[1]
    assert d_conv % tc == 0 and row0 % SUBLANES == 0 and row0 + 3 * d_conv <= w_in_t.shape[0]
    cw = jnp.pad(conv_w, ((0, MOD_ROWS - 3), (0, 0)))
    n_j = d_conv // tc
    c_in, c_out, c_shape = _cast_rows_specs(w_cast, (t_lat // tm) * n_j, lambda i, j: i * n_j + j)
    return pl.pallas_call(
        functools.partial(_inproj_conv_body, t_lat=t_lat),
        grid=(t_lat // tm, n_j),
        in_specs=[pl.BlockSpec((tm, d), lambda i, j: (i, 0)),
                  _rows_spec(tc, d, lambda j: row0 + tc * j),
                  _rows_spec(tc, d, lambda j: row0 + d_conv + tc * j),
                  _rows_spec(tc, d, lambda j: row0 + 2 * d_conv + tc * j),
                  pl.BlockSpec((MOD_ROWS, tc), lambda i, j: (0, j)),
                  c_in],
        out_specs=[pl.BlockSpec((tm, tc), lambda i, j: (i, j)), c_out],
        out_shape=[jax.ShapeDtypeStruct((t_lat, d_conv), BF16), c_shape],
        compiler_params=_cparams("arbitrary", "arbitrary"), name="inproj_conv",
    )(h_all, w_in_t, w_in_t, w_in_t, cw, w_cast)


def _readout_body(y0_ref, y1_ref, k_ref, v_ref, r_ref, la_ref, gl_ref, ka_ref, rk_ref, a0_ref,
                  wa0_ref, wa1_ref, wg_ref, lw_ref, lb_ref, o_ref):
    la = _lora_tiles(la_ref[...])[1].astype(BF16)
    a0 = a0_ref[...]
    a_sum = _sigmoid(a0[0:1, :] + _dot(la, wa0_ref[...])) + _sigmoid(a0[1:2, :] + _dot(la, wa1_ref[...]))
    coef = r_ref[...] * rk_ref[...] * k_ref[...] * (2.0 + (a_sum - 2.0) * ka_ref[...])
    y = y0_ref[...] + y1_ref[...]
    g = _dot(_sigmoid(gl_ref[...]).astype(BF16), wg_ref[...])

    yc = y - _head_sums(y) * (1.0 / HEAD)
    var = _head_sums(yc * yc) * (1.0 / HEAD)
    yn = yc * lax.rsqrt(var + LNX_EPS) * lw_ref[...] + lb_ref[...]
    bonus = _head_sums(coef) * v_ref[...]
    o_ref[...] = ((yn + bonus) * g).astype(o_ref.dtype)


def _readout_call(y0, y1, za, k_a, r_k, a0, wa0, wa1, w_gate, lnx_w, lnx_b, *, t_lat, d_rwkv, tm):
    row = lambda jb: pl.BlockSpec((tm, d_rwkv), lambda i: (i, jb))
    vec = pl.BlockSpec((1, d_rwkv), lambda i: (0, 0))
    mat = pl.BlockSpec((LORA_TILE, d_rwkv), lambda i: (0, 0))
    lora_blk = 3 * d_rwkv // LORA_PAD
    assert GATE_LORA == LORA_PAD
    return pl.pallas_call(
        _readout_body, grid=(t_lat // tm,),
        in_specs=[row(0), row(0), row(0), row(1), row(2),
                  pl.BlockSpec((tm, LORA_PAD), lambda i: (i, lora_blk)),
                  pl.BlockSpec((tm, GATE_LORA), lambda i: (i, lora_blk + 1)),
                  vec, vec, pl.BlockSpec((2, d_rwkv), lambda i: (0, 0)),
                  mat, mat, pl.BlockSpec((GATE_LORA, d_rwkv), lambda i: (0, 0)), vec, vec],
        out_specs=pl.BlockSpec((tm, d_rwkv), lambda i: (i, 0)),
        out_shape=jax.ShapeDtypeStruct((t_lat, d_rwkv), BF16),
        compiler_params=_cparams("arbitrary"), name="rwkv_readout",
    )(y0, y1, za, za, za, za, za, k_a, r_k, a0, wa0, wa1, w_gate, lnx_w, lnx_b)


def _outproj_body(ya_ref, yb_ref, wa_ref, wb_ref, x_ref, g_ref, n2_ref, sh_ref, sc_ref, o_ref, h_ref, rows_ref):
    j = pl.program_id(1)
    n_tiles, _, tn = rows_ref.shape
    acc = _dot(ya_ref[...], wa_ref[...]) + _dot(yb_ref[...], wb_ref[...])
    x1 = x_ref[...] + g_ref[0:1, :] * acc
    o_ref[...] = x1
    rows_ref[j] = x1

    @pl.when(j == n_tiles - 1)
    def _():
        ss = sum(jnp.sum(rows_ref[s] * rows_ref[s], axis=-1, keepdims=True) for s in range(n_tiles))
        rs = lax.rsqrt(ss * (1.0 / (n_tiles * tn)) + NORM_EPS)
        for s in range(n_tiles):
            sl = slice(s * tn, (s + 1) * tn)
            y = rows_ref[s] * rs * n2_ref[:, sl]
            h_ref[:, sl] = (y * (1.0 + sc_ref[0:1, sl]) + sh_ref[0:1, sl]).astype(h_ref.dtype)


def _outproj_call(ya, yb, w, x, gate, gain2, shift, scale, *, tm, tn):
    (gate_arr, gate_col), (shift_arr, shift_col), (scale_arr, scale_col) = gate, shift, scale
    t, da = ya.shape
    db = yb.shape[1]
    n = w.shape[1]
    assert da == db
    return pl.pallas_call(
        _outproj_body, grid=(t // tm, n // tn),
        in_specs=[pl.BlockSpec((tm, da), lambda i, j: (i, 0)),
                  pl.BlockSpec((tm, db), lambda i, j: (i, 0)),
                  pl.BlockSpec((da, tn), lambda i, j: (0, j)),
                  pl.BlockSpec((db, tn), lambda i, j: (1, j)),
                  pl.BlockSpec((tm, tn), lambda i, j: (i, j)),
                  pl.BlockSpec((MOD_ROWS, tn), lambda i, j: (0, gate_col // tn + j)),
                  pl.BlockSpec((1, n), lambda i, j: (0, 0)),
                  pl.BlockSpec((MOD_ROWS, n), lambda i, j: (0, shift_col // n)),
                  pl.BlockSpec((MOD_ROWS, n), lambda i, j: (0, scale_col // n))],
        out_specs=[pl.BlockSpec((tm, tn), lambda i, j: (i, j)),
                   pl.BlockSpec((tm, n), lambda i, j: (i, 0))],
        out_shape=[jax.ShapeDtypeStruct((t, n), F32), jax.ShapeDtypeStruct((t, n), BF16)],
        scratch_shapes=[pltpu.VMEM((n // tn, tm, tn), F32)],
        compiler_params=_cparams("arbitrary", "arbitrary"), name="outproj_residual_norm",
    )(ya, yb, w, w, x, gate_arr, gain2, shift_arr, scale_arr)


def _cast_rows_specs(w, n_steps, step_of):
    rows, cols = w.shape
    assert rows % n_steps == 0 and (rows // n_steps) % BF16_SUBLANES == 0
    blk = (rows // n_steps, cols)
    return (pl.BlockSpec(blk, lambda i, j: (step_of(i, j), 0)), pl.BlockSpec(blk, lambda i, j: (step_of(i, j), 0)),
            jax.ShapeDtypeStruct(w.shape, BF16))


def _ffn_up_body(h_ref, wg_ref, wu_ref, wcast_ref, o_ref, wcast_o_ref):
    h = h_ref[...]
    g = _dot(h, wg_ref[...].astype(BF16))
    u = _dot(h, wu_ref[...].astype(BF16))
    o_ref[...] = (g * _sigmoid(g) * u).astype(o_ref.dtype)
    wcast_o_ref[...] = wcast_ref[...].astype(BF16)


def _ffn_up_call(h, wg, wu, w_cast, *, tm, tn):
    t, d = h.shape
    n = wg.shape[1]
    n_j = n // tn
    c_in, c_out, c_shape = _cast_rows_specs(w_cast, (t // tm) * n_j, lambda i, j: i * n_j + j)
    return pl.pallas_call(
        _ffn_up_body, grid=(t // tm, n_j),
        in_specs=[pl.BlockSpec((tm, d), lambda i, j: (i, 0), pipeline_mode=pl.Buffered(1)),
                  pl.BlockSpec((d, tn), lambda i, j: (0, j)),
                  pl.BlockSpec((d, tn), lambda i, j: (0, j)),
                  c_in],
        out_specs=[pl.BlockSpec((tm, tn), lambda i, j: (i, j)), c_out],
        out_shape=[jax.ShapeDtypeStruct((t, n), BF16), c_shape],
        compiler_params=_cparams("arbitrary", "arbitrary"), name="ffn_gate_up",
    )(h, wg, wu, w_cast)


def _ffn_down_body(a_ref, w_ref, x_ref, g_ref, o_ref):
    o_ref[...] = x_ref[...] + g_ref[0:1, :] * _dot(a_ref[...], w_ref[...])


def _ffn_down_call(act, w, x, gate, *, tm, tn):
    mod, gate_col = gate
    gate_blk0 = gate_col // tn
    t, kdim = act.shape
    d = w.shape[1]
    return pl.pallas_call(
        _ffn_down_body, grid=(t // tm, d // tn),
        in_specs=[pl.BlockSpec((tm, kdim), lambda i, j: (i, 0)),
                  pl.BlockSpec((kdim, tn), lambda i, j: (0, j)),
                  pl.BlockSpec((tm, tn), lambda i, j: (i, j)),
                  pl.BlockSpec((MOD_ROWS, tn), lambda i, j: (0, gate_blk0 + j))],
        out_specs=pl.BlockSpec((tm, tn), lambda i, j: (i, j)),
        out_shape=jax.ShapeDtypeStruct((t, d), F32),
        compiler_params=_cparams("arbitrary", "arbitrary"), name="ffn_down_residual",
    )(act, w, x, mod)


def _final_norm_body(x_ref, g_ref, o_ref):
    x = x_ref[...]
    o_ref[...] = x * lax.rsqrt(jnp.mean(x * x, axis=-1, keepdims=True) + NORM_EPS) * g_ref[...]


def _final_norm_call(x, gain, *, tm):
    t, d = x.shape
    return pl.pallas_call(
        _final_norm_body, grid=(t // tm,),
        in_specs=[pl.BlockSpec((tm, d), lambda i: (i, 0)), pl.BlockSpec((1, d), lambda i: (0, 0))],
        out_specs=pl.BlockSpec((tm, d), lambda i: (i, 0)),
        out_shape=jax.ShapeDtypeStruct((t, d), F32),
        compiler_params=_cparams("arbitrary"), name="final_norm",
    )(x, gain)


def _tiles():
    return dict(mod_tn=1024, norm_tm=256, shift_tm=2816, shift_tn=256, shift_sub=4, conv_tm=1024, conv_tc=256,
                side_mod_tn=256, scan_sub=4, readout_tm=256, out_tm=512, out_tn=1024, up_tm=2048, up_tn=256, down_tm=512, down_tn=512)


def kernel(x, c, ctx, c_ctx, w_ada, b_ada, norm1, w_in, mu_shift, k_k, k_a, r_k, w0, w_decay_up, a0, w_iclr_up, w_gate_up, lnx_w, lnx_b, conv_w, w_out, norm2, w_ffn_gate, w_ffn_up, w_ffn_down, norm_f):
    assert x.shape[0] == 1 and w_ada.shape[0] == 1, "single batch element, single layer"
    tl = _tiles()
    t_lat, d = x.shape[1], x.shape[2]
    ctx_len = ctx.shape[1]
    d_rwkv = k_k.shape[1]
    d_conv = conv_w.shape[2]
    d_ff = w_ffn_gate.shape[2]
    assert t_lat % GRID_W == 0 and ctx_len % CHUNK == 0 and d_rwkv % SEG_W == 0
    x2d, ctx2d = x[0], ctx[0]

    off_lora = 2 * d_rwkv
    off_r = off_lora + DECAY_LORA + ICLR_LORA
    off_gl = off_r + d_rwkv
    rwkv_cols = off_gl + GATE_LORA
    wi_t = jnp.swapaxes(w_in, 1, 2)[0]
    segments = [(0, off_lora), (off_r, d_rwkv), (off_lora, LORA_PAD), (off_gl, GATE_LORA)]
    ms = mu_shift[0]
    mu = jnp.concatenate([ms[r0:r0 + n] for r0, n in segments])[None, :]
    wd = [jnp.zeros((LORA_TILE, d_rwkv), BF16).at[:DECAY_LORA].set(w_decay_up[0, i].astype(BF16)) for i in range(2)]
    wa = [jnp.zeros((LORA_TILE, d_rwkv), BF16).at[:ICLR_LORA].set(w_iclr_up[0, i].astype(BF16)) for i in range(2)]

    cc = jnp.zeros((MOD_ROWS, d), F32).at[0].set(c[0]).at[1].set(c_ctx)
    mod1 = _mod_call(cc, w_ada[0], b_ada, n_cols=2 * d, tn=tl["mod_tn"])
    c_bcast = jnp.broadcast_to(c[0][:, None], (d, tl["side_mod_tn"]))

    h_all = _norm_call(x2d, ctx2d, norm1, mod1, 0, 1, tm=tl["norm_tm"])
    za, mod2 = _inproj_shift_call(h_all, wi_t, mu, segments, w_ada[0], c_bcast, b_ada, t_lat=t_lat, ctx_len=ctx_len,
                                  tm=tl["shift_tm"], tn=tl["shift_tn"], n_sub=tl["shift_sub"],
                                  mod_col0=2 * d, mod_cols=4 * d, mod_tn=tl["side_mod_tn"])
    y_conv, wo = _inproj_conv_call(h_all, wi_t, conv_w[0], w_out[0], row0=rwkv_cols, t_lat=t_lat,
                                   tm=tl["conv_tm"], tc=tl["conv_tc"])
    n_lat_chunks, n_ctx_chunks = t_lat // CHUNK, ctx_len // CHUNK
    ys = [_scan_call(za, k_k, k_a, w0[0, i:i + 1], a0[0, i:i + 1], wd[i], wa[i], rev=(i == 1),
                     n_lat_chunks=n_lat_chunks, n_ctx_chunks=n_ctx_chunks, d_rwkv=d_rwkv,
                     n_sub=tl["scan_sub"]) for i in range(2)]
    y_rwkv = _readout_call(ys[0], ys[1], za, k_a, r_k.reshape(1, d_rwkv), a0[0], wa[0], wa[1],
                           w_gate_up[0].astype(BF16), lnx_w, lnx_b, t_lat=t_lat, d_rwkv=d_rwkv,
                           tm=tl["readout_tm"])
    x1, h2 = _outproj_call(y_rwkv, y_conv, wo, x2d, (mod2, 0), norm2, (mod2, d), (mod2, 2 * d),
                           tm=tl["out_tm"], tn=tl["out_tn"])

    act, wdn = _ffn_up_call(h2, w_ffn_gate[0], w_ffn_up[0], w_ffn_down[0], tm=tl["up_tm"], tn=tl["up_tn"])
    x2 = _ffn_down_call(act, wdn, x1, (mod2, 3 * d), tm=tl["down_tm"], tn=tl["down_tn"])
    out = _final_norm_call(x2, norm_f[None, :], tm=tl["norm_tm"])
    return out[None]
```

```python
import functools
import math

import jax
import jax.numpy as jnp
from jax import lax
from jax.experimental import pallas as pl
from jax.experimental.pallas import tpu as pltpu

F32 = jnp.float32
BF16 = jnp.bfloat16

SUBLANES = 8
BF16_SUBLANES = 16
MXU_WIDTH = 256
SEG_W = MXU_WIDTH
VMEM_LIMIT_BYTES = 56 * 1024 * 1024

HEAD = 64
GROUP = 128
CHUNK = 64
GRID_W = 64
DECAY_LORA = 96
ICLR_LORA = 96
GATE_LORA = 256
LORA_TILE = 128
LORA_PAD = 2 * LORA_TILE
NORM_EPS = 1e-6
LNX_EPS = 64e-5
EXP_M05 = math.exp(-0.5)


def _cparams(*sem):
    return pltpu.CompilerParams(dimension_semantics=sem, vmem_limit_bytes=VMEM_LIMIT_BYTES)


def _dot(a, b):
    return jnp.dot(a, b, preferred_element_type=F32)


def _bmm(a, b):
    return lax.dot_general(a, b, (((2,), (1,)), ((0,), (0,))), preferred_element_type=F32)


def _bmm_nt(a, b):
    return lax.dot_general(a, b, (((2,), (2,)), ((0,), (0,))), preferred_element_type=F32)


def _bmm_tn(a, b):
    return lax.dot_general(a, b, (((1,), (1,)), ((0,), (0,))), preferred_element_type=F32)


def _split2(x):
    hi = x.astype(BF16)
    lo = (x - hi.astype(F32)).astype(BF16)
    return hi, lo


def _sigmoid(x):
    return 1.0 / (1.0 + jnp.exp(-x))


def _head_sums(x):
    rows, d = x.shape
    n_seg = d // SEG_W
    xs = jnp.concatenate([x[:, s * SEG_W:(s + 1) * SEG_W] for s in range(n_seg)], axis=0)
    rr = lax.broadcasted_iota(jnp.int32, (SEG_W, SEG_W), 0) // HEAD
    cc = lax.broadcasted_iota(jnp.int32, (SEG_W, SEG_W), 1) // HEAD
    ones_seg = (rr == cc).astype(BF16)
    hi, lo = _split2(xs)
    ss = _dot(hi, ones_seg) + _dot(lo, ones_seg)
    return jnp.concatenate([ss[s * rows:(s + 1) * rows] for s in range(n_seg)], axis=1)


def _lora_tiles(la):
    return la[:, :LORA_TILE], la[:, DECAY_LORA:DECAY_LORA + LORA_TILE]


def _scan_chunk(lw, l_cum, a, kd, kk, v, r, ht0, *, rev, n_groups):
    def st(x):
        return jnp.stack([x[:, p * GROUP:(p + 1) * GROUP] for p in range(n_groups)], axis=0)

    lw_s, l_s, a_s, kd_s, kk_s, v_s, r_s = (st(t) for t in (lw, l_cum, a, kd, kk, v, r))
    ltot = l_s[:, 0:1, :] if rev else l_s[:, CHUNK - 1:CHUNK, :]
    e_l = jnp.exp(l_s)
    e_nl = jnp.exp(-l_s)
    e_lx = jnp.exp(l_s - lw_s)
    e_tl = jnp.exp(ltot - l_s)
    g_tot = jnp.exp(ltot)

    t_i = lax.broadcasted_iota(jnp.int32, (1, CHUNK, GROUP), 1)
    lane = lax.broadcasted_iota(jnp.int32, (1, CHUNK, GROUP), 2)
    s_i = lane & (HEAD - 1)
    head_of_lane = lane // HEAD
    strict = (s_i > t_i) if rev else (s_i < t_i)
    incl = (s_i >= t_i) if rev else (s_i <= t_i)
    eye = (s_i == t_i).astype(F32)
    rr = lax.broadcasted_iota(jnp.int32, (GROUP, GROUP), 0)
    cc = lax.broadcasted_iota(jnp.int32, (GROUP, GROUP), 1)
    same_head = (rr // HEAD) == (cc // HEAD)

    def bd(x):
        z = jnp.zeros_like(x)
        return jnp.concatenate([jnp.where(head_of_lane == g, x, z) for g in range(GROUP // HEAD)], axis=1)

    b_s = kk_s * a_s
    ah = (-kk_s * e_lx).astype(BF16)
    rh = (r_s * e_l).astype(BF16)
    bc = (b_s * e_nl).astype(BF16)
    kc = (kd_s * e_nl).astype(BF16)
    kt = (kd_s * e_tl).astype(BF16)
    bt = (b_s * e_tl).astype(BF16)
    v_b = v_s.astype(BF16)

    sc = _bmm_nt(jnp.concatenate([ah, rh], axis=1),
                 jnp.concatenate([bd(bc), bd(kc)], axis=1))
    m_ab = jnp.where(strict, sc[:, :CHUNK, :GROUP], 0.0)
    m_ak = jnp.where(strict, sc[:, :CHUNK, GROUP:], 0.0).astype(BF16)
    n_rb = jnp.where(incl, sc[:, CHUNK:, :GROUP], 0.0).astype(BF16)
    n_rk = jnp.where(incl, sc[:, CHUNK:, GROUP:], 0.0).astype(BF16)

    s_acc = eye + m_ab
    m_pow = m_ab.astype(BF16)
    m_pow = _bmm(m_pow, bd(m_pow))
    n_steps = int(math.log2(CHUNK)) - 1
    for j in range(n_steps):
        mb = m_pow.astype(BF16)
        if j < n_steps - 1:
            both = _bmm(mb, jnp.concatenate([bd(mb), bd(s_acc.astype(BF16))], axis=2))
            m_pow = both[:, :, :GROUP]
            s_acc = s_acc + both[:, :, GROUP:]
        else:
            s_acc = s_acc + _bmm(mb, bd(s_acc.astype(BF16)))
    t_inv = s_acc.astype(BF16)

    ht0_b = ht0.astype(BF16)
    v_bd = bd(v_b)
    from_h0 = _bmm_nt(jnp.concatenate([ah, rh], axis=1), ht0_b)
    w = from_h0[:, :CHUNK] + _bmm(m_ak, v_bd)
    u_b = _bmm(t_inv, bd(w.astype(BF16))).astype(BF16)
    y = from_h0[:, CHUNK:] + _bmm(jnp.concatenate([n_rk, n_rb], axis=2),
                                  jnp.concatenate([v_bd, bd(u_b)], axis=1))
    upd = _bmm_tn(jnp.concatenate([v_b, u_b], axis=1), jnp.concatenate([kt, bt], axis=1))
    ht1 = ht0 * g_tot + jnp.where(same_head[None], upd, 0.0)
    return jnp.concatenate([y[p] for p in range(n_groups)], axis=1), ht1


def _scan_body(k_ref, v_ref, r_ref, la_ref, kk_ref, ka_ref, w0_ref, a0_ref, wd_ref, wa_ref,
               y_ref, h_ref, *, rev, n_groups):
    c = pl.program_id(0)

    @pl.when(c == 0)
    def _():
        h_ref[...] = jnp.zeros_like(h_ref)

    k = k_ref[...]
    v = v_ref[...]
    r = r_ref[...]
    la_dec, la_icl = _lora_tiles(la_ref[...])
    dec_pre = w0_ref[...] + _dot(jnp.tanh(la_dec).astype(BF16), wd_ref[...])
    icl_pre = a0_ref[...] + _dot(la_icl.astype(BF16), wa_ref[...])
    lw = -EXP_M05 * _sigmoid(dec_pre)
    a = _sigmoid(icl_pre)
    kd = k * (1.0 + (a - 1.0) * ka_ref[...])
    kkr = k * kk_ref[...]
    kk = kkr / jnp.maximum(jnp.sqrt(_head_sums(kkr * kkr)), 1e-12)

    rows = k.shape[0]
    row = lax.broadcasted_iota(jnp.int32, (rows, rows), 0)
    col = lax.broadcasted_iota(jnp.int32, (rows, rows), 1)
    in_order = (col >= row) if rev else (col <= row)
    tri = (in_order & ((row // CHUNK) == (col // CHUNK))).astype(BF16)
    lw_hi, lw_lo = _split2(lw)
    l_cum = _dot(tri, lw_hi) + _dot(tri, lw_lo)

    n_sub = rows // CHUNK
    ht = h_ref[...]
    ys = [None] * n_sub
    for s in (reversed(range(n_sub)) if rev else range(n_sub)):
        sl = slice(s * CHUNK, (s + 1) * CHUNK)
        ys[s], ht = _scan_chunk(lw[sl], l_cum[sl], a[sl], kd[sl], kk[sl], v[sl], r[sl], ht,
                                rev=rev, n_groups=n_groups)
    y_ref[...] = jnp.concatenate(ys, axis=0)
    h_ref[...] = ht


def _scan_call(za, kk, ka, w0, a0, wd, wa, *, rev, n_lat_chunks, n_ctx_chunks, d_rwkv, n_sub):
    assert n_lat_chunks % n_sub == 0 and n_ctx_chunks % n_sub == 0
    n_lat_blk, n_ctx_blk = n_lat_chunks // n_sub, n_ctx_chunks // n_sub
    n_blk = n_lat_blk + n_ctx_blk
    rows = n_sub * CHUNK
    n_groups = d_rwkv // GROUP
    lora_blk = 3 * d_rwkv // LORA_PAD
    assert CHUNK == HEAD and d_rwkv % GROUP == 0

    def blk_of(c):
        if rev:
            return n_blk - 1 - c
        return jnp.where(c < n_ctx_blk, n_lat_blk + c, c - n_ctx_blk)

    vec =pl.BlockSpec((1, d_rwkv), lambda c: (0, 0))
    mat = pl.BlockSpec((LORA_TILE, d_rwkv), lambda c: (0, 0))
    return pl.pallas_call(
        functools.partial(_scan_body, rev=rev, n_groups=n_groups),
        grid=(n_blk,),
        in_specs=[
            pl.BlockSpec((rows, d_rwkv), lambda c: (blk_of(c), 0)),
            pl.BlockSpec((rows, d_rwkv), lambda c: (blk_of(c), 1)),
            pl.BlockSpec((rows, d_rwkv), lambda c: (blk_of(c), 2)),
            pl.BlockSpec((rows, LORA_PAD), lambda c: (blk_of(c), lora_blk)),
            vec, vec, vec, vec, mat, mat,
        ],
        out_specs=pl.BlockSpec((rows, d_rwkv), lambda c: (blk_of(c), 0)),
        out_shape=jax.ShapeDtypeStruct((n_blk * rows, d_rwkv), F32),
        scratch_shapes=[pltpu.VMEM((n_groups, GROUP, GROUP), F32)],
        compiler_params=_cparams("arbitrary"),
        name="wkv_scan_rev" if rev else "wkv_scan_fwd",
    )(za, za, za, za, kk, ka, w0, a0, wd, wa)


MOD_ROWS = 8


def _mod_body(cc_ref, w_ref, b_ref, o_ref):
    cc = cc_ref[...]
    s = cc * _sigmoid(cc)
    s_hi, s_lo = _split2(s)
    w = w_ref[...].astype(BF16)
    o_ref[...] = _dot(s_hi, w) + _dot(s_lo, w) + b_ref[...]


def _mod_call(cc, w_ada, b_ada, *, n_cols, tn):
    d, n = w_ada.shape[0], n_cols
    assert n % tn == 0 and n <= w_ada.shape[1]
    return pl.pallas_call(
        _mod_body,
        grid=(n // tn,),
        in_specs=[pl.BlockSpec((MOD_ROWS, d), lambda j: (0, 0)),
                  pl.BlockSpec((d, tn), lambda j: (0, j)),
                  pl.BlockSpec((1, tn), lambda j: (0, j))],
        out_specs=pl.BlockSpec((MOD_ROWS, tn), lambda j: (0, j)),
        out_shape=jax.ShapeDtypeStruct((MOD_ROWS, n), F32),
        compiler_params=_cparams("arbitrary"),
        name="adaln_mod",
    )(cc, w_ada, b_ada)


def _norm_mod(xf, gain, shift, scale):
    y = xf * lax.rsqrt(jnp.mean(xf * xf, axis=-1, keepdims=True) + NORM_EPS) * gain
    return y * (1.0 + scale) + shift


def _norm_ctx_body(x_ref, ctx_ref, g_ref, sh_ref, sc_ref, o_ref, *, n_lat_blocks):
    i = pl.program_id(0)

    @pl.when(i < n_lat_blocks)
    def _():
        o_ref[...] = _norm_mod(x_ref[...], g_ref[...], sh_ref[0:1, :], sc_ref[0:1, :]).astype(o_ref.dtype)

    @pl.when(i >= n_lat_blocks)
    def _():
        o_ref[...] = _norm_mod(ctx_ref[...], g_ref[...], sh_ref[1:2, :], sc_ref[1:2, :]).astype(o_ref.dtype)


def _norm_call(x, ctx, gain, mod, shift_blk, scale_blk, *, tm):
    t, d = x.shape
    n_lat = t // tm
    vec = pl.BlockSpec((1, d), lambda i: (0, 0))
    sh = pl.BlockSpec((MOD_ROWS, d), lambda i: (0, shift_blk))
    sc = pl.BlockSpec((MOD_ROWS, d), lambda i: (0, scale_blk))
    tc = ctx.shape[0]
    n_ctx = tc // tm
    return pl.pallas_call(
        functools.partial(_norm_ctx_body, n_lat_blocks=n_lat), grid=(n_lat + n_ctx,),
        in_specs=[pl.BlockSpec((tm, d), lambda i: (jnp.minimum(i, n_lat - 1), 0)),
                  pl.BlockSpec((tm, d), lambda i: (jnp.maximum(i - n_lat, 0), 0)), vec, sh, sc],
        out_specs=pl.BlockSpec((tm, d), lambda i: (i, 0)),
        out_shape=jax.ShapeDtypeStruct((t + tc, d), BF16),
        compiler_params=_cparams("arbitrary"), name="norm_mod_ctx",
    )(x, ctx, gain, mod, mod)


def _neighbours(z, row0, t_lat, ctx_len):
    tm = z.shape[0]
    g = row0 + lax.broadcasted_iota(jnp.int32, (tm, 1), 0)
    is_ctx = g >= t_lat
    pos = jnp.where(is_ctx, g - t_lat, g & (GRID_W - 1))
    last = jnp.where(is_ctx, ctx_len - 1, GRID_W - 1)
    prev = jnp.where(pos == 0, 0.0, pltpu.roll(z, 1, 0))
    nxt = jnp.where(pos == last, 0.0, pltpu.roll(z, tm - 1, 0))
    return prev, nxt


def _dot_nt(a, b):
    return lax.dot_general(a, b, (((1,), (1,)), ((), ())), preferred_element_type=F32)


def _rows_spec(rows, width, row_of):
    return pl.BlockSpec((pl.Element(rows), pl.Element(width)),
                        lambda i, j: (pl.multiple_of(row_of(j), SUBLANES), 0))


def _inproj_shift_body(h_ref, wt_ref, mu_ref, wada_ref, cb_ref, bada_ref, o_ref, mod_ref, silu_ref,
                       *, t_lat, ctx_len, n_sub):
    @pl.when((pl.program_id(0) == 0) & (pl.program_id(1) == 0))
    def _():
        cb = cb_ref[...]
        silu_ref[...] = cb * _sigmoid(cb)

    d_model, mod_tn = wada_ref.shape
    n_acc = 16
    slab = d_model // n_acc
    parts = [jnp.sum((wada_ref[a * slab:(a + 1) * slab, :] * silu_ref[a * slab:(a + 1) * slab, :])
                     .reshape(slab // SUBLANES, SUBLANES, mod_tn), axis=0) for a in range(n_acc)]
    mod_row = jnp.sum(sum(parts), axis=0, keepdims=True) + bada_ref[...]
    mod_ref[...] = jnp.broadcast_to(mod_row, mod_ref.shape)
    anchor = lax.shift_right_logical(lax.shift_right_logical(mod_row.astype(jnp.int32), 16), 16).astype(F32)
    assert mod_row.shape == mu_ref.shape, "anchor rides on mu: equal tile widths"

    w = wt_ref[...].astype(BF16)
    tm = h_ref.shape[0]
    ts = tm // n_sub
    for s in range(n_sub):
        rows = slice(s * ts, (s + 1) * ts)
        z = _dot_nt(h_ref[rows, :], w)
        prev, nxt = _neighbours(z, pl.program_id(0) * tm + s * ts, t_lat, ctx_len)
        mu = mu_ref[...] + anchor if s == 0 else mu_ref[...]
        o_ref[rows, :] = z + mu * (0.5 * (prev + nxt) - z)


def _inproj_shift_call(h_all, w_in_t, mu, segments, w_ada, c_bcast, b_ada, *, t_lat, ctx_len, tm, tn, n_sub,
                       mod_col0, mod_cols, mod_tn):
    t, d = h_all.shape
    assert tm % (n_sub * GRID_W) == 0
    assert all(r % SUBLANES == 0 and n % tn == 0 and r + n <= w_in_t.shape[0] for r, n in segments)
    tile0 = [sum(n for _, n in segments[:s]) // tn for s in range(len(segments) + 1)]
    n_tiles = tile0[-1]

    def row_of(j):
        return sum(jnp.where((j >= lo) & (j < hi), r + tn * (j - lo), 0)
                   for (r, _), lo, hi in zip(segments, tile0[:-1], tile0[1:]))

    d_model = w_ada.shape[0]
    n_mod = mod_cols // mod_tn
    assert mod_cols % mod_tn == 0 and mod_col0 % mod_tn == 0 and n_mod <= (t // tm) * n_tiles
    assert c_bcast.shape == (d_model, mod_tn)

    def mod_tile(i, j):
        return jnp.minimum(i * n_tiles + j, n_mod - 1)

    return pl.pallas_call(
        functools.partial(_inproj_shift_body, t_lat=t_lat, ctx_len=ctx_len, n_sub=n_sub),
        grid=(t // tm, n_tiles),
        in_specs=[pl.BlockSpec((tm, d), lambda i, j: (i, 0), pipeline_mode=pl.Buffered(1)),
                  _rows_spec(tn, d, row_of),
                  pl.BlockSpec((1, tn), lambda i, j: (0, j)),
                  pl.BlockSpec((d_model, mod_tn), lambda i, j: (0, mod_col0 // mod_tn + mod_tile(i, j))),
                  pl.BlockSpec((d_model, mod_tn), lambda i, j: (0, 0)),
                  pl.BlockSpec((1, mod_tn), lambda i, j: (0, mod_col0 // mod_tn + mod_tile(i, j)))],
        out_specs=[pl.BlockSpec((tm, tn), lambda i, j: (i, j)),
                   pl.BlockSpec((MOD_ROWS, mod_tn), lambda i, j: (0, mod_tile(i, j)))],
        out_shape=[jax.ShapeDtypeStruct((t, n_tiles * tn), F32), jax.ShapeDtypeStruct((MOD_ROWS, mod_cols), F32)],
        scratch_shapes=[pltpu.VMEM((d_model, mod_tn), F32)],
        compiler_params=_cparams("arbitrary", "arbitrary"), name="inproj_shift",
    )(h_all, w_in_t, mu, w_ada, c_bcast, b_ada)


def _inproj_conv_body(h_ref, wb_ref, wc_ref, wx_ref, cw_ref, wcast_ref, o_ref, wcast_o_ref, *, t_lat):
    wcast_o_ref[...] = wcast_ref[...].astype(BF16)
    h = h_ref[...]
    u = _dot_nt(h, wc_ref[...].astype(BF16)) * _dot_nt(h, wx_ref[...].astype(BF16))
    prev, nxt = _neighbours(u, pl.program_id(0) * u.shape[0], t_lat, 1)
    cw = cw_ref[...]
    conv = cw[0:1, :] * prev + cw[1:2, :] * u + cw[2:3, :] * nxt
    o_ref[...] = (_dot_nt(h, wb_ref[...].astype(BF16)) * conv).astype(o_ref.dtype)


def _inproj_conv_call(h_all, w_in_t, conv_w, w_cast, *, row0, t_lat, tm, tc):
    d = h_all.shape[1]
    d_conv = conv_w.shape[1]
    assert d_conv % tc == 0 and row0 % SUBLANES == 0 and row0 + 3 * d_conv <= w_in_t.shape[0]
    cw = jnp.pad(conv_w, ((0, MOD_ROWS - 3), (0, 0)))
    n_j = d_conv // tc
    c_in, c_out, c_shape = _cast_rows_specs(w_cast, (t_lat // tm) * n_j, lambda i, j: i * n_j + j)
    return pl.pallas_call(
        functools.partial(_inproj_conv_body, t_lat=t_lat),
        grid=(t_lat // tm, n_j),
        in_specs=[pl.BlockSpec((tm, d), lambda i, j: (i, 0)),
                  _rows_spec(tc, d, lambda j: row0 + tc * j),
                  _rows_spec(tc, d, lambda j: row0 + d_conv + tc * j),
                  _rows_spec(tc, d, lambda j: row0 + 2 * d_conv + tc * j),
                  pl.BlockSpec((MOD_ROWS, tc), lambda i, j: (0, j)),
                  c_in],
        out_specs=[pl.BlockSpec((tm, tc), lambda i, j: (i, j)), c_out],
        out_shape=[jax.ShapeDtypeStruct((t_lat, d_conv), BF16), c_shape],
        compiler_params=_cparams("arbitrary", "arbitrary"), name="inproj_conv",
    )(h_all, w_in_t, w_in_t, w_in_t, cw, w_cast)


def _readout_body(y0_ref, y1_ref, k_ref, v_ref, r_ref, la_ref, gl_ref, ka_ref, rk_ref, a0_ref,
                  wa0_ref, wa1_ref, wg_ref, lw_ref, lb_ref, o_ref):
    la = _lora_tiles(la_ref[...])[1].astype(BF16)
    a0 = a0_ref[...]
    a_sum = _sigmoid(a0[0:1, :] + _dot(la, wa0_ref[...])) + _sigmoid(a0[1:2, :] + _dot(la, wa1_ref[...]))
    coef = r_ref[...] * rk_ref[...] * k_ref[...] * (2.0 + (a_sum - 2.0) * ka_ref[...])
    y = y0_ref[...] + y1_ref[...]
    g = _dot(_sigmoid(gl_ref[...]).astype(BF16), wg_ref[...])

    yc = y - _head_sums(y) * (1.0 / HEAD)
    var = _head_sums(yc * yc) * (1.0 / HEAD)
    yn = yc * lax.rsqrt(var + LNX_EPS) * lw_ref[...] + lb_ref[...]
    bonus = _head_sums(coef) * v_ref[...]
    o_ref[...] = ((yn + bonus) * g).astype(o_ref.dtype)


def _readout_call(y0, y1, za, k_a, r_k, a0, wa0, wa1, w_gate, lnx_w, lnx_b, *, t_lat, d_rwkv, tm):
    row = lambda jb: pl.BlockSpec((tm, d_rwkv), lambda i: (i, jb))
    vec = pl.BlockSpec((1, d_rwkv), lambda i: (0, 0))
    mat = pl.BlockSpec((LORA_TILE, d_rwkv), lambda i: (0, 0))
    lora_blk = 3 * d_rwkv // LORA_PAD
    assert GATE_LORA == LORA_PAD
    return pl.pallas_call(
        _readout_body, grid=(t_lat // tm,),
        in_specs=[row(0), row(0), row(0), row(1), row(2),
                  pl.BlockSpec((tm, LORA_PAD), lambda i: (i, lora_blk)),
                  pl.BlockSpec((tm, GATE_LORA), lambda i: (i, lora_blk + 1)),
                  vec, vec, pl.BlockSpec((2, d_rwkv), lambda i: (0, 0)),
                  mat, mat, pl.BlockSpec((GATE_LORA, d_rwkv), lambda i: (0, 0)), vec, vec],
        out_specs=pl.BlockSpec((tm, d_rwkv), lambda i: (i, 0)),
        out_shape=jax.ShapeDtypeStruct((t_lat, d_rwkv), BF16),
        compiler_params=_cparams("arbitrary"), name="rwkv_readout",
    )(y0, y1, za, za, za, za, za, k_a, r_k, a0, wa0, wa1, w_gate, lnx_w, lnx_b)


def _outproj_body(ya_ref, yb_ref, wa_ref, wb_ref, x_ref, g_ref, n2_ref, sh_ref, sc_ref, o_ref, h_ref, rows_ref):
    j = pl.program_id(1)
    n_tiles, _, tn = rows_ref.shape
    acc = _dot(ya_ref[...], wa_ref[...]) + _dot(yb_ref[...], wb_ref[...])
    x1 = x_ref[...] + g_ref[0:1, :] * acc
    o_ref[...] = x1
    rows_ref[j] = x1

    @pl.when(j == n_tiles - 1)
    def _():
        ss = sum(jnp.sum(rows_ref[s] * rows_ref[s], axis=-1, keepdims=True) for s in range(n_tiles))
        rs = lax.rsqrt(ss * (1.0 / (n_tiles * tn)) + NORM_EPS)
        for s in range(n_tiles):
            sl = slice(s * tn, (s + 1) * tn)
            y = rows_ref[s] * rs * n2_ref[:, sl]
            h_ref[:, sl] = (y * (1.0 + sc_ref[0:1, sl]) + sh_ref[0:1, sl]).astype(h_ref.dtype)


def _outproj_call(ya, yb, w, x, gate, gain2, shift, scale, *, tm, tn):
    (gate_arr, gate_col), (shift_arr, shift_col), (scale_arr, scale_col) = gate, shift, scale
    t, da = ya.shape
    db = yb.shape[1]
    n = w.shape[1]
    assert da == db
    return pl.pallas_call(
        _outproj_body, grid=(t // tm, n // tn),
        in_specs=[pl.BlockSpec((tm, da), lambda i, j: (i, 0)),
                  pl.BlockSpec((tm, db), lambda i, j: (i, 0)),
                  pl.BlockSpec((da, tn), lambda i, j: (0, j)),
                  pl.BlockSpec((db, tn), lambda i, j: (1, j)),
                  pl.BlockSpec((tm, tn), lambda i, j: (i, j)),
                  pl.BlockSpec((MOD_ROWS, tn), lambda i, j: (0, gate_col // tn + j)),
                  pl.BlockSpec((1, n), lambda i, j: (0, 0)),
                  pl.BlockSpec((MOD_ROWS, n), lambda i, j: (0, shift_col // n)),
                  pl.BlockSpec((MOD_ROWS, n), lambda i, j: (0, scale_col // n))],
        out_specs=[pl.BlockSpec((tm, tn), lambda i, j: (i, j)),
                   pl.BlockSpec((tm, n), lambda i, j: (i, 0))],
        out_shape=[jax.ShapeDtypeStruct((t, n), F32), jax.ShapeDtypeStruct((t, n), BF16)],
        scratch_shapes=[pltpu.VMEM((n // tn, tm, tn), F32)],
        compiler_params=_cparams("arbitrary", "arbitrary"), name="outproj_residual_norm",
    )(ya, yb, w, w, x, gate_arr, gain2, shift_arr, scale_arr)


def _cast_rows_specs(w, n_steps, step_of):
    rows, cols = w.shape
    assert rows % n_steps == 0 and (rows // n_steps) % BF16_SUBLANES == 0
    blk = (rows // n_steps, cols)
    return (pl.BlockSpec(blk, lambda i, j: (step_of(i, j), 0)), pl.BlockSpec(blk, lambda i, j: (step_of(i, j), 0)),
            jax.ShapeDtypeStruct(w.shape, BF16))


def _ffn_up_body(h_ref, wg_ref, wu_ref, wcast_ref, o_ref, wcast_o_ref):
    h = h_ref[...]
    g = _dot(h, wg_ref[...].astype(BF16))
    u = _dot(h, wu_ref[...].astype(BF16))
    o_ref[...] = (g * _sigmoid(g) * u).astype(o_ref.dtype)
    wcast_o_ref[...] = wcast_ref[...].astype(BF16)


def _ffn_up_call(h, wg, wu, w_cast, *, tm, tn):
    t, d = h.shape
    n = wg.shape[1]
    n_j = n // tn
    c_in, c_out, c_shape = _cast_rows_specs(w_cast, (t // tm) * n_j, lambda i, j: i * n_j + j)
    return pl.pallas_call(
        _ffn_up_body, grid=(t // tm, n_j),
        in_specs=[pl.BlockSpec((tm, d), lambda i, j: (i, 0), pipeline_mode=pl.Buffered(1)),
                  pl.BlockSpec((d, tn), lambda i, j: (0, j)),
                  pl.BlockSpec((d, tn), lambda i, j: (0, j)),
                  c_in],
        out_specs=[pl.BlockSpec((tm, tn), lambda i, j: (i, j)), c_out],
        out_shape=[jax.ShapeDtypeStruct((t, n), BF16), c_shape],
        compiler_params=_cparams("arbitrary", "arbitrary"), name="ffn_gate_up",
    )(h, wg, wu, w_cast)


def _ffn_down_body(a_ref, w_ref, x_ref, g_ref, o_ref):
    o_ref[...] = x_ref[...] + g_ref[0:1, :] * _dot(a_ref[...], w_ref[...])


def _ffn_down_call(act, w, x, gate, *, tm, tn):
    mod, gate_col = gate
    gate_blk0 = gate_col // tn
    t, kdim = act.shape
    d = w.shape[1]
    return pl.pallas_call(
        _ffn_down_body, grid=(t // tm, d // tn),
        in_specs=[pl.BlockSpec((tm, kdim), lambda i, j: (i, 0)),
                  pl.BlockSpec((kdim, tn), lambda i, j: (0, j)),
                  pl.BlockSpec((tm, tn), lambda i, j: (i, j)),
                  pl.BlockSpec((MOD_ROWS, tn), lambda i, j: (0, gate_blk0 + j))],
        out_specs=pl.BlockSpec((tm, tn), lambda i, j: (i, j)),
        out_shape=jax.ShapeDtypeStruct((t, d), F32),
        compiler_params=_cparams("arbitrary", "arbitrary"), name="ffn_down_residual",
    )(act, w, x, mod)


def _final_norm_body(x_ref, g_ref, o_ref):
    x = x_ref[...]
    o_ref[...] = x * lax.rsqrt(jnp.mean(x * x, axis=-1, keepdims=True) + NORM_EPS) * g_ref[...]


def _final_norm_call(x, gain, *, tm):
    t, d = x.shape
    return pl.pallas_call(
        _final_norm_body, grid=(t // tm,),
        in_specs=[pl.BlockSpec((tm, d), lambda i: (i, 0)), pl.BlockSpec((1, d), lambda i: (0, 0))],
        out_specs=pl.BlockSpec((tm, d), lambda i: (i, 0)),
        out_shape=jax.ShapeDtypeStruct((t, d), F32),
        compiler_params=_cparams("arbitrary"), name="final_norm",
    )(x, gain)


def _tiles():
    return dict(mod_tn=1024, norm_tm=256, shift_tm=2816, shift_tn=256, shift_sub=4, conv_tm=1024, conv_tc=256,
                side_mod_tn=256, scan_sub=4, readout_tm=256, out_tm=512, out_tn=1024, up_tm=2048, up_tn=256,
                down_tm=512, down_tn=512, final_tm=512)


def kernel(x, c, ctx, c_ctx, w_ada, b_ada, norm1, w_in, mu_shift, k_k, k_a, r_k, w0, w_decay_up, a0, w_iclr_up, w_gate_up, lnx_w, lnx_b, conv_w, w_out, norm2, w_ffn_gate, w_ffn_up, w_ffn_down, norm_f):
    assert x.shape[0] == 1 and w_ada.shape[0] == 1, "single batch element, single layer"
    tl = _tiles()
    t_lat, d = x.shape[1], x.shape[2]
    ctx_len = ctx.shape[1]
    d_rwkv = k_k.shape[1]
    d_conv = conv_w.shape[2]
    d_ff = w_ffn_gate.shape[2]
    assert t_lat % GRID_W == 0 and ctx_len % CHUNK == 0 and d_rwkv % SEG_W == 0
    x2d, ctx2d = x[0], ctx[0]

    off_lora = 2 * d_rwkv
    off_r = off_lora + DECAY_LORA + ICLR_LORA
    off_gl = off_r + d_rwkv
    rwkv_cols = off_gl + GATE_LORA
    wi_t = jnp.swapaxes(w_in, 1, 2)[0]
    segments = [(0, off_lora), (off_r, d_rwkv), (off_lora, LORA_PAD), (off_gl, GATE_LORA)]
    ms = mu_shift[0]
    mu = jnp.concatenate([ms[r0:r0 + n] for r0, n in segments])[None, :]
    wd = [jnp.zeros((LORA_TILE, d_rwkv), BF16).at[:DECAY_LORA].set(w_decay_up[0, i].astype(BF16)) for i in range(2)]
    wa = [jnp.zeros((LORA_TILE, d_rwkv), BF16).at[:ICLR_LORA].set(w_iclr_up[0, i].astype(BF16)) for i in range(2)]

    cc = jnp.zeros((MOD_ROWS, d), F32).at[0].set(c[0]).at[1].set(c_ctx)
    mod1 = _mod_call(cc, w_ada[0], b_ada, n_cols=2 * d, tn=tl["mod_tn"])
    c_bcast = jnp.broadcast_to(c[0][:, None], (d, tl["side_mod_tn"]))

    h_all = _norm_call(x2d, ctx2d, norm1, mod1, 0, 1, tm=tl["norm_tm"])
    za, mod2 = _inproj_shift_call(h_all, wi_t, mu, segments, w_ada[0], c_bcast, b_ada, t_lat=t_lat, ctx_len=ctx_len,
                                  tm=tl["shift_tm"], tn=tl["shift_tn"], n_sub=tl["shift_sub"],
                                  mod_col0=2 * d, mod_cols=4 * d, mod_tn=tl["side_mod_tn"])
    y_conv, wo = _inproj_conv_call(h_all, wi_t, conv_w[0], w_out[0], row0=rwkv_cols, t_lat=t_lat,
                                   tm=tl["conv_tm"], tc=tl["conv_tc"])
    n_lat_chunks, n_ctx_chunks = t_lat // CHUNK, ctx_len // CHUNK
    ys = [_scan_call(za, k_k, k_a, w0[0, i:i + 1], a0[0, i:i + 1], wd[i], wa[i], rev=(i == 1),
                     n_lat_chunks=n_lat_chunks, n_ctx_chunks=n_ctx_chunks, d_rwkv=d_rwkv,
                     n_sub=tl["scan_sub"]) for i in range(2)]
    y_rwkv = _readout_call(ys[0], ys[1], za, k_a, r_k.reshape(1, d_rwkv), a0[0], wa[0], wa[1],
                           w_gate_up[0].astype(BF16), lnx_w, lnx_b, t_lat=t_lat, d_rwkv=d_rwkv,
                           tm=tl["readout_tm"])
    x1, h2 = _outproj_call(y_rwkv, y_conv, wo, x2d, (mod2, 0), norm2, (mod2, d), (mod2, 2 * d),
                           tm=tl["out_tm"], tn=tl["out_tn"])

    act, wdn = _ffn_up_call(h2, w_ffn_gate[0], w_ffn_up[0], w_ffn_down[0], tm=tl["up_tm"], tn=tl["up_tn"])
    x2 = _ffn_down_call(act, wdn, x1, (mod2, 3 * d), tm=tl["down_tm"], tn=tl["down_tn"])
    out = _final_norm_call(x2, norm_f[None, :], tm=tl["final_tm"])
    return out[None]
```

```python
import functools
import math

import jax
import jax.numpy as jnp
from jax import lax
from jax.experimental import pallas as pl
from jax.experimental.pallas import tpu as pltpu

F32 = jnp.float32
BF16 = jnp.bfloat16

SUBLANES = 8
BF16_SUBLANES = 16
MXU_WIDTH = 256
SEG_W = MXU_WIDTH
VMEM_LIMIT_BYTES = 56 * 1024 * 1024

HEAD = 64
GROUP = 128
CHUNK = 64
GRID_W = 64
DECAY_LORA = 96
ICLR_LORA = 96
GATE_LORA = 256
LORA_TILE = 128
LORA_PAD = 2 * LORA_TILE
NORM_EPS = 1e-6
LNX_EPS = 64e-5
EXP_M05 = math.exp(-0.5)


def _cparams(*sem):
    return pltpu.CompilerParams(dimension_semantics=sem, vmem_limit_bytes=VMEM_LIMIT_BYTES)


def _dot(a, b):
    return jnp.dot(a, b, preferred_element_type=F32)


def _bmm(a, b):
    return lax.dot_general(a, b, (((2,), (1,)), ((0,), (0,))), preferred_element_type=F32)


def _bmm_nt(a, b):
    return lax.dot_general(a, b, (((2,), (2,)), ((0,), (0,))), preferred_element_type=F32)


def _bmm_tn(a, b):
    return lax.dot_general(a, b, (((1,), (1,)), ((0,), (0,))), preferred_element_type=F32)


def _split2(x):
    hi = x.astype(BF16)
    lo = (x - hi.astype(F32)).astype(BF16)
    return hi, lo


def _sigmoid(x):
    return 1.0 / (1.0 + jnp.exp(-x))


def _head_sums(x):
    rows, d = x.shape
    n_seg = d // SEG_W
    xs = jnp.concatenate([x[:, s * SEG_W:(s + 1) * SEG_W] for s in range(n_seg)], axis=0)
    rr = lax.broadcasted_iota(jnp.int32, (SEG_W, SEG_W), 0) // HEAD
    cc = lax.broadcasted_iota(jnp.int32, (SEG_W, SEG_W), 1) // HEAD
    ones_seg = (rr == cc).astype(BF16)
    hi, lo = _split2(xs)
    ss = _dot(hi, ones_seg) + _dot(lo, ones_seg)
    return jnp.concatenate([ss[s * rows:(s + 1) * rows] for s in range(n_seg)], axis=1)


def _lora_tiles(la):
    return la[:, :LORA_TILE], la[:, DECAY_LORA:DECAY_LORA + LORA_TILE]


def _scan_chunk(lw, l_cum, a, kd, kk, v, r, ht0, *, rev, n_groups):
    def st(x):
        return jnp.stack([x[:, p * GROUP:(p + 1) * GROUP] for p in range(n_groups)], axis=0)

    lw_s, l_s, a_s, kd_s, kk_s, v_s, r_s = (st(t) for t in (lw, l_cum, a, kd, kk, v, r))
    ltot = l_s[:, 0:1, :] if rev else l_s[:, CHUNK - 1:CHUNK, :]
    e_l = jnp.exp(l_s)
    e_nl = jnp.exp(-l_s)
    e_lx = jnp.exp(l_s - lw_s)
    e_tl = jnp.exp(ltot - l_s)
    g_tot = jnp.exp(ltot)

    t_i = lax.broadcasted_iota(jnp.int32, (1, CHUNK, GROUP), 1)
    lane = lax.broadcasted_iota(jnp.int32, (1, CHUNK, GROUP), 2)
    s_i = lane & (HEAD - 1)
    head_of_lane = lane // HEAD
    strict = (s_i > t_i) if rev else (s_i < t_i)
    incl = (s_i >= t_i) if rev else (s_i <= t_i)
    eye = (s_i == t_i).astype(F32)
    rr = lax.broadcasted_iota(jnp.int32, (GROUP, GROUP), 0)
    cc = lax.broadcasted_iota(jnp.int32, (GROUP, GROUP), 1)
    same_head = (rr // HEAD) == (cc // HEAD)

    def bd(x):
        z = jnp.zeros_like(x)
        return jnp.concatenate([jnp.where(head_of_lane == g, x, z) for g in range(GROUP // HEAD)], axis=1)

    b_s = kk_s * a_s
    ah = (-kk_s * e_lx).astype(BF16)
    rh = (r_s * e_l).astype(BF16)
    bc = (b_s * e_nl).astype(BF16)
    kc = (kd_s * e_nl).astype(BF16)
    kt = (kd_s * e_tl).astype(BF16)
    bt = (b_s * e_tl).astype(BF16)
    v_b = v_s.astype(BF16)

    sc = _bmm_nt(jnp.concatenate([ah, rh], axis=1),
                 jnp.concatenate([bd(bc), bd(kc)], axis=1))
    m_ab = jnp.where(strict, sc[:, :CHUNK, :GROUP], 0.0)
    m_ak = jnp.where(strict, sc[:, :CHUNK, GROUP:], 0.0).astype(BF16)
    n_rb = jnp.where(incl, sc[:, CHUNK:, :GROUP], 0.0).astype(BF16)
    n_rk = jnp.where(incl, sc[:, CHUNK:, GROUP:], 0.0).astype(BF16)

    s_acc = eye + m_ab
    m_pow = m_ab.astype(BF16)
    m_pow = _bmm(m_pow, bd(m_pow))
    n_steps = int(math.log2(CHUNK)) - 1
    for j in range(n_steps):
        mb = m_pow.astype(BF16)
        if j < n_steps - 1:
            both = _bmm(mb, jnp.concatenate([bd(mb), bd(s_acc.astype(BF16))], axis=2))
            m_pow = both[:, :, :GROUP]
            s_acc = s_acc + both[:, :, GROUP:]
        else:
            s_acc = s_acc + _bmm(mb, bd(s_acc.astype(BF16)))
    t_inv = s_acc.astype(BF16)

    ht0_b = ht0.astype(BF16)
    v_bd = bd(v_b)
    from_h0 = _bmm_nt(jnp.concatenate([ah, rh], axis=1), ht0_b)
    w = from_h0[:, :CHUNK] + _bmm(m_ak, v_bd)
    u_b = _bmm(t_inv, bd(w.astype(BF16))).astype(BF16)
    y = from_h0[:, CHUNK:] + _bmm(jnp.concatenate([n_rk, n_rb], axis=2),
                                  jnp.concatenate([v_bd, bd(u_b)], axis=1))
    upd = _bmm_tn(jnp.concatenate([v_b, u_b], axis=1), jnp.concatenate([kt, bt], axis=1))
    ht1 = ht0 * g_tot + jnp.where(same_head[None], upd, 0.0)
    return jnp.concatenate([y[p] for p in range(n_groups)], axis=1), ht1


def _scan_body(k_ref, v_ref, r_ref, la_ref, kk_ref, ka_ref, w0_ref, a0_ref, wd_ref, wa_ref,
               *rest, rev, n_groups):
    y_other_ref, y_ref, h_ref = rest if len(rest) == 3 else (None,) + rest
    c = pl.program_id(0)

    @pl.when(c == 0)
    def _():
        h_ref[...] = jnp.zeros_like(h_ref)

    k = k_ref[...]
    v = v_ref[...]
    r = r_ref[...]
    la_dec, la_icl = _lora_tiles(la_ref[...])
    dec_pre = w0_ref[...] + _dot(jnp.tanh(la_dec).astype(BF16), wd_ref[...])
    icl_pre = a0_ref[...] + _dot(la_icl.astype(BF16), wa_ref[...])
    lw = -EXP_M05 * _sigmoid(dec_pre)
    a = _sigmoid(icl_pre)
    kd = k * (1.0 + (a - 1.0) * ka_ref[...])
    kkr = k * kk_ref[...]
    kk = kkr / jnp.maximum(jnp.sqrt(_head_sums(kkr * kkr)), 1e-12)

    rows = k.shape[0]
    row = lax.broadcasted_iota(jnp.int32, (rows, rows), 0)
    col = lax.broadcasted_iota(jnp.int32, (rows, rows), 1)
    in_order = (col >= row) if rev else (col <= row)
    tri = (in_order & ((row // CHUNK) == (col // CHUNK))).astype(BF16)
    lw_hi, lw_lo = _split2(lw)
    l_cum = _dot(tri, lw_hi) + _dot(tri, lw_lo)

    n_sub = rows // CHUNK
    ht = h_ref[...]
    ys = [None] * n_sub
    for s in (reversed(range(n_sub)) if rev else range(n_sub)):
        sl = slice(s * CHUNK, (s + 1) * CHUNK)
        ys[s], ht = _scan_chunk(lw[sl], l_cum[sl], a[sl], kd[sl], kk[sl], v[sl], r[sl], ht,
                                rev=rev, n_groups=n_groups)
    y = jnp.concatenate(ys, axis=0)
    y_ref[...] = y if y_other_ref is None else y + y_other_ref[...]
    h_ref[...] = ht


def _scan_call(za, kk, ka, w0, a0, wd, wa, y_other=None, *, rev, n_lat_chunks, n_ctx_chunks, d_rwkv, n_sub):
    assert n_lat_chunks % n_sub == 0 and n_ctx_chunks % n_sub == 0
    n_lat_blk, n_ctx_blk = n_lat_chunks // n_sub, n_ctx_chunks // n_sub
    n_blk = n_lat_blk + n_ctx_blk
    rows = n_sub * CHUNK
    n_groups = d_rwkv // GROUP
    lora_blk = 3 * d_rwkv // LORA_PAD
    assert CHUNK == HEAD and d_rwkv % GROUP == 0

    def blk_of(c):
        if rev:
            return n_blk - 1 - c
        return jnp.where(c < n_ctx_blk, n_lat_blk + c, c - n_ctx_blk)

    vec = pl.BlockSpec((1, d_rwkv), lambda c: (0, 0))
    mat = pl.BlockSpec((LORA_TILE, d_rwkv), lambda c: (0, 0))
    y_spec = pl.BlockSpec((rows, d_rwkv), lambda c: (blk_of(c), 0))
    others = () if y_other is None else (y_other,)
    return pl.pallas_call(
        functools.partial(_scan_body, rev=rev, n_groups=n_groups),
        grid=(n_blk,),
        in_specs=[
            pl.BlockSpec((rows, d_rwkv), lambda c: (blk_of(c), 0)),
            pl.BlockSpec((rows, d_rwkv), lambda c: (blk_of(c), 1)),
            pl.BlockSpec((rows, d_rwkv), lambda c: (blk_of(c), 2)),
            pl.BlockSpec((rows, LORA_PAD), lambda c: (blk_of(c), lora_blk)),
            vec, vec, vec, vec, mat, mat,
        ] + [y_spec] * len(others),
        out_specs=y_spec,
        out_shape=jax.ShapeDtypeStruct((n_blk * rows, d_rwkv), F32),
        scratch_shapes=[pltpu.VMEM((n_groups, GROUP, GROUP), F32)],
        compiler_params=_cparams("arbitrary"),
        name="wkv_scan_rev" if rev else "wkv_scan_fwd",
    )(za, za, za, za, kk, ka, w0, a0, wd, wa, *others)


MOD_ROWS = 8


def _mod_body(cc_ref, w_ref, b_ref, o_ref):
    cc = cc_ref[...]
    s = cc * _sigmoid(cc)
    s_hi, s_lo = _split2(s)
    w = w_ref[...].astype(BF16)
    o_ref[...] = _dot(s_hi, w) + _dot(s_lo, w) + b_ref[...]


def _mod_call(cc, w_ada, b_ada, *, n_cols, tn):
    d, n = w_ada.shape[0], n_cols
    assert n % tn == 0 and n <= w_ada.shape[1]
    return pl.pallas_call(
        _mod_body,
        grid=(n // tn,),
        in_specs=[pl.BlockSpec((MOD_ROWS, d), lambda j: (0, 0)),
                  pl.BlockSpec((d, tn), lambda j: (0, j)),
                  pl.BlockSpec((1, tn), lambda j: (0, j))],
        out_specs=pl.BlockSpec((MOD_ROWS, tn), lambda j: (0, j)),
        out_shape=jax.ShapeDtypeStruct((MOD_ROWS, n), F32),
        compiler_params=_cparams("arbitrary"),
        name="adaln_mod",
    )(cc, w_ada, b_ada)


def _norm_mod(xf, gain, shift, scale):
    y = xf * lax.rsqrt(jnp.mean(xf * xf, axis=-1, keepdims=True) + NORM_EPS) * gain
    return y * (1.0 + scale) + shift


def _norm_ctx_body(x_ref, ctx_ref, g_ref, sh_ref, sc_ref, o_ref, *, n_lat_blocks):
    i = pl.program_id(0)

    @pl.when(i < n_lat_blocks)
    def _():
        o_ref[...] = _norm_mod(x_ref[...], g_ref[...], sh_ref[0:1, :], sc_ref[0:1, :]).astype(o_ref.dtype)

    @pl.when(i >= n_lat_blocks)
    def _():
        o_ref[...] = _norm_mod(ctx_ref[...], g_ref[...], sh_ref[1:2, :], sc_ref[1:2, :]).astype(o_ref.dtype)


def _norm_call(x, ctx, gain, mod, shift_blk, scale_blk, *, tm):
    t, d = x.shape
    n_lat = t // tm
    vec = pl.BlockSpec((1, d), lambda i: (0, 0))
    sh = pl.BlockSpec((MOD_ROWS, d), lambda i: (0, shift_blk))
    sc = pl.BlockSpec((MOD_ROWS, d), lambda i: (0, scale_blk))
    tc = ctx.shape[0]
    n_ctx = tc // tm
    return pl.pallas_call(
        functools.partial(_norm_ctx_body, n_lat_blocks=n_lat), grid=(n_lat + n_ctx,),
        in_specs=[pl.BlockSpec((tm, d), lambda i: (jnp.minimum(i, n_lat - 1), 0)),
                  pl.BlockSpec((tm, d), lambda i: (jnp.maximum(i - n_lat, 0), 0)), vec, sh, sc],
        out_specs=pl.BlockSpec((tm, d), lambda i: (i, 0)),
        out_shape=jax.ShapeDtypeStruct((t + tc, d), BF16),
        compiler_params=_cparams("arbitrary"), name="norm_mod_ctx",
    )(x, ctx, gain, mod, mod)


def _neighbours(z, row0, t_lat, ctx_len):
    tm = z.shape[0]
    g = row0 + lax.broadcasted_iota(jnp.int32, (tm, 1), 0)
    is_ctx = g >= t_lat
    pos = jnp.where(is_ctx, g - t_lat, g & (GRID_W - 1))
    last = jnp.where(is_ctx, ctx_len - 1, GRID_W - 1)
    prev = jnp.where(pos == 0, 0.0, pltpu.roll(z, 1, 0))
    nxt = jnp.where(pos == last, 0.0, pltpu.roll(z, tm - 1, 0))
    return prev, nxt


def _dot_nt(a, b):
    return lax.dot_general(a, b, (((1,), (1,)), ((), ())), preferred_element_type=F32)


def _rows_spec(rows, width, row_of):
    return pl.BlockSpec((pl.Element(rows), pl.Element(width)),
                        lambda i, j: (pl.multiple_of(row_of(j), SUBLANES), 0))


def _inproj_shift_body(h_ref, wt_ref, mu_ref, wada_ref, cb_ref, bada_ref, o_ref, mod_ref, silu_ref,
                       *, t_lat, ctx_len, n_sub):
    @pl.when((pl.program_id(0) == 0) & (pl.program_id(1) == 0))
    def _():
        cb = cb_ref[...]
        silu_ref[...] = cb * _sigmoid(cb)

    d_model, mod_tn = wada_ref.shape
    n_acc = 16
    slab = d_model // n_acc
    parts = [jnp.sum((wada_ref[a * slab:(a + 1) * slab, :] * silu_ref[a * slab:(a + 1) * slab, :])
                     .reshape(slab // SUBLANES, SUBLANES, mod_tn), axis=0) for a in range(n_acc)]
    mod_row = jnp.sum(sum(parts), axis=0, keepdims=True) + bada_ref[...]
    mod_ref[...] = jnp.broadcast_to(mod_row, mod_ref.shape)

    w = wt_ref[...].astype(BF16)
    tm = h_ref.shape[0]
    ts = tm // n_sub
    for s in range(n_sub):
        rows = slice(s * ts, (s + 1) * ts)
        z = _dot_nt(h_ref[rows, :], w)
        prev, nxt = _neighbours(z, pl.program_id(0) * tm + s * ts, t_lat, ctx_len)
        o_ref[rows, :] = z + mu_ref[...] * (0.5 * (prev + nxt) - z)


def _inproj_shift_call(h_all, w_in_t, mu, segments, w_ada, c_bcast, b_ada, *, t_lat, ctx_len, tm, tn, n_sub,
                       mod_col0, mod_cols, mod_tn):
    t, d = h_all.shape
    assert tm % (n_sub * GRID_W) == 0
    assert all(r % SUBLANES == 0 and n % tn == 0 and r + n <= w_in_t.shape[0] for r, n in segments)
    tile0 = [sum(n for _, n in segments[:s]) // tn for s in range(len(segments) + 1)]
    n_tiles = tile0[-1]

    def row_of(j):
        return sum(jnp.where((j >= lo) & (j < hi), r + tn * (j - lo), 0)
                   for (r, _), lo, hi in zip(segments, tile0[:-1], tile0[1:]))

    d_model = w_ada.shape[0]
    n_mod = mod_cols // mod_tn
    assert mod_cols % mod_tn == 0 and mod_col0 % mod_tn == 0 and n_mod <= (t // tm) * n_tiles
    assert c_bcast.shape == (d_model, mod_tn)

    def mod_tile(i, j):
        return jnp.minimum(i * n_tiles + j, n_mod - 1)

    return pl.pallas_call(
        functools.partial(_inproj_shift_body, t_lat=t_lat, ctx_len=ctx_len, n_sub=n_sub),
        grid=(t // tm, n_tiles),
        in_specs=[pl.BlockSpec((tm, d), lambda i, j: (i, 0), pipeline_mode=pl.Buffered(1)),
                  _rows_spec(tn, d, row_of),
                  pl.BlockSpec((1, tn), lambda i, j: (0, j)),
                  pl.BlockSpec((d_model, mod_tn), lambda i, j: (0, mod_col0 // mod_tn + mod_tile(i, j))),
                  pl.BlockSpec((d_model, mod_tn), lambda i, j: (0, 0)),
                  pl.BlockSpec((1, mod_tn), lambda i, j: (0, mod_col0 // mod_tn + mod_tile(i, j)))],
        out_specs=[pl.BlockSpec((tm, tn), lambda i, j: (i, j)),
                   pl.BlockSpec((MOD_ROWS, mod_tn), lambda i, j: (0, mod_tile(i, j)))],
        out_shape=[jax.ShapeDtypeStruct((t, n_tiles * tn), F32), jax.ShapeDtypeStruct((MOD_ROWS, mod_cols), F32)],
        scratch_shapes=[pltpu.VMEM((d_model, mod_tn), F32)],
        compiler_params=_cparams("arbitrary", "arbitrary"), name="inproj_shift",
    )(h_all, w_in_t, mu, w_ada, c_bcast, b_ada)


def _inproj_conv_body(h_ref, wb_ref, wc_ref, wx_ref, cw_ref, wcast_ref, o_ref, wcast_o_ref, *, t_lat):
    wcast_o_ref[...] = wcast_ref[...].astype(BF16)
    h = h_ref[...]
    u = _dot_nt(h, wc_ref[...].astype(BF16)) * _dot_nt(h, wx_ref[...].astype(BF16))
    prev, nxt = _neighbours(u, pl.program_id(0) * u.shape[0], t_lat, 1)
    cw = cw_ref[...]
    conv = cw[0:1, :] * prev + cw[1:2, :] * u + cw[2:3, :] * nxt
    o_ref[...] = (_dot_nt(h, wb_ref[...].astype(BF16)) * conv).astype(o_ref.dtype)


def _inproj_conv_call(h_all, w_in_t, conv_w, w_cast, *, row0, t_lat, tm, tc):
    d = h_all.shape[1]
    d_conv = conv_w.shape[1]
    assert d_conv % tc == 0 and row0 % SUBLANES == 0 and row0 + 3 * d_conv <= w_in_t.shape[0]
    cw = jnp.pad(conv_w, ((0, MOD_ROWS - 3), (0, 0)))
    n_j = d_conv // tc
    c_in, c_out, c_shape = _cast_rows_specs(w_cast, (t_lat // tm) * n_j, lambda i, j: i * n_j + j)
    return pl.pallas_call(
        functools.partial(_inproj_conv_body, t_lat=t_lat),
        grid=(t_lat // tm, n_j),
        in_specs=[pl.BlockSpec((tm, d), lambda i, j: (i, 0)),
                  _rows_spec(tc, d, lambda j: row0 + tc * j),
                  _rows_spec(tc, d, lambda j: row0 + d_conv + tc * j),
                  _rows_spec(tc, d, lambda j: row0 + 2 * d_conv + tc * j),
                  pl.BlockSpec((MOD_ROWS, tc), lambda i, j: (0, j)),
                  c_in],
        out_specs=[pl.BlockSpec((tm, tc), lambda i, j: (i, j)), c_out],
        out_shape=[jax.ShapeDtypeStruct((t_lat, d_conv), BF16), c_shape],
        compiler_params=_cparams("arbitrary", "arbitrary"), name="inproj_conv",
    )(h_all, w_in_t, w_in_t, w_in_t, cw, w_cast)


def _readout_body(y_ref, k_ref, v_ref, r_ref, la_ref, gl_ref, ka_ref, rk_ref, a0_ref,
                  wa0_ref, wa1_ref, wg_ref, lw_ref, lb_ref, o_ref):
    la = _lora_tiles(la_ref[...])[1].astype(BF16)
    a0 = a0_ref[...]
    a_sum = _sigmoid(a0[0:1, :] + _dot(la, wa0_ref[...])) + _sigmoid(a0[1:2, :] + _dot(la, wa1_ref[...]))
    coef = r_ref[...] * rk_ref[...] * k_ref[...] * (2.0 + (a_sum - 2.0) * ka_ref[...])
    y = y_ref[...]
    g = _dot(_sigmoid(gl_ref[...]).astype(BF16), wg_ref[...])

    yc = y - _head_sums(y) * (1.0 / HEAD)
    var = _head_sums(yc * yc) * (1.0 / HEAD)
    yn = yc * lax.rsqrt(var + LNX_EPS) * lw_ref[...] + lb_ref[...]
    bonus = _head_sums(coef) * v_ref[...]
    o_ref[...] = ((yn + bonus) * g).astype(o_ref.dtype)


def _readout_call(y, za, k_a, r_k, a0, wa0, wa1, w_gate, lnx_w, lnx_b, *, t_lat, d_rwkv, tm):
    row = lambda jb: pl.BlockSpec((tm, d_rwkv), lambda i: (i, jb))
    vec = pl.BlockSpec((1, d_rwkv), lambda i: (0, 0))
    mat = pl.BlockSpec((LORA_TILE, d_rwkv), lambda i: (0, 0))
    lora_blk = 3 * d_rwkv // LORA_PAD
    assert GATE_LORA == LORA_PAD
    return pl.pallas_call(
        _readout_body, grid=(t_lat // tm,),
        in_specs=[row(0), row(0), row(1), row(2),
                  pl.BlockSpec((tm, LORA_PAD), lambda i: (i, lora_blk)),
                  pl.BlockSpec((tm, GATE_LORA), lambda i: (i, lora_blk + 1)),
                  vec, vec, pl.BlockSpec((2, d_rwkv), lambda i: (0, 0)),
                  mat, mat, pl.BlockSpec((GATE_LORA, d_rwkv), lambda i: (0, 0)), vec, vec],
        out_specs=pl.BlockSpec((tm, d_rwkv), lambda i: (i, 0)),
        out_shape=jax.ShapeDtypeStruct((t_lat, d_rwkv), BF16),
        compiler_params=_cparams("arbitrary"), name="rwkv_readout",
    )(y, za, za, za, za, za, k_a, r_k, a0, wa0, wa1, w_gate, lnx_w, lnx_b)


def _outproj_body(ya_ref, yb_ref, wa_ref, wb_ref, x_ref, g_ref, n2_ref, sh_ref, sc_ref, o_ref, h_ref, rows_ref):
    j = pl.program_id(1)
    n_tiles, _, tn = rows_ref.shape
    acc = _dot(ya_ref[...], wa_ref[...]) + _dot(yb_ref[...], wb_ref[...])
    x1 = x_ref[...] + g_ref[0:1, :] * acc
    o_ref[...] = x1
    rows_ref[j] = x1

    @pl.when(j == n_tiles - 1)
    def _():
        ss = sum(jnp.sum(rows_ref[s] * rows_ref[s], axis=-1, keepdims=True) for s in range(n_tiles))
        rs = lax.rsqrt(ss * (1.0 / (n_tiles * tn)) + NORM_EPS)
        for s in range(n_tiles):
            sl = slice(s * tn, (s + 1) * tn)
            y = rows_ref[s] * rs * n2_ref[:, sl]
            h_ref[:, sl] = (y * (1.0 + sc_ref[0:1, sl]) + sh_ref[0:1, sl]).astype(h_ref.dtype)


def _outproj_call(ya, yb, w, x, gate, gain2, shift, scale, *, tm, tn):
    (gate_arr, gate_col), (shift_arr, shift_col), (scale_arr, scale_col) = gate, shift, scale
    t, da = ya.shape
    db = yb.shape[1]
    n = w.shape[1]
    assert da == db
    return pl.pallas_call(
        _outproj_body, grid=(t // tm, n // tn),
        in_specs=[pl.BlockSpec((tm, da), lambda i, j: (i, 0)),
                  pl.BlockSpec((tm, db), lambda i, j: (i, 0)),
                  pl.BlockSpec((da, tn), lambda i, j: (0, j)),
                  pl.BlockSpec((db, tn), lambda i, j: (1, j)),
                  pl.BlockSpec((tm, tn), lambda i, j: (i, j)),
                  pl.BlockSpec((MOD_ROWS, tn), lambda i, j: (0, gate_col // tn + j)),
                  pl.BlockSpec((1, n), lambda i, j: (0, 0)),
                  pl.BlockSpec((MOD_ROWS, n), lambda i, j: (0, shift_col // n)),
                  pl.BlockSpec((MOD_ROWS, n), lambda i, j: (0, scale_col // n))],
        out_specs=[pl.BlockSpec((tm, tn), lambda i, j: (i, j)),
                   pl.BlockSpec((tm, n), lambda i, j: (i, 0))],
        out_shape=[jax.ShapeDtypeStruct((t, n), F32), jax.ShapeDtypeStruct((t, n), BF16)],
        scratch_shapes=[pltpu.VMEM((n // tn, tm, tn), F32)],
        compiler_params=_cparams("arbitrary", "arbitrary"), name="outproj_residual_norm",
    )(ya, yb, w, w, x, gate_arr, gain2, shift_arr, scale_arr)


def _cast_rows_specs(w, n_steps, step_of):
    rows, cols = w.shape
    assert rows % n_steps == 0 and (rows // n_steps) % BF16_SUBLANES == 0
    blk = (rows // n_steps, cols)
    return (pl.BlockSpec(blk, lambda i, j: (step_of(i, j), 0)), pl.BlockSpec(blk, lambda i, j: (step_of(i, j), 0)),
            jax.ShapeDtypeStruct(w.shape, BF16))


def _ffn_up_body(h_ref, wg_ref, wu_ref, wcast_ref, o_ref, wcast_o_ref):
    h = h_ref[...]
    g = _dot(h, wg_ref[...].astype(BF16))
    u = _dot(h, wu_ref[...].astype(BF16))
    o_ref[...] = (g * _sigmoid(g) * u).astype(o_ref.dtype)
    wcast_o_ref[...] = wcast_ref[...].astype(BF16)


def _ffn_up_call(h, wg, wu, w_cast, *, tm, tn):
    t, d = h.shape
    n = wg.shape[1]
    n_j = n // tn
    c_in, c_out, c_shape = _cast_rows_specs(w_cast, (t // tm) * n_j, lambda i, j: i * n_j + j)
    return pl.pallas_call(
        _ffn_up_body, grid=(t // tm, n_j),
        in_specs=[pl.BlockSpec((tm, d), lambda i, j: (i, 0), pipeline_mode=pl.Buffered(1)),
                  pl.BlockSpec((d, tn), lambda i, j: (0, j)),
                  pl.BlockSpec((d, tn), lambda i, j: (0, j)),
                  c_in],
        out_specs=[pl.BlockSpec((tm, tn), lambda i, j: (i, j)), c_out],
        out_shape=[jax.ShapeDtypeStruct((t, n), BF16), c_shape],
        compiler_params=_cparams("arbitrary", "arbitrary"), name="ffn_gate_up",
    )(h, wg, wu, w_cast)


def _ffn_down_body(a_ref, w_ref, x_ref, g_ref, o_ref):
    o_ref[...] = x_ref[...] + g_ref[0:1, :] * _dot(a_ref[...], w_ref[...])


def _ffn_down_call(act, w, x, gate, *, tm, tn):
    mod, gate_col = gate
    gate_blk0 = gate_col // tn
    t, kdim = act.shape
    d = w.shape[1]
    return pl.pallas_call(
        _ffn_down_body, grid=(t // tm, d // tn),
        in_specs=[pl.BlockSpec((tm, kdim), lambda i, j: (i, 0)),
                  pl.BlockSpec((kdim, tn), lambda i, j: (0, j)),
                  pl.BlockSpec((tm, tn), lambda i, j: (i, j)),
                  pl.BlockSpec((MOD_ROWS, tn), lambda i, j: (0, gate_blk0 + j))],
        out_specs=pl.BlockSpec((tm, tn), lambda i, j: (i, j)),
        out_shape=jax.ShapeDtypeStruct((t, d), F32),
        compiler_params=_cparams("arbitrary", "arbitrary"), name="ffn_down_residual",
    )(act, w, x, mod)


def _final_norm_body(x_ref, g_ref, o_ref):
    x = x_ref[...]
    o_ref[...] = x * lax.rsqrt(jnp.mean(x * x, axis=-1, keepdims=True) + NORM_EPS) * g_ref[...]


def _final_norm_call(x, gain, *, tm):
    t, d = x.shape
    return pl.pallas_call(
        _final_norm_body, grid=(t // tm,),
        in_specs=[pl.BlockSpec((tm, d), lambda i: (i, 0)), pl.BlockSpec((1, d), lambda i: (0, 0))],
        out_specs=pl.BlockSpec((tm, d), lambda i: (i, 0)),
        out_shape=jax.ShapeDtypeStruct((t, d), F32),
        compiler_params=_cparams("arbitrary"), name="final_norm",
    )(x, gain)


def _tiles():
    return dict(mod_tn=1024, norm_tm=256, shift_tm=2816, shift_tn=256, shift_sub=4, conv_tm=1024, conv_tc=256,
                side_mod_tn=256, scan_sub=4, readout_tm=256, out_tm=512, out_tn=1024, up_tm=2048, up_tn=256,
                down_tm=512, down_tn=512, final_tm=512)


def kernel(x, c, ctx, c_ctx, w_ada, b_ada, norm1, w_in, mu_shift, k_k, k_a, r_k, w0, w_decay_up, a0, w_iclr_up, w_gate_up, lnx_w, lnx_b, conv_w, w_out, norm2, w_ffn_gate, w_ffn_up, w_ffn_down, norm_f):
    assert x.shape[0] == 1 and w_ada.shape[0] == 1, "single batch element, single layer"
    tl = _tiles()
    t_lat, d = x.shape[1], x.shape[2]
    ctx_len = ctx.shape[1]
    d_rwkv = k_k.shape[1]
    d_conv = conv_w.shape[2]
    d_ff = w_ffn_gate.shape[2]
    assert t_lat % GRID_W == 0 and ctx_len % CHUNK == 0 and d_rwkv % SEG_W == 0
    x2d, ctx2d = x[0], ctx[0]

    off_lora = 2 * d_rwkv
    off_r = off_lora + DECAY_LORA + ICLR_LORA
    off_gl = off_r + d_rwkv
    rwkv_cols = off_gl + GATE_LORA
    wi_t = jnp.swapaxes(w_in, 1, 2)[0]
    segments = [(0, off_lora), (off_r, d_rwkv), (off_lora, LORA_PAD), (off_gl, GATE_LORA)]
    ms = mu_shift[0]
    mu = jnp.concatenate([ms[r0:r0 + n] for r0, n in segments])[None, :]
    wd = [jnp.zeros((LORA_TILE, d_rwkv), BF16).at[:DECAY_LORA].set(w_decay_up[0, i].astype(BF16)) for i in range(2)]
    wa = [jnp.zeros((LORA_TILE, d_rwkv), BF16).at[:ICLR_LORA].set(w_iclr_up[0, i].astype(BF16)) for i in range(2)]

    cc = jnp.zeros((MOD_ROWS, d), F32).at[0].set(c[0]).at[1].set(c_ctx)
    mod1 = _mod_call(cc, w_ada[0], b_ada, n_cols=2 * d, tn=tl["mod_tn"])
    c_bcast = jnp.broadcast_to(c[0][:, None], (d, tl["side_mod_tn"]))

    h_all = _norm_call(x2d, ctx2d, norm1, mod1, 0, 1, tm=tl["norm_tm"])
    za, mod2 = _inproj_shift_call(h_all, wi_t, mu, segments, w_ada[0], c_bcast, b_ada, t_lat=t_lat, ctx_len=ctx_len,
                                  tm=tl["shift_tm"], tn=tl["shift_tn"], n_sub=tl["shift_sub"],
                                  mod_col0=2 * d, mod_cols=4 * d, mod_tn=tl["side_mod_tn"])
    y_conv, wo = _inproj_conv_call(h_all, wi_t, conv_w[0], w_out[0], row0=rwkv_cols, t_lat=t_lat,
                                   tm=tl["conv_tm"], tc=tl["conv_tc"])
    n_lat_chunks, n_ctx_chunks = t_lat // CHUNK, ctx_len // CHUNK
    y_sum = None
    for i in range(2):
        y_sum = _scan_call(za, k_k, k_a, w0[0, i:i + 1], a0[0, i:i + 1], wd[i], wa[i], y_sum, rev=(i == 1),
                           n_lat_chunks=n_lat_chunks, n_ctx_chunks=n_ctx_chunks, d_rwkv=d_rwkv,
                           n_sub=tl["scan_sub"])
    y_rwkv = _readout_call(y_sum, za, k_a, r_k.reshape(1, d_rwkv), a0[0], wa[0], wa[1],
                           w_gate_up[0].astype(BF16), lnx_w, lnx_b, t_lat=t_lat, d_rwkv=d_rwkv,
                           tm=tl["readout_tm"])
    x1, h2 = _outproj_call(y_rwkv, y_conv, wo, x2d, (mod2, 0), norm2, (mod2, d), (mod2, 2 * d),
                           tm=tl["out_tm"], tn=tl["out_tn"])

    act, wdn = _ffn_up_call(h2, w_ffn_gate[0], w_ffn_up[0], w_ffn_down[0], tm=tl["up_tm"], tn=tl["up_tn"])
    x2 = _ffn_down_call(act, wdn, x1, (mod2, 3 * d), tm=tl["down_tm"], tn=tl["down_tn"])
    out = _final_norm_call(x2, norm_f[None, :], tm=tl["final_tm"])
    return out[None]
```

```python
import functools
import math

import jax
import jax.numpy as jnp
from jax import lax
from jax.experimental import pallas as pl
from jax.experimental.pallas import tpu as pltpu

F32 = jnp.float32
BF16 = jnp.bfloat16

SUBLANES = 8
BF16_SUBLANES = 16
MXU_WIDTH = 256
SEG_W = MXU_WIDTH
VMEM_LIMIT_BYTES = 56 * 1024 * 1024

HEAD = 64
GROUP = 128
CHUNK = 64
GRID_W = 64
DECAY_LORA = 96
ICLR_LORA = 96
GATE_LORA = 256
LORA_TILE = 128
LORA_PAD = 2 * LORA_TILE
NORM_EPS = 1e-6
LNX_EPS = 64e-5
EXP_M05 = math.exp(-0.5)


def _cparams(*sem):
    return pltpu.CompilerParams(dimension_semantics=sem, vmem_limit_bytes=VMEM_LIMIT_BYTES)


def _dot(a, b):
    return jnp.dot(a, b, preferred_element_type=F32)


def _bmm(a, b):
    return lax.dot_general(a, b, (((2,), (1,)), ((0,), (0,))), preferred_element_type=F32)


def _bmm_nt(a, b):
    return lax.dot_general(a, b, (((2,), (2,)), ((0,), (0,))), preferred_element_type=F32)


def _bmm_tn(a, b):
    return lax.dot_general(a, b, (((1,), (1,)), ((0,), (0,))), preferred_element_type=F32)


def _split2(x):
    hi = x.astype(BF16)
    lo = (x - hi.astype(F32)).astype(BF16)
    return hi, lo


def _sigmoid(x):
    return 1.0 / (1.0 + jnp.exp(-x))


def _head_sums(x):
    rows, d = x.shape
    n_seg = d // SEG_W
    xs = jnp.concatenate([x[:, s * SEG_W:(s + 1) * SEG_W] for s in range(n_seg)], axis=0)
    rr = lax.broadcasted_iota(jnp.int32, (SEG_W, SEG_W), 0) // HEAD
    cc = lax.broadcasted_iota(jnp.int32, (SEG_W, SEG_W), 1) // HEAD
    ones_seg = (rr == cc).astype(BF16)
    hi, lo = _split2(xs)
    ss = _dot(hi, ones_seg) + _dot(lo, ones_seg)
    return jnp.concatenate([ss[s * rows:(s + 1) * rows] for s in range(n_seg)], axis=1)


def _lora_tiles(la):
    return la[:, :LORA_TILE], la[:, DECAY_LORA:DECAY_LORA + LORA_TILE]


def _scan_chunk(lw, l_cum, a, kd, kk, v, r, ht0, *, rev, n_groups):
    def st(x):
        return jnp.stack([x[:, p * GROUP:(p + 1) * GROUP] for p in range(n_groups)], axis=0)

    lw_s, l_s, a_s, kd_s, kk_s, v_s, r_s = (st(t) for t in (lw, l_cum, a, kd, kk, v, r))
    ltot = l_s[:, 0:1, :] if rev else l_s[:, CHUNK - 1:CHUNK, :]
    e_l = jnp.exp(l_s)
    e_nl = jnp.exp(-l_s)
    e_lx = jnp.exp(l_s - lw_s)
    e_tl = jnp.exp(ltot - l_s)
    g_tot = jnp.exp(ltot)

    t_i = lax.broadcasted_iota(jnp.int32, (1, CHUNK, GROUP), 1)
    lane = lax.broadcasted_iota(jnp.int32, (1, CHUNK, GROUP), 2)
    s_i = lane & (HEAD - 1)
    head_of_lane = lane // HEAD
    strict = (s_i > t_i) if rev else (s_i < t_i)
    incl = (s_i >= t_i) if rev else (s_i <= t_i)
    eye = (s_i == t_i).astype(F32)
    rr = lax.broadcasted_iota(jnp.int32, (GROUP, GROUP), 0)
    cc = lax.broadcasted_iota(jnp.int32, (GROUP, GROUP), 1)
    same_head = (rr // HEAD) == (cc // HEAD)

    def bd(x):
        z = jnp.zeros_like(x)
        return jnp.concatenate([jnp.where(head_of_lane == g, x, z) for g in range(GROUP // HEAD)], axis=1)

    b_s = kk_s * a_s
    ah = (-kk_s * e_lx).astype(BF16)
    rh = (r_s * e_l).astype(BF16)
    bc = (b_s * e_nl).astype(BF16)
    kc = (kd_s * e_nl).astype(BF16)
    kt = (kd_s * e_tl).astype(BF16)
    bt = (b_s * e_tl).astype(BF16)
    v_b = v_s.astype(BF16)

    sc = _bmm_nt(jnp.concatenate([ah, rh], axis=1),
                 jnp.concatenate([bd(bc), bd(kc)], axis=1))
    m_ab = jnp.where(strict, sc[:, :CHUNK, :GROUP], 0.0)
    m_ak = jnp.where(strict, sc[:, :CHUNK, GROUP:], 0.0).astype(BF16)
    n_rb = jnp.where(incl, sc[:, CHUNK:, :GROUP], 0.0).astype(BF16)
    n_rk = jnp.where(incl, sc[:, CHUNK:, GROUP:], 0.0).astype(BF16)

    s_acc = eye + m_ab
    m_pow = m_ab.astype(BF16)
    m_pow = _bmm(m_pow, bd(m_pow))
    n_steps = int(math.log2(CHUNK)) - 1
    for j in range(n_steps):
        mb = m_pow.astype(BF16)
        if j < n_steps - 1:
            both = _bmm(mb, jnp.concatenate([bd(mb), bd(s_acc.astype(BF16))], axis=2))
            m_pow = both[:, :, :GROUP]
            s_acc = s_acc + both[:, :, GROUP:]
        else:
            s_acc = s_acc + _bmm(mb, bd(s_acc.astype(BF16)))
    t_inv = s_acc.astype(BF16)

    ht0_b = ht0.astype(BF16)
    v_bd = bd(v_b)
    from_h0 = _bmm_nt(jnp.concatenate([ah, rh], axis=1), ht0_b)
    w = from_h0[:, :CHUNK] + _bmm(m_ak, v_bd)
    u_b = _bmm(t_inv, bd(w.astype(BF16))).astype(BF16)
    y = from_h0[:, CHUNK:] + _bmm(jnp.concatenate([n_rk, n_rb], axis=2),
                                  jnp.concatenate([v_bd, bd(u_b)], axis=1))
    upd = _bmm_tn(jnp.concatenate([v_b, u_b], axis=1), jnp.concatenate([kt, bt], axis=1))
    ht1 = ht0 * g_tot + jnp.where(same_head[None], upd, 0.0)
    return jnp.concatenate([y[p] for p in range(n_groups)], axis=1), ht1


def _scan_body(k_ref, v_ref, r_ref, la_ref, kk_ref, ka_ref, w0_ref, a0_ref, wd_ref, wa_ref,
               *rest, rev, n_groups):
    y_other_ref, y_ref, h_ref = rest if len(rest) == 3 else (None,) + rest
    c = pl.program_id(0)

    @pl.when(c == 0)
    def _():
        h_ref[...] = jnp.zeros_like(h_ref)

    k = k_ref[...]
    v = v_ref[...]
    r = r_ref[...]
    la_dec, la_icl = _lora_tiles(la_ref[...])
    dec_pre = w0_ref[...] + _dot(jnp.tanh(la_dec).astype(BF16), wd_ref[...])
    icl_pre = a0_ref[...] + _dot(la_icl.astype(BF16), wa_ref[...])
    lw = -EXP_M05 * _sigmoid(dec_pre)
    a = _sigmoid(icl_pre)
    kd = k * (1.0 + (a - 1.0) * ka_ref[...])
    kkr = k * kk_ref[...]
    kk = kkr / jnp.maximum(jnp.sqrt(_head_sums(kkr * kkr)), 1e-12)

    rows = k.shape[0]
    row = lax.broadcasted_iota(jnp.int32, (rows, rows), 0)
    col = lax.broadcasted_iota(jnp.int32, (rows, rows), 1)
    in_order = (col >= row) if rev else (col <= row)
    tri = (in_order & ((row // CHUNK) == (col // CHUNK))).astype(BF16)
    lw_hi, lw_lo = _split2(lw)
    l_cum = _dot(tri, lw_hi) + _dot(tri, lw_lo)

    n_sub = rows // CHUNK
    ht = h_ref[...]
    ys = [None] * n_sub
    for s in (reversed(range(n_sub)) if rev else range(n_sub)):
        sl = slice(s * CHUNK, (s + 1) * CHUNK)
        ys[s], ht = _scan_chunk(lw[sl], l_cum[sl], a[sl], kd[sl], kk[sl], v[sl], r[sl], ht,
                                rev=rev, n_groups=n_groups)
    y = jnp.concatenate(ys, axis=0)
    y_ref[...] = y if y_other_ref is None else y + y_other_ref[...]
    h_ref[...] = ht


def _scan_call(za, kk, ka, w0, a0, wd, wa, y_other=None, *, rev, n_lat_chunks, n_ctx_chunks, d_rwkv, n_sub):
    assert n_lat_chunks % n_sub == 0 and n_ctx_chunks % n_sub == 0
    n_lat_blk, n_ctx_blk = n_lat_chunks // n_sub, n_ctx_chunks // n_sub
    n_blk = n_lat_blk + n_ctx_blk
    rows = n_sub * CHUNK
    n_groups = d_rwkv // GROUP
    lora_blk = 3 * d_rwkv // LORA_PAD
    assert CHUNK == HEAD and d_rwkv % GROUP == 0

    def blk_of(c):
        if rev:
            return n_blk - 1 - c
        return jnp.where(c < n_ctx_blk, n_lat_blk + c, c - n_ctx_blk)

    vec = pl.BlockSpec((1, d_rwkv), lambda c: (0, 0))
    mat = pl.BlockSpec((LORA_TILE, d_rwkv), lambda c: (0, 0))
    y_spec = pl.BlockSpec((rows, d_rwkv), lambda c: (blk_of(c), 0))
    others = () if y_other is None else (y_other,)
    return pl.pallas_call(
        functools.partial(_scan_body, rev=rev, n_groups=n_groups),
        grid=(n_blk,),
        in_specs=[
            pl.BlockSpec((rows, d_rwkv), lambda c: (blk_of(c), 0)),
            pl.BlockSpec((rows, d_rwkv), lambda c: (blk_of(c), 1)),
            pl.BlockSpec((rows, d_rwkv), lambda c: (blk_of(c), 2)),
            pl.BlockSpec((rows, LORA_PAD), lambda c: (blk_of(c), lora_blk)),
            vec, vec, vec, vec, mat, mat,
        ] + [y_spec] * len(others),
        out_specs=y_spec,
        out_shape=jax.ShapeDtypeStruct((n_blk * rows, d_rwkv), F32),
        scratch_shapes=[pltpu.VMEM((n_groups, GROUP, GROUP), F32)],
        compiler_params=_cparams("arbitrary"),
        name="wkv_scan_rev" if rev else "wkv_scan_fwd",
    )(za, za, za, za, kk, ka, w0, a0, wd, wa, *others)


MOD_ROWS = 8


def _mod_body(cc_ref, w_ref, b_ref, o_ref):
    cc = cc_ref[...]
    s = cc * _sigmoid(cc)
    s_hi, s_lo = _split2(s)
    w = w_ref[...].astype(BF16)
    o_ref[...] = _dot(s_hi, w) + _dot(s_lo, w) + b_ref[...]


def _mod_call(cc, w_ada, b_ada, *, n_cols, tn):
    d, n = w_ada.shape[0], n_cols
    assert n % tn == 0 and n <= w_ada.shape[1]
    return pl.pallas_call(
        _mod_body,
        grid=(n // tn,),
        in_specs=[pl.BlockSpec((MOD_ROWS, d), lambda j: (0, 0)),
                  pl.BlockSpec((d, tn), lambda j: (0, j)),
                  pl.BlockSpec((1, tn), lambda j: (0, j))],
        out_specs=pl.BlockSpec((MOD_ROWS, tn), lambda j: (0, j)),
        out_shape=jax.ShapeDtypeStruct((MOD_ROWS, n), F32),
        compiler_params=_cparams("arbitrary"),
        name="adaln_mod",
    )(cc, w_ada, b_ada)


def _norm_mod(xf, gain, shift, scale):
    y = xf * lax.rsqrt(jnp.mean(xf * xf, axis=-1, keepdims=True) + NORM_EPS) * gain
    return y * (1.0 + scale) + shift


def _norm_ctx_body(x_ref, ctx_ref, g_ref, sh_ref, sc_ref, o_ref, *, n_lat_blocks):
    i = pl.program_id(0)

    @pl.when(i < n_lat_blocks)
    def _():
        o_ref[...] = _norm_mod(x_ref[...], g_ref[...], sh_ref[0:1, :], sc_ref[0:1, :]).astype(o_ref.dtype)

    @pl.when(i >= n_lat_blocks)
    def _():
        o_ref[...] = _norm_mod(ctx_ref[...], g_ref[...], sh_ref[1:2, :], sc_ref[1:2, :]).astype(o_ref.dtype)


def _norm_call(x, ctx, gain, mod, shift_blk, scale_blk, *, tm):
    t, d = x.shape
    n_lat = t // tm
    vec = pl.BlockSpec((1, d), lambda i: (0, 0))
    sh = pl.BlockSpec((MOD_ROWS, d), lambda i: (0, shift_blk))
    sc = pl.BlockSpec((MOD_ROWS, d), lambda i: (0, scale_blk))
    tc = ctx.shape[0]
    n_ctx = tc // tm
    return pl.pallas_call(
        functools.partial(_norm_ctx_body, n_lat_blocks=n_lat), grid=(n_lat + n_ctx,),
        in_specs=[pl.BlockSpec((tm, d), lambda i: (jnp.minimum(i, n_lat - 1), 0)),
                  pl.BlockSpec((tm, d), lambda i: (jnp.maximum(i - n_lat, 0), 0)), vec, sh, sc],
        out_specs=pl.BlockSpec((tm, d), lambda i: (i, 0)),
        out_shape=jax.ShapeDtypeStruct((t + tc, d), BF16),
        compiler_params=_cparams("arbitrary"), name="norm_mod_ctx",
    )(x, ctx, gain, mod, mod)


def _neighbours(z, row0, t_lat, ctx_len):
    tm = z.shape[0]
    g = row0 + lax.broadcasted_iota(jnp.int32, (tm, 1), 0)
    is_ctx = g >= t_lat
    pos = jnp.where(is_ctx, g - t_lat, g & (GRID_W - 1))
    last = jnp.where(is_ctx, ctx_len - 1, GRID_W - 1)
    prev = jnp.where(pos == 0, 0.0, pltpu.roll(z, 1, 0))
    nxt = jnp.where(pos == last, 0.0, pltpu.roll(z, tm - 1, 0))
    return prev, nxt


def _dot_nt(a, b):
    return lax.dot_general(a, b, (((1,), (1,)), ((), ())), preferred_element_type=F32)


def _rows_spec(rows, width, row_of):
    return pl.BlockSpec((pl.Element(rows), pl.Element(width)),
                        lambda i, j: (pl.multiple_of(row_of(j), SUBLANES), 0))


def _inproj_shift_body(h_ref, wt_ref, mu_ref, wada_ref, cb_ref, bada_ref, o_ref, mod_ref, silu_ref,
                       *, t_lat, ctx_len, n_sub):
    @pl.when((pl.program_id(0) == 0) & (pl.program_id(1) == 0))
    def _():
        cb = cb_ref[...]
        silu_ref[...] = cb * _sigmoid(cb)

    d_model, mod_tn = wada_ref.shape
    n_acc = 16
    slab = d_model // n_acc
    parts = [jnp.sum((wada_ref[a * slab:(a + 1) * slab, :] * silu_ref[a * slab:(a + 1) * slab, :])
                     .reshape(slab // SUBLANES, SUBLANES, mod_tn), axis=0) for a in range(n_acc)]
    mod_row = jnp.sum(sum(parts), axis=0, keepdims=True) + bada_ref[...]
    mod_ref[...] = jnp.broadcast_to(mod_row, mod_ref.shape)

    w = wt_ref[...].astype(BF16)
    tm = h_ref.shape[0]
    ts = tm // n_sub
    for s in range(n_sub):
        rows = slice(s * ts, (s + 1) * ts)
        z = _dot_nt(h_ref[rows, :], w)
        prev, nxt = _neighbours(z, pl.program_id(0) * tm + s * ts, t_lat, ctx_len)
        o_ref[rows, :] = z + mu_ref[...] * (0.5 * (prev + nxt) - z)


def _inproj_shift_call(h_all, w_in_t, mu, segments, w_ada, c_bcast, b_ada, *, t_lat, ctx_len, tm, tn, n_sub,
                       mod_col0, mod_cols, mod_tn):
    t, d = h_all.shape
    assert tm % (n_sub * GRID_W) == 0
    assert all(r % SUBLANES == 0 and n % tn == 0 and r + n <= w_in_t.shape[0] for r, n in segments)
    tile0 = [sum(n for _, n in segments[:s]) // tn for s in range(len(segments) + 1)]
    n_tiles = tile0[-1]

    def row_of(j):
        return sum(jnp.where((j >= lo) & (j < hi), r + tn * (j - lo), 0)
                   for (r, _), lo, hi in zip(segments, tile0[:-1], tile0[1:]))

    d_model = w_ada.shape[0]
    n_mod = mod_cols // mod_tn
    assert mod_cols % mod_tn == 0 and mod_col0 % mod_tn == 0 and n_mod <= (t // tm) * n_tiles
    assert c_bcast.shape == (d_model, mod_tn)

    def mod_tile(i, j):
        return jnp.minimum(i * n_tiles + j, n_mod - 1)

    return pl.pallas_call(
        functools.partial(_inproj_shift_body, t_lat=t_lat, ctx_len=ctx_len, n_sub=n_sub),
        grid=(t // tm, n_tiles),
        in_specs=[pl.BlockSpec((tm, d), lambda i, j: (i, 0), pipeline_mode=pl.Buffered(1)),
                  _rows_spec(tn, d, row_of),
                  pl.BlockSpec((1, tn), lambda i, j: (0, j)),
                  pl.BlockSpec((d_model, mod_tn), lambda i, j: (0, mod_col0 // mod_tn + mod_tile(i, j))),
                  pl.BlockSpec((d_model, mod_tn), lambda i, j: (0, 0)),
                  pl.BlockSpec((1, mod_tn), lambda i, j: (0, mod_col0 // mod_tn + mod_tile(i, j)))],
        out_specs=[pl.BlockSpec((tm, tn), lambda i, j: (i, j)),
                   pl.BlockSpec((MOD_ROWS, mod_tn), lambda i, j: (0, mod_tile(i, j)))],
        out_shape=[jax.ShapeDtypeStruct((t, n_tiles * tn), F32), jax.ShapeDtypeStruct((MOD_ROWS, mod_cols), F32)],
        scratch_shapes=[pltpu.VMEM((d_model, mod_tn), F32)],
        compiler_params=_cparams("arbitrary", "arbitrary"), name="inproj_shift",
    )(h_all, w_in_t, mu, w_ada, c_bcast, b_ada)


def _inproj_conv_body(h_ref, wb_ref, wc_ref, wx_ref, cw_ref, wcast_ref, o_ref, wcast_o_ref, *, t_lat):
    wcast_o_ref[...] = wcast_ref[...].astype(BF16)
    wb, wc, wx = (w_ref[...].astype(BF16) for w_ref in (wb_ref, wc_ref, wx_ref))
    cw = cw_ref[...]
    tm = h_ref.shape[0]
    n_sub = 2
    ts = tm // n_sub
    assert ts % GRID_W == 0
    for s in range(n_sub):
        rows = slice(s * ts, (s + 1) * ts)
        h = h_ref[rows, :]
        u = _dot_nt(h, wc) * _dot_nt(h, wx)
        prev, nxt = _neighbours(u, pl.program_id(0) * tm + s * ts, t_lat, 1)
        conv = cw[0:1, :] * prev + cw[1:2, :] * u + cw[2:3, :] * nxt
        o_ref[rows, :] = (_dot_nt(h, wb) * conv).astype(o_ref.dtype)


def _inproj_conv_call(h_all, w_in_t, conv_w, w_cast, *, row0, t_lat, tm, tc):
    d = h_all.shape[1]
    d_conv = conv_w.shape[1]
    assert d_conv % tc == 0 and row0 % SUBLANES == 0 and row0 + 3 * d_conv <= w_in_t.shape[0]
    cw = jnp.pad(conv_w, ((0, MOD_ROWS - 3), (0, 0)))
    n_j = d_conv // tc
    c_in, c_out, c_shape = _cast_rows_specs(w_cast, (t_lat // tm) * n_j, lambda i, j: i * n_j + j)
    return pl.pallas_call(
        functools.partial(_inproj_conv_body, t_lat=t_lat),
        grid=(t_lat // tm, n_j),
        in_specs=[pl.BlockSpec((tm, d), lambda i, j: (i, 0)),
                  _rows_spec(tc, d, lambda j: row0 + tc * j),
                  _rows_spec(tc, d, lambda j: row0 + d_conv + tc * j),
                  _rows_spec(tc, d, lambda j: row0 + 2 * d_conv + tc * j),
                  pl.BlockSpec((MOD_ROWS, tc), lambda i, j: (0, j)),
                  c_in],
        out_specs=[pl.BlockSpec((tm, tc), lambda i, j: (i, j)), c_out],
        out_shape=[jax.ShapeDtypeStruct((t_lat, d_conv), BF16), c_shape],
        compiler_params=_cparams("arbitrary", "arbitrary"), name="inproj_conv",
    )(h_all, w_in_t, w_in_t, w_in_t, cw, w_cast)


def _readout_body(y_ref, k_ref, v_ref, r_ref, la_ref, gl_ref, ka_ref, rk_ref, a0_ref,
                  wa0_ref, wa1_ref, wg_ref, lw_ref, lb_ref, o_ref):
    la = _lora_tiles(la_ref[...])[1].astype(BF16)
    a0 = a0_ref[...]
    a_sum = _sigmoid(a0[0:1, :] + _dot(la, wa0_ref[...])) + _sigmoid(a0[1:2, :] + _dot(la, wa1_ref[...]))
    coef = r_ref[...] * rk_ref[...] * k_ref[...] * (2.0 + (a_sum - 2.0) * ka_ref[...])
    y = y_ref[...]
    g = _dot(_sigmoid(gl_ref[...]).astype(BF16), wg_ref[...])

    yc = y - _head_sums(y) * (1.0 / HEAD)
    var = _head_sums(yc * yc) * (1.0 / HEAD)
    yn = yc * lax.rsqrt(var + LNX_EPS) * lw_ref[...] + lb_ref[...]
    bonus = _head_sums(coef) * v_ref[...]
    o_ref[...] = ((yn + bonus) * g).astype(o_ref.dtype)


def _readout_call(y, za, k_a, r_k, a0, wa0, wa1, w_gate, lnx_w, lnx_b, *, t_lat, d_rwkv, tm):
    row = lambda jb: pl.BlockSpec((tm, d_rwkv), lambda i: (i, jb))
    vec = pl.BlockSpec((1, d_rwkv), lambda i: (0, 0))
    mat = pl.BlockSpec((LORA_TILE, d_rwkv), lambda i: (0, 0))
    lora_blk = 3 * d_rwkv // LORA_PAD
    assert GATE_LORA == LORA_PAD
    return pl.pallas_call(
        _readout_body, grid=(t_lat // tm,),
        in_specs=[row(0), row(0), row(1), row(2),
                  pl.BlockSpec((tm, LORA_PAD), lambda i: (i, lora_blk)),
                  pl.BlockSpec((tm, GATE_LORA), lambda i: (i, lora_blk + 1)),
                  vec, vec, pl.BlockSpec((2, d_rwkv), lambda i: (0, 0)),
                  mat, mat, pl.BlockSpec((GATE_LORA, d_rwkv), lambda i: (0, 0)), vec, vec],
        out_specs=pl.BlockSpec((tm, d_rwkv), lambda i: (i, 0)),
        out_shape=jax.ShapeDtypeStruct((t_lat, d_rwkv), BF16),
        compiler_params=_cparams("arbitrary"), name="rwkv_readout",
    )(y, za, za, za, za, za, k_a, r_k, a0, wa0, wa1, w_gate, lnx_w, lnx_b)


def _outproj_body(ya_ref, yb_ref, wa_ref, wb_ref, x_ref, g_ref, n2_ref, sh_ref, sc_ref, o_ref, h_ref, rows_ref):
    j = pl.program_id(1)
    n_tiles, _, tn = rows_ref.shape
    acc = _dot(ya_ref[...], wa_ref[...]) + _dot(yb_ref[...], wb_ref[...])
    x1 = x_ref[...] + g_ref[0:1, :] * acc
    o_ref[...] = x1
    rows_ref[j] = x1

    @pl.when(j == n_tiles - 1)
    def _():
        ss = sum(jnp.sum(rows_ref[s] * rows_ref[s], axis=-1, keepdims=True) for s in range(n_tiles))
        rs = lax.rsqrt(ss * (1.0 / (n_tiles * tn)) + NORM_EPS)
        for s in range(n_tiles):
            sl = slice(s * tn, (s + 1) * tn)
            y = rows_ref[s] * rs * n2_ref[:, sl]
            h_ref[:, sl] = (y * (1.0 + sc_ref[0:1, sl]) + sh_ref[0:1, sl]).astype(h_ref.dtype)


def _outproj_call(ya, yb, w, x, gate, gain2, shift, scale, *, tm, tn):
    (gate_arr, gate_col), (shift_arr, shift_col), (scale_arr, scale_col) = gate, shift, scale
    t, da = ya.shape
    db = yb.shape[1]
    n = w.shape[1]
    assert da == db
    return pl.pallas_call(
        _outproj_body, grid=(t // tm, n // tn),
        in_specs=[pl.BlockSpec((tm, da), lambda i, j: (i, 0)),
                  pl.BlockSpec((tm, db), lambda i, j: (i, 0)),
                  pl.BlockSpec((da, tn), lambda i, j: (0, j)),
                  pl.BlockSpec((db, tn), lambda i, j: (1, j)),
                  pl.BlockSpec((tm, tn), lambda i, j: (i, j)),
                  pl.BlockSpec((MOD_ROWS, tn), lambda i, j: (0, gate_col // tn + j)),
                  pl.BlockSpec((1, n), lambda i, j: (0, 0)),
                  pl.BlockSpec((MOD_ROWS, n), lambda i, j: (0, shift_col // n)),
                  pl.BlockSpec((MOD_ROWS, n), lambda i, j: (0, scale_col // n))],
        out_specs=[pl.BlockSpec((tm, tn), lambda i, j: (i, j)),
                   pl.BlockSpec((tm, n), lambda i, j: (i, 0))],
        out_shape=[jax.ShapeDtypeStruct((t, n), F32), jax.ShapeDtypeStruct((t, n), BF16)],
        scratch_shapes=[pltpu.VMEM((n // tn, tm, tn), F32)],
        compiler_params=_cparams("arbitrary", "arbitrary"), name="outproj_residual_norm",
    )(ya, yb, w, w, x, gate_arr, gain2, shift_arr, scale_arr)


def _cast_rows_specs(w, n_steps, step_of):
    rows, cols = w.shape
    assert rows % n_steps == 0 and (rows // n_steps) % BF16_SUBLANES == 0
    blk = (rows // n_steps, cols)
    return (pl.BlockSpec(blk, lambda i, j: (step_of(i, j), 0)), pl.BlockSpec(blk, lambda i, j: (step_of(i, j), 0)),
            jax.ShapeDtypeStruct(w.shape, BF16))


def _ffn_up_body(h_ref, wg_ref, wu_ref, wcast_ref, o_ref, wcast_o_ref):
    h = h_ref[...]
    g = _dot(h, wg_ref[...].astype(BF16))
    u = _dot(h, wu_ref[...].astype(BF16))
    o_ref[...] = (g * _sigmoid(g) * u).astype(o_ref.dtype)
    wcast_o_ref[...] = wcast_ref[...].astype(BF16)


def _ffn_up_call(h, wg, wu, w_cast, *, tm, tn):
    t, d = h.shape
    n = wg.shape[1]
    n_j = n // tn
    c_in, c_out, c_shape = _cast_rows_specs(w_cast, (t // tm) * n_j, lambda i, j: i * n_j + j)
    return pl.pallas_call(
        _ffn_up_body, grid=(t // tm, n_j),
        in_specs=[pl.BlockSpec((tm, d), lambda i, j: (i, 0), pipeline_mode=pl.Buffered(1)),
                  pl.BlockSpec((d, tn), lambda i, j: (0, j)),
                  pl.BlockSpec((d, tn), lambda i, j: (0, j)),
                  c_in],
        out_specs=[pl.BlockSpec((tm, tn), lambda i, j: (i, j)), c_out],
        out_shape=[jax.ShapeDtypeStruct((t, n), BF16), c_shape],
        compiler_params=_cparams("arbitrary", "arbitrary"), name="ffn_gate_up",
    )(h, wg, wu, w_cast)


def _ffn_down_body(a_ref, w_ref, x_ref, g_ref, o_ref):
    o_ref[...] = x_ref[...] + g_ref[0:1, :] * _dot(a_ref[...], w_ref[...])


def _ffn_down_call(act, w, x, gate, *, tm, tn):
    mod, gate_col = gate
    gate_blk0 = gate_col // tn
    t, kdim = act.shape
    d = w.shape[1]
    return pl.pallas_call(
        _ffn_down_body, grid=(t // tm, d // tn),
        in_specs=[pl.BlockSpec((tm, kdim), lambda i, j: (i, 0)),
                  pl.BlockSpec((kdim, tn), lambda i, j: (0, j)),
                  pl.BlockSpec((tm, tn), lambda i, j: (i, j)),
                  pl.BlockSpec((MOD_ROWS, tn), lambda i, j: (0, gate_blk0 + j))],
        out_specs=pl.BlockSpec((tm, tn), lambda i, j: (i, j)),
        out_shape=jax.ShapeDtypeStruct((t, d), F32),
        compiler_params=_cparams("arbitrary", "arbitrary"), name="ffn_down_residual",
    )(act, w, x, mod)


def _final_norm_body(x_ref, g_ref, o_ref):
    x = x_ref[...]
    o_ref[...] = x * lax.rsqrt(jnp.mean(x * x, axis=-1, keepdims=True) + NORM_EPS) * g_ref[...]


def _final_norm_call(x, gain, *, tm):
    t, d = x.shape
    return pl.pallas_call(
        _final_norm_body, grid=(t // tm,),
        in_specs=[pl.BlockSpec((tm, d), lambda i: (i, 0)), pl.BlockSpec((1, d), lambda i: (0, 0))],
        out_specs=pl.BlockSpec((tm, d), lambda i: (i, 0)),
        out_shape=jax.ShapeDtypeStruct((t, d), F32),
        compiler_params=_cparams("arbitrary"), name="final_norm",
    )(x, gain)


def _tiles():
    return dict(mod_tn=1024, norm_tm=256, shift_tm=2816, shift_tn=256, shift_sub=4, conv_tm=1024, conv_tc=256,
                side_mod_tn=256, scan_sub=4, readout_tm=256, out_tm=512, out_tn=1024, up_tm=2048, up_tn=256,
                down_tm=512, down_tn=512, final_tm=512)


def kernel(x, c, ctx, c_ctx, w_ada, b_ada, norm1, w_in, mu_shift, k_k, k_a, r_k, w0, w_decay_up, a0, w_iclr_up, w_gate_up, lnx_w, lnx_b, conv_w, w_out, norm2, w_ffn_gate, w_ffn_up, w_ffn_down, norm_f):
    assert x.shape[0] == 1 and w_ada.shape[0] == 1, "single batch element, single layer"
    tl = _tiles()
    t_lat, d = x.shape[1], x.shape[2]
    ctx_len = ctx.shape[1]
    d_rwkv = k_k.shape[1]
    d_conv = conv_w.shape[2]
    d_ff = w_ffn_gate.shape[2]
    assert t_lat % GRID_W == 0 and ctx_len % CHUNK == 0 and d_rwkv % SEG_W == 0
    x2d, ctx2d = x[0], ctx[0]

    off_lora = 2 * d_rwkv
    off_r = off_lora + DECAY_LORA + ICLR_LORA
    off_gl = off_r + d_rwkv
    rwkv_cols = off_gl + GATE_LORA
    wi_t = jnp.swapaxes(w_in, 1, 2)[0]
    segments = [(0, off_lora), (off_r, d_rwkv), (off_lora, LORA_PAD), (off_gl, GATE_LORA)]
    ms = mu_shift[0]
    mu = jnp.concatenate([ms[r0:r0 + n] for r0, n in segments])[None, :]
    wd = [jnp.zeros((LORA_TILE, d_rwkv), BF16).at[:DECAY_LORA].set(w_decay_up[0, i].astype(BF16)) for i in range(2)]
    wa = [jnp.zeros((LORA_TILE, d_rwkv), BF16).at[:ICLR_LORA].set(w_iclr_up[0, i].astype(BF16)) for i in range(2)]

    cc = jnp.zeros((MOD_ROWS, d), F32).at[0].set(c[0]).at[1].set(c_ctx)
    mod1 = _mod_call(cc, w_ada[0], b_ada, n_cols=2 * d, tn=tl["mod_tn"])
    c_bcast = jnp.broadcast_to(c[0][:, None], (d, tl["side_mod_tn"]))

    h_all = _norm_call(x2d, ctx2d, norm1, mod1, 0, 1, tm=tl["norm_tm"])
    za, mod2 = _inproj_shift_call(h_all, wi_t, mu, segments, w_ada[0], c_bcast, b_ada, t_lat=t_lat, ctx_len=ctx_len,
                                  tm=tl["shift_tm"], tn=tl["shift_tn"], n_sub=tl["shift_sub"],
                                  mod_col0=2 * d, mod_cols=4 * d, mod_tn=tl["side_mod_tn"])
    y_conv, wo = _inproj_conv_call(h_all, wi_t, conv_w[0], w_out[0], row0=rwkv_cols, t_lat=t_lat,
                                   tm=tl["conv_tm"], tc=tl["conv_tc"])
    n_lat_chunks, n_ctx_chunks = t_lat // CHUNK, ctx_len // CHUNK
    y_sum = None
    for i in range(2):
        y_sum = _scan_call(za, k_k, k_a, w0[0, i:i + 1], a0[0, i:i + 1], wd[i], wa[i], y_sum, rev=(i == 1),
                           n_lat_chunks=n_lat_chunks, n_ctx_chunks=n_ctx_chunks, d_rwkv=d_rwkv,
                           n_sub=tl["scan_sub"])
    y_rwkv = _readout_call(y_sum, za, k_a, r_k.reshape(1, d_rwkv), a0[0], wa[0], wa[1],
                           w_gate_up[0].astype(BF16), lnx_w, lnx_b, t_lat=t_lat, d_rwkv=d_rwkv,
                           tm=tl["readout_tm"])
    x1, h2 = _outproj_call(y_rwkv, y_conv, wo, x2d, (mod2, 0), norm2, (mod2, d), (mod2, 2 * d),
                           tm=tl["out_tm"], tn=tl["out_tn"])

    act, wdn = _ffn_up_call(h2, w_ffn_gate[0], w_ffn_up[0], w_ffn_down[0], tm=tl["up_tm"], tn=tl["up_tn"])
    x2 = _ffn_down_call(act, wdn, x1, (mod2, 3 * d), tm=tl["down_tm"], tn=tl["down_tn"])
    out = _final_norm_call(x2, norm_f[None, :], tm=tl["final_tm"])
    return out[None]
```
